```python
import math
import jax, jax.numpy as jnp
from jax import lax
import numpy as np

D_MODEL = 1024
BATCH = 2
SEQ = 8192
DEPTH = 2

CHUNK = 64
Q_BLOCK = 128
HEAD_DIM = 64
N_DIFF_HEADS = D_MODEL // (2 * HEAD_DIM)
ROT_DIM = HEAD_DIM // 4
ROPE_THETA = 500000.0
CONV_WIDTH = 31
N_GROUPS = 4
EXPERTS_PER_GROUP = 8
N_EXPERTS = N_GROUPS * EXPERTS_PER_GROUP
EXPERT_TOPK = 2
D_EXPERT = D_MODEL // 2
N_MIXERS = 2
N_A = (DEPTH + 1) // 2
N_B = DEPTH // 2
EPS = 1e-6

kernel_name = "hybrid_diffattn_conformer_hmoe"


def rms_norm(x, g, eps=EPS):
    xf = x.astype(jnp.float32)
    y = xf * lax.rsqrt(jnp.mean(xf * xf, axis=-1, keepdims=True) + eps)
    return (y * g.astype(jnp.float32)).astype(x.dtype)


def layer_norm(x, g, b, eps=EPS):
    xf = x.astype(jnp.float32)
    mu = jnp.mean(xf, axis=-1, keepdims=True)
    xc = xf - mu
    var = jnp.mean(xc * xc, axis=-1, keepdims=True)
    y = xc * lax.rsqrt(var + eps) * g.astype(jnp.float32) + b.astype(jnp.float32)
    return y.astype(x.dtype)


def rope_tables(seq, dtype):
    pos = jnp.arange(seq, dtype=jnp.float32)
    inv_freq = ROPE_THETA ** (-jnp.arange(0, ROT_DIM, 2, dtype=jnp.float32) / ROT_DIM)
    ang = pos[:, None] * inv_freq[None, :]
    return jnp.cos(ang).astype(dtype), jnp.sin(ang).astype(dtype)


def partial_rope(x, cos, sin):
    c = cos[None, :, None, None, :]
    s = sin[None, :, None, None, :]
    half = ROT_DIM // 2
    x1 = x[..., :half]
    x2 = x[..., half:ROT_DIM]
    return jnp.concatenate([x1 * c - x2 * s, x2 * c + x1 * s, x[..., ROT_DIM:]], axis=-1)


def diff_attention(h, w_in, q_gain, k_gain, lq1, lk1, lq2, lk2, subln_g, w_out, lambda_init, cos, sin):
    B, S, D = h.shape
    H, DH = N_DIFF_HEADS, HEAD_DIM
    qkv = h @ w_in
    q = qkv[..., :D].reshape(B, S, H, 2, DH)
    k = qkv[..., D:2 * D].reshape(B, S, H, 2, DH)
    v = qkv[..., 2 * D:].reshape(B, S, H, 2 * DH)
    q = partial_rope(rms_norm(q, q_gain), cos, sin)
    k = partial_rope(rms_norm(k, k_gain), cos, sin)
    f32 = jnp.float32
    lam = (jnp.exp(jnp.sum(lq1.astype(f32) * lk1.astype(f32)))
           - jnp.exp(jnp.sum(lq2.astype(f32) * lk2.astype(f32))) + lambda_init)
    nb = S // Q_BLOCK
    q_blocks = jnp.moveaxis(q.reshape(B, nb, Q_BLOCK, H, 2, DH), 1, 0)
    key_chunk = jnp.arange(S) // CHUNK
    scale = DH ** -0.5

    def attend_block(args):
        qb, bi = args
        q_chunk = (bi * Q_BLOCK + jnp.arange(Q_BLOCK)) // CHUNK
        mask = key_chunk[None, :] <= q_chunk[:, None]
        s = jnp.einsum('bqhcd,bkhcd->bhcqk', qb, k).astype(f32) * scale
        s = jnp.where(mask, s, -jnp.inf)
        p = jax.nn.softmax(s, axis=-1)
        a = p[:, :, 0] - lam * p[:, :, 1]
        return jnp.einsum('bhqk,bkhe->bqhe', a.astype(v.dtype), v)

    o = lax.map(attend_block, (q_blocks, jnp.arange(nb)))
    o = jnp.moveaxis(o, 0, 1).reshape(B, S, H, 2 * DH)
    o = rms_norm(o, subln_g) * (1.0 - lambda_init)
    return o.reshape(B, S, D) @ w_out


def conformer_conv(h, w_in, b_in, dw_w, dw_b, ln_g, ln_b, w_out, b_out):
    D = h.shape[-1]
    a, g = jnp.split(h @ w_in + b_in, 2, axis=-1)
    u = a * jax.nn.sigmoid(g)
    u = lax.conv_general_dilated(
        u, dw_w[:, None, :].astype(u.dtype), window_strides=(1,),
        padding=[(CONV_WIDTH - 1, 0)],
        dimension_numbers=('NWC', 'WIO', 'NWC'),
        feature_group_count=D) + dw_b
    u = jax.nn.silu(layer_norm(u, ln_g, ln_b))
    return u @ w_out + b_out


def hier_moe(h, w_rg, b_rg, w_re, b_re, w_gate, w_up, w_down):
    B, S, D = h.shape
    t = h.reshape(B * S, D)
    f32 = jnp.float32
    g_prob = jax.nn.softmax((t @ w_rg + b_rg).astype(f32), axis=-1)
    g_p, g_idx = lax.top_k(g_prob, 1)
    e_all = (jnp.einsum('nd,gde->nge', t, w_re) + b_re).astype(f32)
    e_logits = jnp.einsum('nge,ng->ne', e_all, jax.nn.one_hot(g_idx[:, 0], N_GROUPS, dtype=f32))
    e_top, e_idx = lax.top_k(e_logits, EXPERT_TOPK)
    w = g_p * jax.nn.softmax(e_top, axis=-1)
    flat_id = g_idx * EXPERTS_PER_GROUP + e_idx
    combine = jnp.sum(jax.nn.one_hot(flat_id, N_EXPERTS, dtype=f32) * w[..., None], axis=1)
    combine = combine.astype(t.dtype)
    y = jnp.zeros_like(t)
    for e in range(N_EXPERTS):
        hid = jax.nn.silu(t @ w_gate[e]) * (t @ w_up[e])
        y = y + combine[:, e:e + 1] * (hid @ w_down[e])
    return y.reshape(B, S, D)


def setup_inputs(seed: int = 0) -> dict:
    key = jax.random.key(seed)
    ks = iter(jax.random.split(key, 40))
    D, DH, F, G, E, NE = D_MODEL, HEAD_DIM, D_EXPERT, N_GROUPS, EXPERTS_PER_GROUP, N_EXPERTS

    def nrm(shape, scale):
        return jax.random.normal(next(ks), shape, jnp.float32) * scale

    def gain(shape):
        return 1.0 + nrm(shape, 0.02)

    return {
        "x": nrm((BATCH, SEQ, D), 1.0),
        "attn_norm": gain((N_A, D)),
        "diff_w_in": nrm((N_A, D, 3 * D), D ** -0.5),
        "diff_q_gain": gain((N_A, DH)),
        "diff_k_gain": gain((N_A, DH)),
        "diff_lambda_q1": nrm((N_A, DH), 0.1),
        "diff_lambda_k1": nrm((N_A, DH), 0.1),
        "diff_lambda_q2": nrm((N_A, DH), 0.1),
        "diff_lambda_k2": nrm((N_A, DH), 0.1),
        "diff_subln": gain((N_A, 2 * DH)),
        "diff_w_out": nrm((N_A, D, D), D ** -0.5),
        "conv_norm": gain((N_B, D)),
        "conv_w_in": nrm((N_B, D, 2 * D), D ** -0.5),
        "conv_b_in": nrm((N_B, 2 * D), 0.02),
        "conv_dw_w": nrm((N_B, CONV_WIDTH, D), CONV_WIDTH ** -0.5),
        "conv_dw_b": nrm((N_B, D), 0.02),
        "conv_ln_g": gain((N_B, D)),
        "conv_ln_b": nrm((N_B, D), 0.02),
        "conv_w_out": nrm((N_B, D, D), D ** -0.5),
        "conv_b_out": nrm((N_B, D), 0.02),
        "ffn_norm": gain((DEPTH, D)),
        "router_group_w": nrm((DEPTH, D, G), D ** -0.5),
        "router_group_b": nrm((DEPTH, G), 0.01),
        "router_expert_w": nrm((DEPTH, G, D, E), D ** -0.5),
        "router_expert_b": nrm((DEPTH, G, E), 0.01),
        "moe_w_gate": nrm((DEPTH, NE, D, F), D ** -0.5),
        "moe_w_up": nrm((DEPTH, NE, D, F), D ** -0.5),
        "moe_w_down": nrm((DEPTH, NE, F, D), F ** -0.5),
    }


def reference(x, attn_norm, diff_w_in, diff_q_gain, diff_k_gain, diff_lambda_q1, diff_lambda_k1,
              diff_lambda_q2, diff_lambda_k2, diff_subln, diff_w_out,
              conv_norm, conv_w_in, conv_b_in, conv_dw_w, conv_dw_b, conv_ln_g, conv_ln_b,
              conv_w_out, conv_b_out,
              ffn_norm, router_group_w, router_group_b, router_expert_w, router_expert_b,
              moe_w_gate, moe_w_up, moe_w_down):
    cos, sin = rope_tables(x.shape[1], x.dtype)
    for i in range(DEPTH):
        j = i // N_MIXERS
        if i % N_MIXERS == 0:
            lambda_init = 0.8 - 0.6 * math.exp(-0.3 * i)
            x = x + diff_attention(
                rms_norm(x, attn_norm[j]), diff_w_in[j], diff_q_gain[j], diff_k_gain[j],
                diff_lambda_q1[j], diff_lambda_k1[j], diff_lambda_q2[j], diff_lambda_k2[j],
                diff_subln[j], diff_w_out[j], lambda_init, cos, sin)
        else:
            x = x + conformer_conv(
                rms_norm(x, conv_norm[j]), conv_w_in[j], conv_b_in[j], conv_dw_w[j], conv_dw_b[j],
                conv_ln_g[j], conv_ln_b[j], conv_w_out[j], conv_b_out[j])
        x = x + hier_moe(rms_norm(x, ffn_norm[i]), router_group_w[i], router_group_b[i],
                         router_expert_w[i], router_expert_b[i],
                         moe_w_gate[i], moe_w_up[i], moe_w_down[i])
    return x
```

```python
import functools
import math

import jax
import jax.numpy as jnp
from jax import lax
from jax.experimental import pallas as pl
from jax.experimental.pallas import tpu as pltpu

F32 = jnp.float32
BF16 = jnp.bfloat16
I32 = jnp.int32

D_MODEL = 1024
HEAD_DIM = 64
HEAD_W = 2 * HEAD_DIM
N_HEADS = D_MODEL // HEAD_W
ROT_DIM = HEAD_DIM // 4
ROPE_THETA = 500000.0
CHUNK = 64
CONV_WIDTH = 31
CONV_HALO = 32
N_GROUPS = 4
EXPERTS_PER_GROUP = 8
N_EXPERTS = N_GROUPS * EXPERTS_PER_GROUP
D_EXPERT = D_MODEL // 2
EPS = 1e-6
ROUTER_ROWS = 64

ATTN_TILE = 512
ROW_TILE = 512
CONV_TILE = 256
ROUTER_TILE = 512
MOE_ROW_TILE = 256
DISPATCH_TILE = 256
COMBINE_TILE = 256
VMEM_LIMIT = 48 * 1024 * 1024

_NT = (((1,), (1,)), ((), ()))


def _rmsnorm_rows(x, g):
    return x * lax.rsqrt(jnp.mean(x * x, axis=-1, keepdims=True) + EPS) * g


def _qkv_kernel(x_ref, g_ref, w_ref, qg_ref, kg_ref, cos_ref, sinp_ref, sinm_ref,
                q_ref, k_ref, vt_ref, *, col_tile):
    tm = x_ref.shape[0]
    h = _rmsnorm_rows(x_ref[...], g_ref[...]).astype(BF16)
    lane = lax.broadcasted_iota(I32, (tm, HEAD_W), 1)
    lo = lane < HEAD_DIM
    cosv, sinp, sinm = cos_ref[...], sinp_ref[...], sinm_ref[...]

    def norm_rope(y, gain, scale):
        y2 = y * y
        s_lo = jnp.sum(jnp.where(lo, y2, 0.0), axis=-1, keepdims=True)
        s_hi = jnp.sum(y2, axis=-1, keepdims=True) - s_lo
        r = jnp.where(lo, lax.rsqrt(s_lo * (1.0 / HEAD_DIM) + EPS), lax.rsqrt(s_hi * (1.0 / HEAD_DIM) + EPS))
        yn = y * r * gain
        half = ROT_DIM // 2
        out = yn * cosv + pltpu.roll(yn, half, 1) * sinp + pltpu.roll(yn, HEAD_W - half, 1) * sinm
        return out * scale

    for c in range(3 * D_MODEL // col_tile):
        y = jnp.dot(h, w_ref[:, c * col_tile:(c + 1) * col_tile], preferred_element_type=F32)
        for s in range(col_tile // HEAD_W):
            col = c * col_tile + s * HEAD_W
            ys = y[:, s * HEAD_W:(s + 1) * HEAD_W]
            if col < D_MODEL:
                q_ref[:, col:col + HEAD_W] = norm_rope(ys, qg_ref[...], HEAD_DIM ** -0.5).astype(BF16)
            elif col < 2 * D_MODEL:
                k_ref[:, col - D_MODEL:col - D_MODEL + HEAD_W] = norm_rope(ys, kg_ref[...], 1.0).astype(BF16)
            else:
                vt_ref[col - 2 * D_MODEL:col - 2 * D_MODEL + HEAD_W, :] = ys.T.astype(BF16)


def _qkv_project(x2, g, w_bf, q_gain, k_gain, cos_t, sinp_t, sinm_t, batch, seq):
    n = x2.shape[0]
    tm = ATTN_TILE
    tiles_per_seq = seq // tm
    row = lambda i: (i, 0)
    fixed = lambda i: (0, 0)
    tab = lambda i: (i % tiles_per_seq, 0)
    return pl.pallas_call(
        functools.partial(_qkv_kernel, col_tile=512),
        grid=(n // tm,),
        in_specs=[
            pl.BlockSpec((tm, D_MODEL), row),
            pl.BlockSpec((1, D_MODEL), fixed),
            pl.BlockSpec((D_MODEL, 3 * D_MODEL), fixed),
            pl.BlockSpec((1, HEAD_W), fixed),
            pl.BlockSpec((1, HEAD_W), fixed),
            pl.BlockSpec((tm, HEAD_W), tab),
            pl.BlockSpec((tm, HEAD_W), tab),
            pl.BlockSpec((tm, HEAD_W), tab),
        ],
        out_specs=[
            pl.BlockSpec((tm, D_MODEL), row),
            pl.BlockSpec((tm, D_MODEL), row),
            pl.BlockSpec((None, None, D_MODEL, tm), lambda i: (i // tiles_per_seq, i % tiles_per_seq, 0, 0)),
        ],
        out_shape=[
            jax.ShapeDtypeStruct((n, D_MODEL), BF16),
            jax.ShapeDtypeStruct((n, D_MODEL), BF16),
            jax.ShapeDtypeStruct((batch, tiles_per_seq, D_MODEL, tm), BF16),
        ],
        compiler_params=pltpu.CompilerParams(dimension_semantics=("arbitrary",), vmem_limit_bytes=VMEM_LIMIT),
        name="qkv_project",
    )(x2, g, w_bf, q_gain, k_gain, cos_t, sinp_t, sinm_t)


def _attn_kernel(lam_ref, q_ref, k_ref, vt_ref, g_ref, o_ref, m_ref, l_ref, acc_ref, *, lambda_init):
    t = q_ref.shape[0]
    i = pl.program_id(2)
    q = q_ref[...]
    lane = lax.broadcasted_iota(I32, (t, HEAD_W), 1)
    zero = jnp.zeros_like(q)
    q_comp = (jnp.where(lane < HEAD_DIM, q, zero), jnp.where(lane >= HEAD_DIM, q, zero))

    m_ref[...] = jnp.full(m_ref.shape, -jnp.inf, F32)
    l_ref[...] = jnp.zeros(l_ref.shape, F32)
    acc_ref[...] = jnp.zeros(acc_ref.shape, F32)

    def step(j, masked):
        kt = k_ref[pl.ds(pl.multiple_of(j * t, t), t), :]
        vt = vt_ref[j]
        if masked:
            key_chunk = lax.broadcasted_iota(I32, (t, t), 0) // CHUNK
            qry_chunk = lax.broadcasted_iota(I32, (t, t), 1) // CHUNK
            visible = key_chunk <= qry_chunk
        for c in range(2):
            s = lax.dot_general(kt, q_comp[c], _NT, preferred_element_type=F32)
            if masked:
                s = jnp.where(visible, s, -jnp.inf)
            m_old = m_ref[c]
            m_new = jnp.maximum(m_old, jnp.max(s, axis=0, keepdims=True))
            alpha = jnp.exp(m_old - m_new)
            p = jnp.exp(s - m_new)
            l_ref[c] = alpha * l_ref[c] + jnp.sum(p, axis=0, keepdims=True)
            acc_ref[c] = alpha * acc_ref[c] + jnp.dot(vt, p.astype(BF16), preferred_element_type=F32)
            m_ref[c] = m_new

    def body(j, carry):
        step(j, False)
        return carry

    lax.fori_loop(0, i, body, 0)
    step(i, True)

    lam_p = lam_ref[...]
    lam = (jnp.exp(jnp.sum(lam_p[0:1] * lam_p[1:2], axis=-1, keepdims=True))
           - jnp.exp(jnp.sum(lam_p[2:3] * lam_p[3:4], axis=-1, keepdims=True)) + lambda_init)
    ot = acc_ref[0] * (1.0 / l_ref[0]) - lam * (acc_ref[1] * (1.0 / l_ref[1]))
    ot = ot * lax.rsqrt(jnp.mean(ot * ot, axis=0, keepdims=True) + EPS)
    o_ref[...] = (ot.T * g_ref[...] * (1.0 - lambda_init)).astype(o_ref.dtype)


def _diff_attention(q3, k3, vt4, lam_params, subln_g, lambda_init):
    batch, seq, _ = q3.shape
    t = ATTN_TILE
    nq = seq // t
    return pl.pallas_call(
        functools.partial(_attn_kernel, lambda_init=lambda_init),
        grid=(batch, N_HEADS, nq),
        in_specs=[
            pl.BlockSpec((4, HEAD_DIM), lambda b, h, i: (0, 0)),
            pl.BlockSpec((None, t, HEAD_W), lambda b, h, i: (b, i, h)),
            pl.BlockSpec((None, seq, HEAD_W), lambda b, h, i: (b, 0, h)),
            pl.BlockSpec((None, nq, HEAD_W, t), lambda b, h, i: (b, 0, h, 0)),
            pl.BlockSpec((1, HEAD_W), lambda b, h, i: (0, 0)),
        ],
        out_specs=pl.BlockSpec((None, t, HEAD_W), lambda b, h, i: (b, i, h)),
        out_shape=jax.ShapeDtypeStruct((batch, seq, D_MODEL), BF16),
        scratch_shapes=[
            pltpu.VMEM((2, 1, t), F32),
            pltpu.VMEM((2, 1, t), F32),
            pltpu.VMEM((2, HEAD_W, t), F32),
        ],
        compiler_params=pltpu.CompilerParams(
            dimension_semantics=("arbitrary", "arbitrary", "arbitrary"), vmem_limit_bytes=VMEM_LIMIT),
        name="diff_attention",
    )(lam_params, q3, k3, vt4, subln_g)


def _proj_residual_kernel(x_ref, y_ref, w_ref, o_ref):
    o_ref[...] = x_ref[...] + jnp.dot(y_ref[...], w_ref[...], preferred_element_type=F32)


def _proj_residual(x2, y2, w_bf):
    n = x2.shape[0]
    tm = ROW_TILE
    return pl.pallas_call(
        _proj_residual_kernel,
        grid=(n // tm,),
        in_specs=[
            pl.BlockSpec((tm, D_MODEL), lambda i: (i, 0)),
            pl.BlockSpec((tm, D_MODEL), lambda i: (i, 0)),
            pl.BlockSpec((D_MODEL, D_MODEL), lambda i: (0, 0)),
        ],
        out_specs=pl.BlockSpec((tm, D_MODEL), lambda i: (i, 0)),
        out_shape=jax.ShapeDtypeStruct((n, D_MODEL), F32),
        compiler_params=pltpu.CompilerParams(dimension_semantics=("arbitrary",), vmem_limit_bytes=VMEM_LIMIT),
        name="attn_out_proj",
    )(x2, y2, w_bf)


def _conv_kernel(x_ref, g_ref, win_ref, bin_ref, dww_ref, dwb_ref, lng_ref, lnb_ref, wout_ref, bout_ref,
                 o_ref, u_ref):
    tm = x_ref.shape[0]
    x = x_ref[...]
    h = _rmsnorm_rows(x, g_ref[...]).astype(BF16)
    a = jnp.dot(h, win_ref[:, :D_MODEL], preferred_element_type=F32) + bin_ref[:, :D_MODEL]
    gate = jnp.dot(h, win_ref[:, D_MODEL:], preferred_element_type=F32) + bin_ref[:, D_MODEL:]
    u = a * jax.nn.sigmoid(gate)

    @pl.when(pl.program_id(1) == 0)
    def _():
        u_ref[0:CONV_HALO, :] = jnp.zeros((CONV_HALO, D_MODEL), F32)

    u_ref[CONV_HALO:CONV_HALO + tm, :] = u
    base = CONV_HALO - (CONV_WIDTH - 1)
    acc = jnp.zeros((tm, D_MODEL), F32) + dwb_ref[...]
    for w in range(CONV_WIDTH):
        acc = acc + dww_ref[w:w + 1, :] * u_ref[base + w:base + w + tm, :]
    u_ref[0:CONV_HALO, :] = u_ref[tm:tm + CONV_HALO, :]

    mu = jnp.mean(acc, axis=-1, keepdims=True)
    xc = acc - mu
    var = jnp.mean(xc * xc, axis=-1, keepdims=True)
    v = xc * lax.rsqrt(var + EPS) * lng_ref[...] + lnb_ref[...]
    v = (v * jax.nn.sigmoid(v)).astype(BF16)
    o_ref[...] = x + jnp.dot(v, wout_ref[...], preferred_element_type=F32) + bout_ref[...]


def _conv_block(x3, g, win_bf, b_in, dw_w, dw_b, ln_g, ln_b, wout_bf, b_out):
    batch, seq, _ = x3.shape
    tm = CONV_TILE
    row = lambda b, s: (b, s, 0)
    fixed = lambda b, s: (0, 0)
    vec = pl.BlockSpec((1, D_MODEL), fixed)
    return pl.pallas_call(
        _conv_kernel,
        grid=(batch, seq // tm),
        in_specs=[
            pl.BlockSpec((None, tm, D_MODEL), row),
            vec,
            pl.BlockSpec((D_MODEL, 2 * D_MODEL), fixed),
            pl.BlockSpec((1, 2 * D_MODEL), fixed),
            pl.BlockSpec((CONV_HALO, D_MODEL), fixed),
            vec, vec, vec,
            pl.BlockSpec((D_MODEL, D_MODEL), fixed),
            vec,
        ],
        out_specs=pl.BlockSpec((None, tm, D_MODEL), row),
        out_shape=jax.ShapeDtypeStruct((batch, seq, D_MODEL), F32),
        scratch_shapes=[pltpu.VMEM((tm + CONV_HALO, D_MODEL), F32)],
        compiler_params=pltpu.CompilerParams(
            dimension_semantics=("arbitrary", "arbitrary"), vmem_limit_bytes=VMEM_LIMIT),
        name="conformer_conv",
    )(x3, g, win_bf, b_in, dw_w, dw_b, ln_g, ln_b, wout_bf, b_out)


def _split_bf16(v):
    hi = v.astype(BF16)
    lo = (v - hi.astype(F32)).astype(BF16)
    return hi, lo


def _router_kernel(x_ref, g_ref, wr_ref, br_ref, meta_ref, wts_ref, cnt_ref, carry_ref, tri_ref):
    t = x_ref.shape[0]

    @pl.when(pl.program_id(0) == 0)
    def _():
        carry_ref[...] = jnp.zeros(carry_ref.shape, F32)
        before = lax.broadcasted_iota(I32, (t, t), 0) < lax.broadcasted_iota(I32, (t, t), 1)
        tri_ref[...] = jnp.where(before, 1.0, 0.0).astype(BF16)

    h = _rmsnorm_rows(x_ref[...], g_ref[...])
    h_hi, h_lo = _split_bf16(h)
    w_hi, w_lo = _split_bf16(wr_ref[...])
    logits = (lax.dot_general(w_hi, h_hi, _NT, preferred_element_type=F32)
              + lax.dot_general(w_hi, h_lo, _NT, preferred_element_type=F32)
              + lax.dot_general(w_lo, h_hi, _NT, preferred_element_type=F32)) + br_ref[...]

    gl = logits[N_EXPERTS:N_EXPERTS + N_GROUPS]
    gmax = jnp.max(gl, axis=0, keepdims=True)
    g_p = 1.0 / jnp.sum(jnp.exp(gl - gmax), axis=0, keepdims=True)
    row_g = lax.broadcasted_iota(I32, gl.shape, 0)
    g_idx = jnp.min(jnp.where(gl == gmax, row_g, N_GROUPS), axis=0, keepdims=True)

    e_sel = jnp.zeros((EXPERTS_PER_GROUP, t), F32)
    for grp in range(N_GROUPS):
        e_sel = e_sel + jnp.where(g_idx == grp, logits[grp * EXPERTS_PER_GROUP:(grp + 1) * EXPERTS_PER_GROUP], 0.0)
    row_e = lax.broadcasted_iota(I32, e_sel.shape, 0)
    m1 = jnp.max(e_sel, axis=0, keepdims=True)
    i1 = jnp.min(jnp.where(e_sel == m1, row_e, EXPERTS_PER_GROUP), axis=0, keepdims=True)
    rest = jnp.where(row_e == i1, -jnp.inf, e_sel)
    m2 = jnp.max(rest, axis=0, keepdims=True)
    i2 = jnp.min(jnp.where(rest == m2, row_e, EXPERTS_PER_GROUP), axis=0, keepdims=True)
    r = jnp.exp(m2 - m1)
    p1 = 1.0 / (1.0 + r)
    f1 = g_idx * EXPERTS_PER_GROUP + i1
    f2 = g_idx * EXPERTS_PER_GROUP + i2

    row_x = lax.broadcasted_iota(I32, (N_EXPERTS, t), 0)
    oh1 = row_x == f1
    oh2 = row_x == f2
    onehot = jnp.where(oh1 | oh2, 1.0, 0.0)
    prior = jnp.dot(onehot.astype(BF16), tri_ref[...], preferred_element_type=F32) + carry_ref[:, 0:1]
    rank1 = jnp.sum(jnp.where(oh1, prior, 0.0), axis=0, keepdims=True)
    rank2 = jnp.sum(jnp.where(oh2, prior, 0.0), axis=0, keepdims=True)
    total = carry_ref[...] + jnp.sum(onehot, axis=1, keepdims=True)
    carry_ref[...] = total
    cnt_ref[...] = total.astype(I32)

    zi = jnp.zeros((4, t), I32)
    meta_ref[...] = jnp.concatenate([f1, f2, rank1.astype(I32), rank2.astype(I32), zi], axis=0)
    wts_ref[...] = jnp.concatenate([g_p * p1, g_p * (r * p1), jnp.zeros((6, t), F32)], axis=0)


def _route(x2, g, wr, br):
    n = x2.shape[0]
    t = ROUTER_TILE
    return pl.pallas_call(
        _router_kernel,
        grid=(n // t,),
        in_specs=[
            pl.BlockSpec((t, D_MODEL), lambda i: (i, 0)),
            pl.BlockSpec((1, D_MODEL), lambda i: (0, 0)),
            pl.BlockSpec((ROUTER_ROWS, D_MODEL), lambda i: (0, 0)),
            pl.BlockSpec((ROUTER_ROWS, 1), lambda i: (0, 0)),
        ],
        out_specs=[
            pl.BlockSpec((8, t), lambda i: (0, i)),
            pl.BlockSpec((8, t), lambda i: (0, i)),
            pl.BlockSpec((N_EXPERTS, 128), lambda i: (0, 0)),
        ],
        out_shape=[
            jax.ShapeDtypeStruct((8, n), I32),
            jax.ShapeDtypeStruct((8, n), F32),
            jax.ShapeDtypeStruct((N_EXPERTS, 128), I32),
        ],
        scratch_shapes=[pltpu.VMEM((N_EXPERTS, 128), F32), pltpu.VMEM((t, t), BF16)],
        compiler_params=pltpu.CompilerParams(dimension_semantics=("arbitrary",), vmem_limit_bytes=VMEM_LIMIT),
        name="moe_router",
    )(x2, g, wr, br)


def _positions_kernel(cnt_ref, meta_ref, pos_ref, tile_expert_ref, info_ref, last_row_ref, off_ref,
                      *, max_tiles):
    def per_expert(e, tiles_before):
        n_tiles = (cnt_ref[e, 0] + (MOE_ROW_TILE - 1)) // MOE_ROW_TILE
        off_ref[e] = tiles_before * MOE_ROW_TILE
        last_row_ref[e] = jnp.where(n_tiles > 0, (tiles_before + n_tiles - 1) * MOE_ROW_TILE, -1)

        def fill(k, c):
            tile_expert_ref[tiles_before + k] = e
            return c

        lax.fori_loop(0, n_tiles, fill, 0)
        return tiles_before + n_tiles

    used = lax.fori_loop(0, N_EXPERTS, per_expert, jnp.int32(0))
    info_ref[0] = used
    last_expert = tile_expert_ref[used - 1]

    def fill_tail(k, c):
        tile_expert_ref[k] = last_expert
        return c

    lax.fori_loop(used, max_tiles, fill_tail, 0)

    ids = meta_ref[0:2, :]
    pos = meta_ref[2:4, :]
    for e in range(N_EXPERTS):
        pos = pos + jnp.where(ids == e, off_ref[e], 0)
    pos_ref[...] = pos


def _positions(counts, meta, max_tiles):
    n = meta.shape[1]
    smem = pl.BlockSpec(memory_space=pltpu.SMEM)
    return pl.pallas_call(
        functools.partial(_positions_kernel, max_tiles=max_tiles),
        in_specs=[smem, pl.BlockSpec((8, n), lambda: (0, 0))],
        out_specs=[pl.BlockSpec((2, n), lambda: (0, 0)), smem, smem, smem],
        out_shape=[
            jax.ShapeDtypeStruct((2, n), I32),
            jax.ShapeDtypeStruct((max_tiles,), I32),
            jax.ShapeDtypeStruct((1,), I32),
            jax.ShapeDtypeStruct((N_EXPERTS,), I32),
        ],
        scratch_shapes=[pltpu.SMEM((N_EXPERTS,), I32)],
        name="moe_positions",
    )(counts, meta)


def _dispatch_kernel(pos_ref, last_row_ref, info_ref, x_ref, g_ref, xs_ref, h_ref, zero_ref, sem, zsem,
                     *, n_tokens):
    i = pl.program_id(0)
    td = x_ref.shape[0]

    def zero_copy(row):
        start = pl.multiple_of(row, MOE_ROW_TILE)
        return pltpu.make_async_copy(zero_ref, xs_ref.at[pl.ds(start, MOE_ROW_TILE)], zsem)

    @pl.when(i == 0)
    def _():
        zero_ref[...] = jnp.zeros(zero_ref.shape, F32)
        n_all_tiles = xs_ref.shape[0] // MOE_ROW_TILE

        def tail(t, c, wait):
            cp = zero_copy(t * MOE_ROW_TILE)
            cp.wait() if wait else cp.start()
            return c

        for wait in (False, True):
            for e in range(N_EXPERTS):
                @pl.when(last_row_ref[e] >= 0)
                def _():
                    cp = zero_copy(last_row_ref[e])
                    cp.wait() if wait else cp.start()
            lax.fori_loop(info_ref[0], n_all_tiles, functools.partial(tail, wait=wait), 0)

    h_ref[...] = _rmsnorm_rows(x_ref[...], g_ref[...])

    def row_copy(r, p):
        return pltpu.make_async_copy(h_ref.at[pl.ds(r, 1)], xs_ref.at[pl.ds(p, 1)], sem)

    def issue(r, c):
        row_copy(r, pos_ref[i * td + r]).start()
        row_copy(r, pos_ref[n_tokens + i * td + r]).start()
        return c

    lax.fori_loop(0, td, issue, 0, unroll=8)

    def drain(r, c):
        row_copy(0, 0).wait()
        row_copy(0, 0).wait()
        return c

    lax.fori_loop(0, td, drain, 0, unroll=8)


def _dispatch(pos_flat, last_row, info, x2, g, n_rows):
    n = x2.shape[0]
    td = DISPATCH_TILE
    return pl.pallas_call(
        functools.partial(_dispatch_kernel, n_tokens=n),
        grid_spec=pltpu.PrefetchScalarGridSpec(
            num_scalar_prefetch=3,
            grid=(n // td,),
            in_specs=[
                pl.BlockSpec((td, D_MODEL), lambda i, pos, last, info: (i, 0)),
                pl.BlockSpec((1, D_MODEL), lambda i, pos, last, info: (0, 0)),
            ],
            out_specs=pl.BlockSpec(memory_space=pl.ANY),
            scratch_shapes=[
                pltpu.VMEM((td, D_MODEL), F32),
                pltpu.VMEM((MOE_ROW_TILE, D_MODEL), F32),
                pltpu.SemaphoreType.DMA,
                pltpu.SemaphoreType.DMA,
            ],
        ),
        out_shape=jax.ShapeDtypeStruct((n_rows, D_MODEL), F32),
        compiler_params=pltpu.CompilerParams(
            dimension_semantics=("arbitrary",), vmem_limit_bytes=VMEM_LIMIT, has_side_effects=True),
        name="moe_dispatch",
    )(pos_flat, last_row, info, x2, g)


def _expert_kernel(tile_expert_ref, info_ref, xs_ref, wg_ref, wu_ref, wd_ref, ys_ref, wg_bf, wu_bf, wd_bf):
    j = pl.program_id(0)
    live = j < info_ref[0]
    e = tile_expert_ref[j]
    prev = tile_expert_ref[jnp.maximum(j - 1, 0)]

    @pl.when(live & ((j == 0) | (e != prev)))
    def _():
        wg_bf[...] = wg_ref[...].astype(BF16)
        wu_bf[...] = wu_ref[...].astype(BF16)
        wd_bf[...] = wd_ref[...].astype(BF16)

    @pl.when(jnp.logical_not(live))
    def _():
        ys_ref[...] = jnp.zeros(ys_ref.shape, F32)

    @pl.when(live)
    def _():
        x = xs_ref[...].astype(BF16)
        gate = jnp.dot(x, wg_bf[...], preferred_element_type=F32)
        up = jnp.dot(x, wu_bf[...], preferred_element_type=F32)
        hid = (gate * jax.nn.sigmoid(gate) * up).astype(BF16)
        ys_ref[...] = jnp.dot(hid, wd_bf[...], preferred_element_type=F32)


def _experts(tile_expert, info, xs, w_gate, w_up, w_down):
    n_rows = xs.shape[0]
    tm = MOE_ROW_TILE
    rows = lambda j, te, info: (jnp.minimum(j, info[0] - 1), 0)
    wsel = lambda j, te, info: (te[j], 0, 0)
    return pl.pallas_call(
        _expert_kernel,
        grid_spec=pltpu.PrefetchScalarGridSpec(
            num_scalar_prefetch=2,
            grid=(n_rows // tm,),
            in_specs=[
                pl.BlockSpec((tm, D_MODEL), rows),
                pl.BlockSpec((None, D_MODEL, D_EXPERT), wsel),
                pl.BlockSpec((None, D_MODEL, D_EXPERT), wsel),
                pl.BlockSpec((None, D_EXPERT, D_MODEL), wsel),
            ],
            out_specs=pl.BlockSpec((tm, D_MODEL), lambda j, te, info: (j, 0)),
            scratch_shapes=[
                pltpu.VMEM((D_MODEL, D_EXPERT), BF16),
                pltpu.VMEM((D_MODEL, D_EXPERT), BF16),
                pltpu.VMEM((D_EXPERT, D_MODEL), BF16),
            ],
        ),
        out_shape=jax.ShapeDtypeStruct((n_rows, D_MODEL), F32),
        compiler_params=pltpu.CompilerParams(dimension_semantics=("arbitrary",), vmem_limit_bytes=VMEM_LIMIT),
        name="moe_experts",
    )(tile_expert, info, xs, w_gate, w_up, w_down)


def _combine_kernel(pos_ref, x_ref, wts_ref, ys_ref, o_ref, rows_ref, sem, *, n_tokens):
    i = pl.program_id(0)
    tc = x_ref.shape[0]

    def row_copy(slot, r, p):
        return pltpu.make_async_copy(ys_ref.at[pl.ds(p, 1)], rows_ref.at[slot, pl.ds(r, 1)], sem)

    def issue(r, c):
        row_copy(0, r, pos_ref[i * tc + r]).start()
        row_copy(1, r, pos_ref[n_tokens + i * tc + r]).start()
        return c

    lax.fori_loop(0, tc, issue, 0, unroll=8)

    eye = jnp.where(lax.broadcasted_iota(I32, (tc, tc), 0) == lax.broadcasted_iota(I32, (tc, tc), 1),
                    1.0, 0.0).astype(BF16)
    w_hi, w_lo = _split_bf16(wts_ref[...])
    w_col = (lax.dot_general(eye, w_hi, _NT, preferred_element_type=F32)
             + lax.dot_general(eye, w_lo, _NT, preferred_element_type=F32))

    def drain(r, c):
        row_copy(0, 0, 0).wait()
        row_copy(1, 0, 0).wait()
        return c

    lax.fori_loop(0, tc, drain, 0, unroll=8)
    o_ref[...] = x_ref[...] + w_col[:, 0:1] * rows_ref[0] + w_col[:, 1:2] * rows_ref[1]


def _combine(pos_flat, x2, wts, ys):
    n = x2.shape[0]
    tc = COMBINE_TILE
    return pl.pallas_call(
        functools.partial(_combine_kernel, n_tokens=n),
        grid_spec=pltpu.PrefetchScalarGridSpec(
            num_scalar_prefetch=1,
            grid=(n // tc,),
            in_specs=[
                pl.BlockSpec((tc, D_MODEL), lambda i, pos: (i, 0)),
                pl.BlockSpec((8, tc), lambda i, pos: (0, i)),
                pl.BlockSpec(memory_space=pl.ANY),
            ],
            out_specs=pl.BlockSpec((tc, D_MODEL), lambda i, pos: (i, 0)),
            scratch_shapes=[pltpu.VMEM((2, tc, D_MODEL), F32), pltpu.SemaphoreType.DMA],
        ),
        out_shape=jax.ShapeDtypeStruct((n, D_MODEL), F32),
        compiler_params=pltpu.CompilerParams(dimension_semantics=("arbitrary",), vmem_limit_bytes=VMEM_LIMIT),
        name="moe_combine",
    )(pos_flat, x2, wts, ys)


def _hier_moe(x2, g, w_rg, b_rg, w_re, b_re, w_gate, w_up, w_down):
    n = x2.shape[0]
    max_tiles = (2 * n) // MOE_ROW_TILE + N_EXPERTS
    wr = jnp.concatenate([
        jnp.transpose(w_re, (0, 2, 1)).reshape(N_EXPERTS, D_MODEL),
        w_rg.T,
        jnp.zeros((ROUTER_ROWS - N_EXPERTS - N_GROUPS, D_MODEL), F32)], axis=0)
    br = jnp.concatenate([
        b_re.reshape(N_EXPERTS), b_rg, jnp.zeros((ROUTER_ROWS - N_EXPERTS - N_GROUPS,), F32)])[:, None]

    meta, wts, counts = _route(x2, g, wr, br)
    pos, tile_expert, info, last_row = _positions(counts, meta, max_tiles)
    pos_flat = pos.reshape(2 * n)
    xs = _dispatch(pos_flat, last_row, info, x2, g, max_tiles * MOE_ROW_TILE)
    ys = _experts(tile_expert, info, xs, w_gate, w_up, w_down)
    return _combine(pos_flat, x2, wts, ys)


def _rope_tables(seq):
    pos = jnp.arange(seq, dtype=F32)
    inv_freq = ROPE_THETA ** (-jnp.arange(0, ROT_DIM, 2, dtype=F32) / ROT_DIM)
    ang = pos[:, None] * inv_freq[None, :]
    cos, sin = jnp.cos(ang), jnp.sin(ang)
    half = ROT_DIM // 2
    ones = jnp.ones((seq, HEAD_DIM - ROT_DIM), F32)
    zeros_h = jnp.zeros((seq, half), F32)
    zeros_r = jnp.zeros((seq, HEAD_DIM - ROT_DIM), F32)
    cos_c = jnp.concatenate([cos, cos, ones], axis=1)
    sinp_c = jnp.concatenate([zeros_h, sin, zeros_r], axis=1)
    sinm_c = jnp.concatenate([-sin, zeros_h, zeros_r], axis=1)
    dup = lambda t: jnp.concatenate([t, t], axis=1)
    return dup(cos_c), dup(sinp_c), dup(sinm_c)


def kernel(x, attn_norm, diff_w_in, diff_q_gain, diff_k_gain, diff_lambda_q1, diff_lambda_k1,
           diff_lambda_q2, diff_lambda_k2, diff_subln, diff_w_out,
           conv_norm, conv_w_in, conv_b_in, conv_dw_w, conv_dw_b, conv_ln_g, conv_ln_b,
           conv_w_out, conv_b_out,
           ffn_norm, router_group_w, router_group_b, router_expert_w, router_expert_b,
           moe_w_gate, moe_w_up, moe_w_down):
    batch, seq, d = x.shape
    assert d == D_MODEL and seq % ATTN_TILE == 0 and seq % CONV_TILE == 0
    n = batch * seq
    depth = ffn_norm.shape[0]
    cos_t, sinp_t, sinm_t = _rope_tables(seq)
    row = lambda v: v.reshape(1, -1)
    x2 = x.reshape(n, d)
    for i in range(depth):
        j = i // 2
        if i % 2 == 0:
            lambda_init = 0.8 - 0.6 * math.exp(-0.3 * i)
            dup = lambda v: jnp.concatenate([v, v]).reshape(1, HEAD_W)
            q2, k2, vt4 = _qkv_project(x2, row(attn_norm[j]), diff_w_in[j].astype(BF16),
                                       dup(diff_q_gain[j]), dup(diff_k_gain[j]), cos_t, sinp_t, sinm_t,
                                       batch, seq)
            lam_params = jnp.stack([diff_lambda_q1[j], diff_lambda_k1[j], diff_lambda_q2[j], diff_lambda_k2[j]])
            o3 = _diff_attention(q2.reshape(batch, seq, d), k2.reshape(batch, seq, d), vt4,
                                 lam_params, row(diff_subln[j]), lambda_init)
            x2 = _proj_residual(x2, o3.reshape(n, d), diff_w_out[j].astype(BF16))
        else:
            dw_w = jnp.concatenate([conv_dw_w[j], jnp.zeros((CONV_HALO - CONV_WIDTH, d), F32)], axis=0)
            x2 = _conv_block(x2.reshape(batch, seq, d), row(conv_norm[j]), conv_w_in[j].astype(BF16),
                             row(conv_b_in[j]), dw_w, row(conv_dw_b[j]), row(conv_ln_g[j]), row(conv_ln_b[j]),
                             conv_w_out[j].astype(BF16), row(conv_b_out[j])).reshape(n, d)
        x2 = _hier_moe(x2, row(ffn_norm[i]), router_group_w[i], router_group_b[i],
                       router_expert_w[i], router_expert_b[i], moe_w_gate[i], moe_w_up[i], moe_w_down[i])
    return x2.reshape(batch, seq, d)
```

```python
import functools
import math

import jax
import jax.numpy as jnp
from jax import lax
from jax.experimental import pallas as pl
from jax.experimental.pallas import tpu as pltpu

F32 = jnp.float32
BF16 = jnp.bfloat16
I32 = jnp.int32

D_MODEL = 1024
HEAD_DIM = 64
HEAD_W = 2 * HEAD_DIM
N_HEADS = D_MODEL // HEAD_W
ROT_DIM = HEAD_DIM // 4
ROPE_THETA = 500000.0
CHUNK = 64
CONV_WIDTH = 31
CONV_HALO = 32
N_GROUPS = 4
EXPERTS_PER_GROUP = 8
N_EXPERTS = N_GROUPS * EXPERTS_PER_GROUP
D_EXPERT = D_MODEL // 2
EPS = 1e-6
ROUTER_ROWS = 64
LOG2_E = math.log2(math.e)
Q_SCALE = HEAD_DIM ** -0.5 * LOG2_E
SCORE_BOUND_UNIT = 1.02 * HEAD_DIM * Q_SCALE
UNSHIFTED_SOFTMAX_LIMIT = 50.0

ATTN_TILE = 512
ATTN_GROUP = 2
ROW_TILE = 512
CONV_TILE = 256
ROUTER_TILE = 512
MOE_ROW_TILE = 256
DISPATCH_TILE = 256
COMBINE_TILE = 256
VMEM_LIMIT = 48 * 1024 * 1024

_NT = (((1,), (1,)), ((), ()))


def _rmsnorm_rows(x, g):
    return x * lax.rsqrt(jnp.mean(x * x, axis=-1, keepdims=True) + EPS) * g


def _qkv_kernel(x_ref, g_ref, w_ref, qg_ref, kg_ref, cos_ref, sinp_ref, sinm_ref,
                q_ref, k_ref, vt_ref, *, col_tile):
    tm = x_ref.shape[0]
    h = _rmsnorm_rows(x_ref[...], g_ref[...]).astype(BF16)
    lane = lax.broadcasted_iota(I32, (tm, HEAD_W), 1)
    lo = lane < HEAD_DIM
    cosv, sinp, sinm = cos_ref[...], sinp_ref[...], sinm_ref[...]

    def norm_rope(y, gain, scale):
        y2 = y * y
        s_lo = jnp.sum(jnp.where(lo, y2, 0.0), axis=-1, keepdims=True)
        s_hi = jnp.sum(y2, axis=-1, keepdims=True) - s_lo
        r = jnp.where(lo, lax.rsqrt(s_lo * (1.0 / HEAD_DIM) + EPS), lax.rsqrt(s_hi * (1.0 / HEAD_DIM) + EPS))
        yn = y * r * gain
        half = ROT_DIM // 2
        out = yn * cosv + pltpu.roll(yn, half, 1) * sinp + pltpu.roll(yn, HEAD_W - half, 1) * sinm
        return out * scale

    for c in range(3 * D_MODEL // col_tile):
        y = jnp.dot(h, w_ref[:, c * col_tile:(c + 1) * col_tile], preferred_element_type=F32)
        for s in range(col_tile // HEAD_W):
            col = c * col_tile + s * HEAD_W
            ys = y[:, s * HEAD_W:(s + 1) * HEAD_W]
            if col < D_MODEL:
                q_ref[:, col:col + HEAD_W] = norm_rope(ys, qg_ref[...], Q_SCALE).astype(BF16)
            elif col < 2 * D_MODEL:
                k_ref[:, col - D_MODEL:col - D_MODEL + HEAD_W] = norm_rope(ys, kg_ref[...], 1.0).astype(BF16)
            else:
                vt_ref[col - 2 * D_MODEL:col - 2 * D_MODEL + HEAD_W, :] = ys.T.astype(BF16)


def _qkv_project(x2, g, w_bf, q_gain, k_gain, cos_t, sinp_t, sinm_t, batch, seq):
    n = x2.shape[0]
    tm = ATTN_TILE
    tiles_per_seq = seq // tm
    row = lambda i: (i, 0)
    fixed = lambda i: (0, 0)
    tab = lambda i: (i % tiles_per_seq, 0)
    return pl.pallas_call(
        functools.partial(_qkv_kernel, col_tile=512),
        grid=(n // tm,),
        in_specs=[
            pl.BlockSpec((tm, D_MODEL), row),
            pl.BlockSpec((1, D_MODEL), fixed),
            pl.BlockSpec((D_MODEL, 3 * D_MODEL), fixed),
            pl.BlockSpec((1, HEAD_W), fixed),
            pl.BlockSpec((1, HEAD_W), fixed),
            pl.BlockSpec((tm, HEAD_W), tab),
            pl.BlockSpec((tm, HEAD_W), tab),
            pl.BlockSpec((tm, HEAD_W), tab),
        ],
        out_specs=[
            pl.BlockSpec((tm, D_MODEL), row),
            pl.BlockSpec((tm, D_MODEL), row),
            pl.BlockSpec((None, None, D_MODEL, tm), lambda i: (i // tiles_per_seq, i % tiles_per_seq, 0, 0)),
        ],
        out_shape=[
            jax.ShapeDtypeStruct((n, D_MODEL), BF16),
            jax.ShapeDtypeStruct((n, D_MODEL), BF16),
            jax.ShapeDtypeStruct((batch, tiles_per_seq, D_MODEL, tm), BF16),
        ],
        compiler_params=pltpu.CompilerParams(dimension_semantics=("arbitrary",), vmem_limit_bytes=VMEM_LIMIT),
        name="qkv_project",
    )(x2, g, w_bf, q_gain, k_gain, cos_t, sinp_t, sinm_t)


def _attn_kernel(bound_ref, lam_ref, q_ref, k_ref, vt_ref, g_ref, o_ref, m_ref, l_ref, acc_ref, *, lambda_init):
    t = q_ref.shape[0]
    i = pl.program_id(2)
    q = q_ref[...]
    lane = lax.broadcasted_iota(I32, (t, HEAD_W), 1)
    zero = jnp.zeros_like(q)
    q_comp = (jnp.where(lane < HEAD_DIM, q, zero), jnp.where(lane >= HEAD_DIM, q, zero))

    l_ref[...] = jnp.zeros(l_ref.shape, F32)
    acc_ref[...] = jnp.zeros(acc_ref.shape, F32)

    def tiles(j):
        return k_ref[pl.ds(pl.multiple_of(j * t, t), t), :], vt_ref[j]

    def chunk_mask():
        key_chunk = lax.broadcasted_iota(I32, (t, t), 0) // CHUNK
        qry_chunk = lax.broadcasted_iota(I32, (t, t), 1) // CHUNK
        return key_chunk <= qry_chunk

    def unshifted_step(j, masked):
        kt, vt = tiles(j)
        for c in range(2):
            s = lax.dot_general(kt, q_comp[c], _NT, preferred_element_type=F32)
            p = jnp.exp2(s)
            if masked:
                p = jnp.where(chunk_mask(), p, 0.0)
            l_ref[c] += jnp.sum(p, axis=0, keepdims=True)
            acc_ref[c] += jnp.dot(vt, p.astype(BF16), preferred_element_type=F32)

    def online_step(j, masked):
        kt, vt = tiles(j)
        for c in range(2):
            s = lax.dot_general(kt, q_comp[c], _NT, preferred_element_type=F32)
            if masked:
                s = jnp.where(chunk_mask(), s, -jnp.inf)
            m_old = m_ref[c]
            m_new = jnp.maximum(m_old, jnp.max(s, axis=0, keepdims=True))
            alpha = jnp.exp2(m_old - m_new)
            p = jnp.exp2(s - m_new)
            l_ref[c] = alpha * l_ref[c] + jnp.sum(p, axis=0, keepdims=True)
            acc_ref[c] = alpha * acc_ref[c] + jnp.dot(vt, p.astype(BF16), preferred_element_type=F32)
            m_ref[c] = m_new

    def sweep(step, group):
        def grouped(jj, carry):
            for u in range(group):
                step(jj * group + u, False)
            return carry

        def single(j, carry):
            step(j, False)
            return carry

        n_grouped = i // group
        lax.fori_loop(0, n_grouped, grouped, 0)
        lax.fori_loop(n_grouped * group, i, single, 0)
        step(i, True)

    unshifted_ok = bound_ref[0] <= UNSHIFTED_SOFTMAX_LIMIT

    @pl.when(unshifted_ok)
    def _():
        sweep(unshifted_step, ATTN_GROUP)

    @pl.when(jnp.logical_not(unshifted_ok))
    def _():
        m_ref[...] = jnp.full(m_ref.shape, -jnp.inf, F32)
        sweep(online_step, 1)

    lam_p = lam_ref[...]
    lam = (jnp.exp(jnp.sum(lam_p[0:1] * lam_p[1:2], axis=-1, keepdims=True))
           - jnp.exp(jnp.sum(lam_p[2:3] * lam_p[3:4], axis=-1, keepdims=True)) + lambda_init)
    ot = acc_ref[0] * (1.0 / l_ref[0]) - lam * (acc_ref[1] * (1.0 / l_ref[1]))
    ot = ot * lax.rsqrt(jnp.mean(ot * ot, axis=0, keepdims=True) + EPS)
    o_ref[...] = (ot.T * g_ref[...] * (1.0 - lambda_init)).astype(o_ref.dtype)


def _diff_attention(score_bound, q3, k3, vt4, lam_params, subln_g, lambda_init):
    batch, seq, _ = q3.shape
    t = ATTN_TILE
    nq = seq // t
    return pl.pallas_call(
        functools.partial(_attn_kernel, lambda_init=lambda_init),
        grid_spec=pltpu.PrefetchScalarGridSpec(
            num_scalar_prefetch=1,
            grid=(batch, N_HEADS, nq),
            in_specs=[
                pl.BlockSpec((4, HEAD_DIM), lambda b, h, i, bound: (0, 0)),
                pl.BlockSpec((None, t, HEAD_W), lambda b, h, i, bound: (b, i, h)),
                pl.BlockSpec((None, seq, HEAD_W), lambda b, h, i, bound: (b, 0, h)),
                pl.BlockSpec((None, nq, HEAD_W, t), lambda b, h, i, bound: (b, 0, h, 0)),
                pl.BlockSpec((1, HEAD_W), lambda b, h, i, bound: (0, 0)),
            ],
            out_specs=pl.BlockSpec((None, t, HEAD_W), lambda b, h, i, bound: (b, i, h)),
            scratch_shapes=[
                pltpu.VMEM((2, 1, t), F32),
                pltpu.VMEM((2, 1, t), F32),
                pltpu.VMEM((2, HEAD_W, t), F32),
            ],
        ),
        out_shape=jax.ShapeDtypeStruct((batch, seq, D_MODEL), BF16),
        compiler_params=pltpu.CompilerParams(
            dimension_semantics=("arbitrary", "arbitrary", "arbitrary"), vmem_limit_bytes=VMEM_LIMIT),
        name="diff_attention",
    )(score_bound, lam_params, q3, k3, vt4, subln_g)


def _proj_residual_kernel(x_ref, y_ref, w_ref, o_ref):
    o_ref[...] = x_ref[...] + jnp.dot(y_ref[...], w_ref[...], preferred_element_type=F32)


def _proj_residual(x2, y2, w_bf):
    n = x2.shape[0]
    tm = ROW_TILE
    return pl.pallas_call(
        _proj_residual_kernel,
        grid=(n // tm,),
        in_specs=[
            pl.BlockSpec((tm, D_MODEL), lambda i: (i, 0)),
            pl.BlockSpec((tm, D_MODEL), lambda i: (i, 0)),
            pl.BlockSpec((D_MODEL, D_MODEL), lambda i: (0, 0)),
        ],
        out_specs=pl.BlockSpec((tm, D_MODEL), lambda i: (i, 0)),
        out_shape=jax.ShapeDtypeStruct((n, D_MODEL), F32),
        compiler_params=pltpu.CompilerParams(dimension_semantics=("arbitrary",), vmem_limit_bytes=VMEM_LIMIT),
        name="attn_out_proj",
    )(x2, y2, w_bf)


def _conv_kernel(x_ref, g_ref, win_ref, bin_ref, dww_ref, dwb_ref, lng_ref, lnb_ref, wout_ref, bout_ref,
                 o_ref, u_ref):
    tm = x_ref.shape[0]
    x = x_ref[...]
    h = _rmsnorm_rows(x, g_ref[...]).astype(BF16)
    a = jnp.dot(h, win_ref[:, :D_MODEL], preferred_element_type=F32) + bin_ref[:, :D_MODEL]
    gate = jnp.dot(h, win_ref[:, D_MODEL:], preferred_element_type=F32) + bin_ref[:, D_MODEL:]
    u = a * jax.nn.sigmoid(gate)

    @pl.when(pl.program_id(1) == 0)
    def _():
        u_ref[0:CONV_HALO, :] = jnp.zeros((CONV_HALO, D_MODEL), F32)

    u_ref[CONV_HALO:CONV_HALO + tm, :] = u
    base = CONV_HALO - (CONV_WIDTH - 1)
    acc = jnp.zeros((tm, D_MODEL), F32) + dwb_ref[...]
    for w in range(CONV_WIDTH):
        acc = acc + dww_ref[w:w + 1, :] * u_ref[base + w:base + w + tm, :]
    u_ref[0:CONV_HALO, :] = u_ref[tm:tm + CONV_HALO, :]

    mu = jnp.mean(acc, axis=-1, keepdims=True)
    xc = acc - mu
    var = jnp.mean(xc * xc, axis=-1, keepdims=True)
    v = xc * lax.rsqrt(var + EPS) * lng_ref[...] + lnb_ref[...]
    v = (v * jax.nn.sigmoid(v)).astype(BF16)
    o_ref[...] = x + jnp.dot(v, wout_ref[...], preferred_element_type=F32) + bout_ref[...]


def _conv_block(x3, g, win_bf, b_in, dw_w, dw_b, ln_g, ln_b, wout_bf, b_out):
    batch, seq, _ = x3.shape
    tm = CONV_TILE
    row = lambda b, s: (b, s, 0)
    fixed = lambda b, s: (0, 0)
    vec = pl.BlockSpec((1, D_MODEL), fixed)
    return pl.pallas_call(
        _conv_kernel,
        grid=(batch, seq // tm),
        in_specs=[
            pl.BlockSpec((None, tm, D_MODEL), row),
            vec,
            pl.BlockSpec((D_MODEL, 2 * D_MODEL), fixed),
            pl.BlockSpec((1, 2 * D_MODEL), fixed),
            pl.BlockSpec((CONV_HALO, D_MODEL), fixed),
            vec, vec, vec,
            pl.BlockSpec((D_MODEL, D_MODEL), fixed),
            vec,
        ],
        out_specs=pl.BlockSpec((None, tm, D_MODEL), row),
        out_shape=jax.ShapeDtypeStruct((batch, seq, D_MODEL), F32),
        scratch_shapes=[pltpu.VMEM((tm + CONV_HALO, D_MODEL), F32)],
        compiler_params=pltpu.CompilerParams(
            dimension_semantics=("arbitrary", "arbitrary"), vmem_limit_bytes=VMEM_LIMIT),
        name="conformer_conv",
    )(x3, g, win_bf, b_in, dw_w, dw_b, ln_g, ln_b, wout_bf, b_out)


def _split_bf16(v):
    hi = v.astype(BF16)
    lo = (v - hi.astype(F32)).astype(BF16)
    return hi, lo


def _router_kernel(x_ref, g_ref, wr_ref, br_ref, meta_ref, wts_ref, cnt_ref, carry_ref, tri_ref):
    t = x_ref.shape[0]

    @pl.when(pl.program_id(0) == 0)
    def _():
        carry_ref[...] = jnp.zeros(carry_ref.shape, F32)
        before = lax.broadcasted_iota(I32, (t, t), 0) < lax.broadcasted_iota(I32, (t, t), 1)
        tri_ref[...] = jnp.where(before, 1.0, 0.0).astype(BF16)

    h = _rmsnorm_rows(x_ref[...], g_ref[...])
    h_hi, h_lo = _split_bf16(h)
    w_hi, w_lo = _split_bf16(wr_ref[...])
    logits = (lax.dot_general(w_hi, h_hi, _NT, preferred_element_type=F32)
              + lax.dot_general(w_hi, h_lo, _NT, preferred_element_type=F32)
              + lax.dot_general(w_lo, h_hi, _NT, preferred_element_type=F32)) + br_ref[...]

    gl = logits[N_EXPERTS:N_EXPERTS + N_GROUPS]
    gmax = jnp.max(gl, axis=0, keepdims=True)
    g_p = 1.0 / jnp.sum(jnp.exp(gl - gmax), axis=0, keepdims=True)
    row_g = lax.broadcasted_iota(I32, gl.shape, 0)
    g_idx = jnp.min(jnp.where(gl == gmax, row_g, N_GROUPS), axis=0, keepdims=True)

    e_sel = jnp.zeros((EXPERTS_PER_GROUP, t), F32)
    for grp in range(N_GROUPS):
        e_sel = e_sel + jnp.where(g_idx == grp, logits[grp * EXPERTS_PER_GROUP:(grp + 1) * EXPERTS_PER_GROUP], 0.0)
    row_e = lax.broadcasted_iota(I32, e_sel.shape, 0)
    m1 = jnp.max(e_sel, axis=0, keepdims=True)
    i1 = jnp.min(jnp.where(e_sel == m1, row_e, EXPERTS_PER_GROUP), axis=0, keepdims=True)
    rest = jnp.where(row_e == i1, -jnp.inf, e_sel)
    m2 = jnp.max(rest, axis=0, keepdims=True)
    i2 = jnp.min(jnp.where(rest == m2, row_e, EXPERTS_PER_GROUP), axis=0, keepdims=True)
    r = jnp.exp(m2 - m1)
    p1 = 1.0 / (1.0 + r)
    f1 = g_idx * EXPERTS_PER_GROUP + i1
    f2 = g_idx * EXPERTS_PER_GROUP + i2

    row_x = lax.broadcasted_iota(I32, (N_EXPERTS, t), 0)
    oh1 = row_x == f1
    oh2 = row_x == f2
    onehot = jnp.where(oh1 | oh2, 1.0, 0.0)
    prior = jnp.dot(onehot.astype(BF16), tri_ref[...], preferred_element_type=F32) + carry_ref[:, 0:1]
    rank1 = jnp.sum(jnp.where(oh1, prior, 0.0), axis=0, keepdims=True)
    rank2 = jnp.sum(jnp.where(oh2, prior, 0.0), axis=0, keepdims=True)
    total = carry_ref[...] + jnp.sum(onehot, axis=1, keepdims=True)
    carry_ref[...] = total
    cnt_ref[...] = total.astype(I32)

    zi = jnp.zeros((4, t), I32)
    meta_ref[...] = jnp.concatenate([f1, f2, rank1.astype(I32), rank2.astype(I32), zi], axis=0)
    wts_ref[...] = jnp.concatenate([g_p * p1, g_p * (r * p1), jnp.zeros((6, t), F32)], axis=0)


def _route(x2, g, wr, br):
    n = x2.shape[0]
    t = ROUTER_TILE
    return pl.pallas_call(
        _router_kernel,
        grid=(n // t,),
        in_specs=[
            pl.BlockSpec((t, D_MODEL), lambda i: (i, 0)),
            pl.BlockSpec((1, D_MODEL), lambda i: (0, 0)),
            pl.BlockSpec((ROUTER_ROWS, D_MODEL), lambda i: (0, 0)),
            pl.BlockSpec((ROUTER_ROWS, 1), lambda i: (0, 0)),
        ],
        out_specs=[
            pl.BlockSpec((8, t), lambda i: (0, i)),
            pl.BlockSpec((8, t), lambda i: (0, i)),
            pl.BlockSpec((N_EXPERTS, 128), lambda i: (0, 0)),
        ],
        out_shape=[
            jax.ShapeDtypeStruct((8, n), I32),
            jax.ShapeDtypeStruct((8, n), F32),
            jax.ShapeDtypeStruct((N_EXPERTS, 128), I32),
        ],
        scratch_shapes=[pltpu.VMEM((N_EXPERTS, 128), F32), pltpu.VMEM((t, t), BF16)],
        compiler_params=pltpu.CompilerParams(dimension_semantics=("arbitrary",), vmem_limit_bytes=VMEM_LIMIT),
        name="moe_router",
    )(x2, g, wr, br)


def _positions_kernel(cnt_ref, meta_ref, pos_ref, tile_expert_ref, info_ref, last_row_ref, off_ref,
                      *, max_tiles):
    def per_expert(e, tiles_before):
        n_tiles = (cnt_ref[e, 0] + (MOE_ROW_TILE - 1)) // MOE_ROW_TILE
        off_ref[e] = tiles_before * MOE_ROW_TILE
        last_row_ref[e] = jnp.where(n_tiles > 0, (tiles_before + n_tiles - 1) * MOE_ROW_TILE, -1)

        def fill(k, c):
            tile_expert_ref[tiles_before + k] = e
            return c

        lax.fori_loop(0, n_tiles, fill, 0)
        return tiles_before + n_tiles

    used = lax.fori_loop(0, N_EXPERTS, per_expert, jnp.int32(0))
    info_ref[0] = used
    last_expert = tile_expert_ref[used - 1]

    def fill_tail(k, c):
        tile_expert_ref[k] = last_expert
        return c

    lax.fori_loop(used, max_tiles, fill_tail, 0)

    ids = meta_ref[0:2, :]
    pos = meta_ref[2:4, :]
    for e in range(N_EXPERTS):
        pos = pos + jnp.where(ids == e, off_ref[e], 0)
    pos_ref[...] = pos


def _positions(counts, meta, max_tiles):
    n = meta.shape[1]
    smem = pl.BlockSpec(memory_space=pltpu.SMEM)
    return pl.pallas_call(
        functools.partial(_positions_kernel, max_tiles=max_tiles),
        in_specs=[smem, pl.BlockSpec((8, n), lambda: (0, 0))],
        out_specs=[pl.BlockSpec((2, n), lambda: (0, 0)), smem, smem, smem],
        out_shape=[
            jax.ShapeDtypeStruct((2, n), I32),
            jax.ShapeDtypeStruct((max_tiles,), I32),
            jax.ShapeDtypeStruct((1,), I32),
            jax.ShapeDtypeStruct((N_EXPERTS,), I32),
        ],
        scratch_shapes=[pltpu.SMEM((N_EXPERTS,), I32)],
        name="moe_positions",
    )(counts, meta)


def _dispatch_kernel(pos_ref, last_row_ref, info_ref, x_ref, g_ref, xs_ref, h_ref, zero_ref, sem, zsem,
                     *, n_tokens):
    i = pl.program_id(0)
    td = x_ref.shape[0]

    def zero_copy(row):
        start = pl.multiple_of(row, MOE_ROW_TILE)
        return pltpu.make_async_copy(zero_ref, xs_ref.at[pl.ds(start, MOE_ROW_TILE)], zsem)

    @pl.when(i == 0)
    def _():
        zero_ref[...] = jnp.zeros(zero_ref.shape, F32)
        n_all_tiles = xs_ref.shape[0] // MOE_ROW_TILE

        def tail(t, c, wait):
            cp = zero_copy(t * MOE_ROW_TILE)
            cp.wait() if wait else cp.start()
            return c

        for wait in (False, True):
            for e in range(N_EXPERTS):
                @pl.when(last_row_ref[e] >= 0)
                def _():
                    cp = zero_copy(last_row_ref[e])
                    cp.wait() if wait else cp.start()
            lax.fori_loop(info_ref[0], n_all_tiles, functools.partial(tail, wait=wait), 0)

    h_ref[...] = _rmsnorm_rows(x_ref[...], g_ref[...])

    def row_copy(r, p):
        return pltpu.make_async_copy(h_ref.at[pl.ds(r, 1)], xs_ref.at[pl.ds(p, 1)], sem)

    def issue(r, c):
        row_copy(r, pos_ref[i * td + r]).start()
        row_copy(r, pos_ref[n_tokens + i * td + r]).start()
        return c

    lax.fori_loop(0, td, issue, 0, unroll=8)

    def drain(r, c):
        row_copy(0, 0).wait()
        row_copy(0, 0).wait()
        return c

    lax.fori_loop(0, td, drain, 0, unroll=8)


def _dispatch(pos_flat, last_row, info, x2, g, n_rows):
    n = x2.shape[0]
    td = DISPATCH_TILE
    return pl.pallas_call(
        functools.partial(_dispatch_kernel, n_tokens=n),
        grid_spec=pltpu.PrefetchScalarGridSpec(
            num_scalar_prefetch=3,
            grid=(n // td,),
            in_specs=[
                pl.BlockSpec((td, D_MODEL), lambda i, pos, last, info: (i, 0)),
                pl.BlockSpec((1, D_MODEL), lambda i, pos, last, info: (0, 0)),
            ],
            out_specs=pl.BlockSpec(memory_space=pl.ANY),
            scratch_shapes=[
                pltpu.VMEM((td, D_MODEL), F32),
                pltpu.VMEM((MOE_ROW_TILE, D_MODEL), F32),
                pltpu.SemaphoreType.DMA,
                pltpu.SemaphoreType.DMA,
            ],
        ),
        out_shape=jax.ShapeDtypeStruct((n_rows, D_MODEL), F32),
        compiler_params=pltpu.CompilerParams(
            dimension_semantics=("arbitrary",), vmem_limit_bytes=VMEM_LIMIT, has_side_effects=True),
        name="moe_dispatch",
    )(pos_flat, last_row, info, x2, g)


def _expert_kernel(tile_expert_ref, info_ref, xs_ref, wg_ref, wu_ref, wd_ref, ys_ref, wg_bf, wu_bf, wd_bf):
    j = pl.program_id(0)
    live = j < info_ref[0]
    e = tile_expert_ref[j]
    prev = tile_expert_ref[jnp.maximum(j - 1, 0)]

    @pl.when(live & ((j == 0) | (e != prev)))
    def _():
        wg_bf[...] = wg_ref[...].astype(BF16)
        wu_bf[...] = wu_ref[...].astype(BF16)
        wd_bf[...] = wd_ref[...].astype(BF16)

    @pl.when(jnp.logical_not(live))
    def _():
        ys_ref[...] = jnp.zeros(ys_ref.shape, F32)

    @pl.when(live)
    def _():
        x = xs_ref[...].astype(BF16)
        gate = jnp.dot(x, wg_bf[...], preferred_element_type=F32)
        up = jnp.dot(x, wu_bf[...], preferred_element_type=F32)
        hid = (gate * jax.nn.sigmoid(gate) * up).astype(BF16)
        ys_ref[...] = jnp.dot(hid, wd_bf[...], preferred_element_type=F32)


def _experts(tile_expert, info, xs, w_gate, w_up, w_down, layer):
    n_rows = xs.shape[0]
    tm = MOE_ROW_TILE
    rows = lambda j, te, info: (jnp.minimum(j, info[0] - 1), 0)
    wsel = lambda j, te, info: (layer, te[j], 0, 0)
    return pl.pallas_call(
        _expert_kernel,
        grid_spec=pltpu.PrefetchScalarGridSpec(
            num_scalar_prefetch=2,
            grid=(n_rows // tm,),
            in_specs=[
                pl.BlockSpec((tm, D_MODEL), rows),
                pl.BlockSpec((None, None, D_MODEL, D_EXPERT), wsel),
                pl.BlockSpec((None, None, D_MODEL, D_EXPERT), wsel),
                pl.BlockSpec((None, None, D_EXPERT, D_MODEL), wsel),
            ],
            out_specs=pl.BlockSpec((tm, D_MODEL), lambda j, te, info: (j, 0)),
            scratch_shapes=[
                pltpu.VMEM((D_MODEL, D_EXPERT), BF16),
                pltpu.VMEM((D_MODEL, D_EXPERT), BF16),
                pltpu.VMEM((D_EXPERT, D_MODEL), BF16),
            ],
        ),
        out_shape=jax.ShapeDtypeStruct((n_rows, D_MODEL), F32),
        compiler_params=pltpu.CompilerParams(dimension_semantics=("arbitrary",), vmem_limit_bytes=VMEM_LIMIT),
        name="moe_experts",
    )(tile_expert, info, xs, w_gate, w_up, w_down)


def _combine_kernel(pos_ref, x_ref, wts_ref, ys_ref, o_ref, rows_ref, sem, *, n_tokens):
    i = pl.program_id(0)
    tc = x_ref.shape[0]

    def row_copy(slot, r, p):
        return pltpu.make_async_copy(ys_ref.at[pl.ds(p, 1)], rows_ref.at[slot, pl.ds(r, 1)], sem)

    def issue(r, c):
        row_copy(0, r, pos_ref[i * tc + r]).start()
        row_copy(1, r, pos_ref[n_tokens + i * tc + r]).start()
        return c

    lax.fori_loop(0, tc, issue, 0, unroll=8)

    eye = jnp.where(lax.broadcasted_iota(I32, (tc, tc), 0) == lax.broadcasted_iota(I32, (tc, tc), 1),
                    1.0, 0.0).astype(BF16)
    w_hi, w_lo = _split_bf16(wts_ref[...])
    w_col = (lax.dot_general(eye, w_hi, _NT, preferred_element_type=F32)
             + lax.dot_general(eye, w_lo, _NT, preferred_element_type=F32))

    def drain(r, c):
        row_copy(0, 0, 0).wait()
        row_copy(1, 0, 0).wait()
        return c

    lax.fori_loop(0, tc, drain, 0, unroll=8)
    o_ref[...] = x_ref[...] + w_col[:, 0:1] * rows_ref[0] + w_col[:, 1:2] * rows_ref[1]


def _combine(pos_flat, x2, wts, ys):
    n = x2.shape[0]
    tc = COMBINE_TILE
    return pl.pallas_call(
        functools.partial(_combine_kernel, n_tokens=n),
        grid_spec=pltpu.PrefetchScalarGridSpec(
            num_scalar_prefetch=1,
            grid=(n // tc,),
            in_specs=[
                pl.BlockSpec((tc, D_MODEL), lambda i, pos: (i, 0)),
                pl.BlockSpec((8, tc), lambda i, pos: (0, i)),
                pl.BlockSpec(memory_space=pl.ANY),
            ],
            out_specs=pl.BlockSpec((tc, D_MODEL), lambda i, pos: (i, 0)),
            scratch_shapes=[pltpu.VMEM((2, tc, D_MODEL), F32), pltpu.SemaphoreType.DMA],
        ),
        out_shape=jax.ShapeDtypeStruct((n, D_MODEL), F32),
        compiler_params=pltpu.CompilerParams(dimension_semantics=("arbitrary",), vmem_limit_bytes=VMEM_LIMIT),
        name="moe_combine",
    )(pos_flat, x2, wts, ys)


def _hier_moe(x2, g, w_rg, b_rg, w_re, b_re, w_gate, w_up, w_down, layer):
    n = x2.shape[0]
    max_tiles = (2 * n) // MOE_ROW_TILE + N_EXPERTS
    wr = jnp.concatenate([
        jnp.transpose(w_re, (0, 2, 1)).reshape(N_EXPERTS, D_MODEL),
        w_rg.T,
        jnp.zeros((ROUTER_ROWS - N_EXPERTS - N_GROUPS, D_MODEL), F32)], axis=0)
    br = jnp.concatenate([
        b_re.reshape(N_EXPERTS), b_rg, jnp.zeros((ROUTER_ROWS - N_EXPERTS - N_GROUPS,), F32)])[:, None]

    meta, wts, counts = _route(x2, g, wr, br)
    pos, tile_expert, info, last_row = _positions(counts, meta, max_tiles)
    pos_flat = pos.reshape(2 * n)
    xs = _dispatch(pos_flat, last_row, info, x2, g, max_tiles * MOE_ROW_TILE)
    ys = _experts(tile_expert, info, xs, w_gate, w_up, w_down, layer)
    return _combine(pos_flat, x2, wts, ys)


def _rope_tables(seq):
    pos = jnp.arange(seq, dtype=F32)
    inv_freq = ROPE_THETA ** (-jnp.arange(0, ROT_DIM, 2, dtype=F32) / ROT_DIM)
    ang = pos[:, None] * inv_freq[None, :]
    cos, sin = jnp.cos(ang), jnp.sin(ang)
    half = ROT_DIM // 2
    ones = jnp.ones((seq, HEAD_DIM - ROT_DIM), F32)
    zeros_h = jnp.zeros((seq, half), F32)
    zeros_r = jnp.zeros((seq, HEAD_DIM - ROT_DIM), F32)
    cos_c = jnp.concatenate([cos, cos, ones], axis=1)
    sinp_c = jnp.concatenate([zeros_h, sin, zeros_r], axis=1)
    sinm_c = jnp.concatenate([-sin, zeros_h, zeros_r], axis=1)
    dup = lambda t: jnp.concatenate([t, t], axis=1)
    return dup(cos_c), dup(sinp_c), dup(sinm_c)


def kernel(x, attn_norm, diff_w_in, diff_q_gain, diff_k_gain, diff_lambda_q1, diff_lambda_k1,
           diff_lambda_q2, diff_lambda_k2, diff_subln, diff_w_out,
           conv_norm, conv_w_in, conv_b_in, conv_dw_w, conv_dw_b, conv_ln_g, conv_ln_b,
           conv_w_out, conv_b_out,
           ffn_norm, router_group_w, router_group_b, router_expert_w, router_expert_b,
           moe_w_gate, moe_w_up, moe_w_down):
    batch, seq, d = x.shape
    assert d == D_MODEL and seq % ATTN_TILE == 0 and seq % CONV_TILE == 0
    n = batch * seq
    depth = ffn_norm.shape[0]
    cos_t, sinp_t, sinm_t = _rope_tables(seq)
    row = lambda v: v.reshape(1, -1)
    x2 = x.reshape(n, d)
    for i in range(depth):
        j = i // 2
        if i % 2 == 0:
            lambda_init = 0.8 - 0.6 * math.exp(-0.3 * i)
            dup = lambda v: jnp.concatenate([v, v]).reshape(1, HEAD_W)
            q2, k2, vt4 = _qkv_project(x2, row(attn_norm[j]), diff_w_in[j].astype(BF16),
                                       dup(diff_q_gain[j]), dup(diff_k_gain[j]), cos_t, sinp_t, sinm_t,
                                       batch, seq)
            lam_params = jnp.stack([diff_lambda_q1[j], diff_lambda_k1[j], diff_lambda_q2[j], diff_lambda_k2[j]])
            score_bound = (SCORE_BOUND_UNIT * jnp.max(jnp.abs(diff_q_gain[j]))
                           * jnp.max(jnp.abs(diff_k_gain[j]))).reshape(1)
            o3 = _diff_attention(score_bound, q2.reshape(batch, seq, d), k2.reshape(batch, seq, d), vt4,
                                 lam_params, row(diff_subln[j]), lambda_init)
            x2 = _proj_residual(x2, o3.reshape(n, d), diff_w_out[j].astype(BF16))
        else:
            dw_w = jnp.concatenate([conv_dw_w[j], jnp.zeros((CONV_HALO - CONV_WIDTH, d), F32)], axis=0)
            x2 = _conv_block(x2.reshape(batch, seq, d), row(conv_norm[j]), conv_w_in[j].astype(BF16),
                             row(conv_b_in[j]), dw_w, row(conv_dw_b[j]), row(conv_ln_g[j]), row(conv_ln_b[j]),
                             conv_w_out[j].astype(BF16), row(conv_b_out[j])).reshape(n, d)
        x2 = _hier_moe(x2, row(ffn_norm[i]), router_group_w[i], router_group_b[i],
                       router_expert_w[i], router_expert_b[i], moe_w_gate, moe_w_up, moe_w_down, i)
    return x2.reshape(batch, seq, d)
```

```python
import functools
import math

import jax
import jax.numpy as jnp
from jax import lax
from jax.experimental import pallas as pl
from jax.experimental.pallas import tpu as pltpu

F32 = jnp.float32
BF16 = jnp.bfloat16
I32 = jnp.int32

D_MODEL = 1024
HEAD_DIM = 64
HEAD_W = 2 * HEAD_DIM
N_HEADS = D_MODEL // HEAD_W
ROT_DIM = HEAD_DIM // 4
ROPE_THETA = 500000.0
CHUNK = 64
CONV_WIDTH = 31
SUBLANES = 8
CONV_HALO = 32
N_GROUPS = 4
EXPERTS_PER_GROUP = 8
N_EXPERTS = N_GROUPS * EXPERTS_PER_GROUP
D_EXPERT = D_MODEL // 2
EPS = 1e-6
ROUTER_ROWS = 64
LOG2_E = math.log2(math.e)
Q_SCALE = HEAD_DIM ** -0.5 * LOG2_E
SCORE_BOUND_UNIT = 1.02 * HEAD_DIM * Q_SCALE
UNSHIFTED_SOFTMAX_LIMIT = 50.0

ATTN_TILE = 512
ATTN_GROUPS = (4, 2, 1)
ATTN_KEY_BLOCK = 512
ROW_TILE = 512
CONV_TILE = 256
ROUTER_TILE = 512
MOE_ROW_TILE = 256
DISPATCH_TILE = 256
COMBINE_TILE = 256
VMEM_LIMIT = 48 * 1024 * 1024

_NT = (((1,), (1,)), ((), ()))


def _rmsnorm_rows(x, g):
    return x * lax.rsqrt(jnp.mean(x * x, axis=-1, keepdims=True) + EPS) * g


def _qkv_kernel(x_ref, g_ref, w_ref, qg_ref, kg_ref, cos_ref, sinp_ref, sinm_ref,
                q_ref, k_ref, vt_ref, *, col_tile):
    tm = x_ref.shape[0]
    h = _rmsnorm_rows(x_ref[...], g_ref[...]).astype(BF16)
    lane = lax.broadcasted_iota(I32, (tm, HEAD_W), 1)
    lo = lane < HEAD_DIM
    cosv, sinp, sinm = cos_ref[...], sinp_ref[...], sinm_ref[...]

    def norm_rope(y, gain, scale):
        y2 = y * y
        s_lo = jnp.sum(jnp.where(lo, y2, 0.0), axis=-1, keepdims=True)
        s_hi = jnp.sum(y2, axis=-1, keepdims=True) - s_lo
        r = jnp.where(lo, lax.rsqrt(s_lo * (1.0 / HEAD_DIM) + EPS), lax.rsqrt(s_hi * (1.0 / HEAD_DIM) + EPS))
        yn = y * r * gain
        half = ROT_DIM // 2
        out = yn * cosv + pltpu.roll(yn, half, 1) * sinp + pltpu.roll(yn, HEAD_W - half, 1) * sinm
        return out * scale

    for c in range(3 * D_MODEL // col_tile):
        y = jnp.dot(h, w_ref[:, c * col_tile:(c + 1) * col_tile], preferred_element_type=F32)
        for s in range(col_tile // HEAD_W):
            col = c * col_tile + s * HEAD_W
            ys = y[:, s * HEAD_W:(s + 1) * HEAD_W]
            if col < D_MODEL:
                q_ref[:, col:col + HEAD_W] = norm_rope(ys, qg_ref[...], Q_SCALE).astype(BF16)
            elif col < 2 * D_MODEL:
                k_ref[:, col - D_MODEL:col - D_MODEL + HEAD_W] = norm_rope(ys, kg_ref[...], 1.0).astype(BF16)
            else:
                vt_ref[col - 2 * D_MODEL:col - 2 * D_MODEL + HEAD_W, :] = ys.T.astype(BF16)


def _qkv_project(x2, g, w_bf, q_gain, k_gain, cos_t, sinp_t, sinm_t, batch, seq):
    n = x2.shape[0]
    tm = ATTN_TILE
    tiles_per_seq = seq // tm
    row = lambda i: (i, 0)
    fixed = lambda i: (0, 0)
    tab = lambda i: (i % tiles_per_seq, 0)
    return pl.pallas_call(
        functools.partial(_qkv_kernel, col_tile=512),
        grid=(n // tm,),
        in_specs=[
            pl.BlockSpec((tm, D_MODEL), row),
            pl.BlockSpec((1, D_MODEL), fixed),
            pl.BlockSpec((D_MODEL, 3 * D_MODEL), fixed),
            pl.BlockSpec((1, HEAD_W), fixed),
            pl.BlockSpec((1, HEAD_W), fixed),
            pl.BlockSpec((tm, HEAD_W), tab),
            pl.BlockSpec((tm, HEAD_W), tab),
            pl.BlockSpec((tm, HEAD_W), tab),
        ],
        out_specs=[
            pl.BlockSpec((tm, D_MODEL), row),
            pl.BlockSpec((tm, D_MODEL), row),
            pl.BlockSpec((None, None, D_MODEL, tm), lambda i: (i // tiles_per_seq, i % tiles_per_seq, 0, 0)),
        ],
        out_shape=[
            jax.ShapeDtypeStruct((n, D_MODEL), BF16),
            jax.ShapeDtypeStruct((n, D_MODEL), BF16),
            jax.ShapeDtypeStruct((batch, tiles_per_seq, D_MODEL, tm), BF16),
        ],
        compiler_params=pltpu.CompilerParams(dimension_semantics=("arbitrary",), vmem_limit_bytes=VMEM_LIMIT),
        name="qkv_project",
    )(x2, g, w_bf, q_gain, k_gain, cos_t, sinp_t, sinm_t)


def _attn_kernel(bound_ref, lam_ref, q_ref, k_ref, vt_ref, g_ref, o_ref, m_ref, l_ref, acc_ref, *, lambda_init):
    t = q_ref.shape[0]
    i = pl.program_id(2)
    q = q_ref[...]
    lane = lax.broadcasted_iota(I32, (t, HEAD_W), 1)
    zero = jnp.zeros_like(q)
    q_comp = (jnp.where(lane < HEAD_DIM, q, zero), jnp.where(lane >= HEAD_DIM, q, zero))

    l_ref[...] = jnp.zeros(l_ref.shape, F32)
    acc_ref[...] = jnp.zeros(acc_ref.shape, F32)

    def tiles(j):
        return k_ref[pl.ds(pl.multiple_of(j * t, t), t), :], vt_ref[j]

    def chunk_mask(key0=0, n_keys=t):
        key_chunk = (key0 + lax.broadcasted_iota(I32, (n_keys, t), 0)) // CHUNK
        qry_chunk = lax.broadcasted_iota(I32, (n_keys, t), 1) // CHUNK
        return key_chunk <= qry_chunk

    def unshifted_step(j, masked):
        kt, vt = tiles(j)
        kb = ATTN_KEY_BLOCK
        for c in range(2):
            l_new = l_ref[c]
            acc_new = acc_ref[c]
            for s0 in range(0, t, kb):
                s = lax.dot_general(kt[s0:s0 + kb], q_comp[c], _NT, preferred_element_type=F32)
                p = jnp.exp2(s)
                if masked:
                    p = jnp.where(chunk_mask(s0, kb), p, 0.0)
                l_new = l_new + jnp.sum(p, axis=0, keepdims=True)
                acc_new = acc_new + jnp.dot(vt[:, s0:s0 + kb], p.astype(BF16), preferred_element_type=F32)
            l_ref[c] = l_new
            acc_ref[c] = acc_new

    def online_step(j, masked):
        kt, vt = tiles(j)
        for c in range(2):
            s = lax.dot_general(kt, q_comp[c], _NT, preferred_element_type=F32)
            if masked:
                s = jnp.where(chunk_mask(), s, -jnp.inf)
            m_old = m_ref[c]
            m_new = jnp.maximum(m_old, jnp.max(s, axis=0, keepdims=True))
            alpha = jnp.exp2(m_old - m_new)
            p = jnp.exp2(s - m_new)
            l_ref[c] = alpha * l_ref[c] + jnp.sum(p, axis=0, keepdims=True)
            acc_ref[c] = alpha * acc_ref[c] + jnp.dot(vt, p.astype(BF16), preferred_element_type=F32)
            m_ref[c] = m_new

    def sweep(step, groups):
        done = 0
        for group in groups:
            def trip(jj, carry, group=group, done=done):
                for u in range(group):
                    step(done + jj * group + u, False)
                return carry

            n_trips = (i - done) // group
            lax.fori_loop(0, n_trips, trip, 0)
            done = done + n_trips * group
        step(i, True)

    unshifted_ok = bound_ref[0] <= UNSHIFTED_SOFTMAX_LIMIT

    @pl.when(unshifted_ok)
    def _():
        sweep(unshifted_step, ATTN_GROUPS)

    @pl.when(jnp.logical_not(unshifted_ok))
    def _():
        m_ref[...] = jnp.full(m_ref.shape, -jnp.inf, F32)
        sweep(online_step, (1,))

    lam_p = lam_ref[...]
    lam = (jnp.exp(jnp.sum(lam_p[0:1] * lam_p[1:2], axis=-1, keepdims=True))
           - jnp.exp(jnp.sum(lam_p[2:3] * lam_p[3:4], axis=-1, keepdims=True)) + lambda_init)
    ot = acc_ref[0] * (1.0 / l_ref[0]) - lam * (acc_ref[1] * (1.0 / l_ref[1]))
    ot = ot * lax.rsqrt(jnp.mean(ot * ot, axis=0, keepdims=True) + EPS)
    o_ref[...] = (ot.T * g_ref[...] * (1.0 - lambda_init)).astype(o_ref.dtype)


def _diff_attention(score_bound, q3, k3, vt4, lam_params, subln_g, lambda_init):
    batch, seq, _ = q3.shape
    t = ATTN_TILE
    nq = seq // t
    return pl.pallas_call(
        functools.partial(_attn_kernel, lambda_init=lambda_init),
        grid_spec=pltpu.PrefetchScalarGridSpec(
            num_scalar_prefetch=1,
            grid=(batch, N_HEADS, nq),
            in_specs=[
                pl.BlockSpec((4, HEAD_DIM), lambda b, h, i, bound: (0, 0)),
                pl.BlockSpec((None, t, HEAD_W), lambda b, h, i, bound: (b, i, h)),
                pl.BlockSpec((None, seq, HEAD_W), lambda b, h, i, bound: (b, 0, h)),
                pl.BlockSpec((None, nq, HEAD_W, t), lambda b, h, i, bound: (b, 0, h, 0)),
                pl.BlockSpec((1, HEAD_W), lambda b, h, i, bound: (0, 0)),
            ],
            out_specs=pl.BlockSpec((None, t, HEAD_W), lambda b, h, i, bound: (b, i, h)),
            scratch_shapes=[
                pltpu.VMEM((2, 1, t), F32),
                pltpu.VMEM((2, 1, t), F32),
                pltpu.VMEM((2, HEAD_W, t), F32),
            ],
        ),
        out_shape=jax.ShapeDtypeStruct((batch, seq, D_MODEL), BF16),
        compiler_params=pltpu.CompilerParams(
            dimension_semantics=("arbitrary", "arbitrary", "arbitrary"), vmem_limit_bytes=VMEM_LIMIT),
        name="diff_attention",
    )(score_bound, lam_params, q3, k3, vt4, subln_g)


def _proj_residual_kernel(x_ref, y_ref, w_ref, o_ref):
    o_ref[...] = x_ref[...] + jnp.dot(y_ref[...], w_ref[...], preferred_element_type=F32)


def _proj_residual(x2, y2, w_bf):
    n = x2.shape[0]
    tm = ROW_TILE
    return pl.pallas_call(
        _proj_residual_kernel,
        grid=(n // tm,),
        in_specs=[
            pl.BlockSpec((tm, D_MODEL), lambda i: (i, 0)),
            pl.BlockSpec((tm, D_MODEL), lambda i: (i, 0)),
            pl.BlockSpec((D_MODEL, D_MODEL), lambda i: (0, 0)),
        ],
        out_specs=pl.BlockSpec((tm, D_MODEL), lambda i: (i, 0)),
        out_shape=jax.ShapeDtypeStruct((n, D_MODEL), F32),
        compiler_params=pltpu.CompilerParams(dimension_semantics=("arbitrary",), vmem_limit_bytes=VMEM_LIMIT),
        name="attn_out_proj",
    )(x2, y2, w_bf)


def _conv_kernel(x_ref, g_ref, win_ref, bin_ref, dww_ref, dwb_ref, lng_ref, lnb_ref, wout_ref, bout_ref,
                 o_ref, u_ref):
    tm = x_ref.shape[0]
    x = x_ref[...]
    h = _rmsnorm_rows(x, g_ref[...]).astype(BF16)
    a = jnp.dot(h, win_ref[:, :D_MODEL], preferred_element_type=F32) + bin_ref[:, :D_MODEL]
    gate = jnp.dot(h, win_ref[:, D_MODEL:], preferred_element_type=F32) + bin_ref[:, D_MODEL:]
    u = a * jax.nn.sigmoid(gate)

    @pl.when(pl.program_id(1) == 0)
    def _():
        u_ref[0:CONV_HALO, :] = jnp.zeros((CONV_HALO, D_MODEL), F32)

    u_ref[CONV_HALO:CONV_HALO + tm, :] = u
    base = CONV_HALO - (CONV_WIDTH - 1)
    window = u_ref[...]
    n_rows = tm + CONV_HALO
    acc = jnp.zeros((tm, D_MODEL), F32) + dwb_ref[...]
    for phase in range(SUBLANES):
        rolled = window if phase == 0 else pltpu.roll(window, n_rows - phase, 0)
        for row0 in range(0, CONV_HALO + 1, SUBLANES):
            w = row0 + phase - base
            if 0 <= w < CONV_WIDTH:
                acc = acc + dww_ref[w:w + 1, :] * rolled[row0:row0 + tm, :]
    u_ref[0:CONV_HALO, :] = u_ref[tm:tm + CONV_HALO, :]

    mu = jnp.mean(acc, axis=-1, keepdims=True)
    xc = acc - mu
    var = jnp.mean(xc * xc, axis=-1, keepdims=True)
    v = xc * lax.rsqrt(var + EPS) * lng_ref[...] + lnb_ref[...]
    v = (v * jax.nn.sigmoid(v)).astype(BF16)
    o_ref[...] = x + jnp.dot(v, wout_ref[...], preferred_element_type=F32) + bout_ref[...]


def _conv_block(x3, g, win_bf, b_in, dw_w, dw_b, ln_g, ln_b, wout_bf, b_out):
    batch, seq, _ = x3.shape
    tm = CONV_TILE
    row = lambda b, s: (b, s, 0)
    fixed = lambda b, s: (0, 0)
    vec = pl.BlockSpec((1, D_MODEL), fixed)
    return pl.pallas_call(
        _conv_kernel,
        grid=(batch, seq // tm),
        in_specs=[
            pl.BlockSpec((None, tm, D_MODEL), row),
            vec,
            pl.BlockSpec((D_MODEL, 2 * D_MODEL), fixed),
            pl.BlockSpec((1, 2 * D_MODEL), fixed),
            pl.BlockSpec((CONV_HALO, D_MODEL), fixed),
            vec, vec, vec,
            pl.BlockSpec((D_MODEL, D_MODEL), fixed),
            vec,
        ],
        out_specs=pl.BlockSpec((None, tm, D_MODEL), row),
        out_shape=jax.ShapeDtypeStruct((batch, seq, D_MODEL), F32),
        scratch_shapes=[pltpu.VMEM((tm + CONV_HALO, D_MODEL), F32)],
        compiler_params=pltpu.CompilerParams(
            dimension_semantics=("arbitrary", "arbitrary"), vmem_limit_bytes=VMEM_LIMIT),
        name="conformer_conv",
    )(x3, g, win_bf, b_in, dw_w, dw_b, ln_g, ln_b, wout_bf, b_out)


def _split_bf16(v):
    hi = v.astype(BF16)
    lo = (v - hi.astype(F32)).astype(BF16)
    return hi, lo


def _router_kernel(x_ref, g_ref, wr_ref, br_ref, meta_ref, wts_ref, cnt_ref, carry_ref, tri_ref):
    t = x_ref.shape[0]

    @pl.when(pl.program_id(0) == 0)
    def _():
        carry_ref[...] = jnp.zeros(carry_ref.shape, F32)
        before = lax.broadcasted_iota(I32, (t, t), 0) < lax.broadcasted_iota(I32, (t, t), 1)
        tri_ref[...] = jnp.where(before, 1.0, 0.0).astype(BF16)

    h = _rmsnorm_rows(x_ref[...], g_ref[...])
    h_hi, h_lo = _split_bf16(h)
    w_hi, w_lo = _split_bf16(wr_ref[...])
    logits = (lax.dot_general(w_hi, h_hi, _NT, preferred_element_type=F32)
              + lax.dot_general(w_hi, h_lo, _NT, preferred_element_type=F32)
              + lax.dot_general(w_lo, h_hi, _NT, preferred_element_type=F32)) + br_ref[...]

    gl = logits[N_EXPERTS:N_EXPERTS + N_GROUPS]
    gmax = jnp.max(gl, axis=0, keepdims=True)
    g_p = 1.0 / jnp.sum(jnp.exp(gl - gmax), axis=0, keepdims=True)
    row_g = lax.broadcasted_iota(I32, gl.shape, 0)
    g_idx = jnp.min(jnp.where(gl == gmax, row_g, N_GROUPS), axis=0, keepdims=True)

    e_sel = jnp.zeros((EXPERTS_PER_GROUP, t), F32)
    for grp in range(N_GROUPS):
        e_sel = e_sel + jnp.where(g_idx == grp, logits[grp * EXPERTS_PER_GROUP:(grp + 1) * EXPERTS_PER_GROUP], 0.0)
    row_e = lax.broadcasted_iota(I32, e_sel.shape, 0)
    m1 = jnp.max(e_sel, axis=0, keepdims=True)
    i1 = jnp.min(jnp.where(e_sel == m1, row_e, EXPERTS_PER_GROUP), axis=0, keepdims=True)
    rest = jnp.where(row_e == i1, -jnp.inf, e_sel)
    m2 = jnp.max(rest, axis=0, keepdims=True)
    i2 = jnp.min(jnp.where(rest == m2, row_e, EXPERTS_PER_GROUP), axis=0, keepdims=True)
    r = jnp.exp(m2 - m1)
    p1 = 1.0 / (1.0 + r)
    f1 = g_idx * EXPERTS_PER_GROUP + i1
    f2 = g_idx * EXPERTS_PER_GROUP + i2

    row_x = lax.broadcasted_iota(I32, (N_EXPERTS, t), 0)
    oh1 = row_x == f1
    oh2 = row_x == f2
    onehot = jnp.where(oh1 | oh2, 1.0, 0.0)
    prior = jnp.dot(onehot.astype(BF16), tri_ref[...], preferred_element_type=F32) + carry_ref[:, 0:1]
    rank1 = jnp.sum(jnp.where(oh1, prior, 0.0), axis=0, keepdims=True)
    rank2 = jnp.sum(jnp.where(oh2, prior, 0.0), axis=0, keepdims=True)
    total = carry_ref[...] + jnp.sum(onehot, axis=1, keepdims=True)
    carry_ref[...] = total
    cnt_ref[...] = total.astype(I32)

    zi = jnp.zeros((4, t), I32)
    meta_ref[...] = jnp.concatenate([f1, f2, rank1.astype(I32), rank2.astype(I32), zi], axis=0)
    wts_ref[...] = jnp.concatenate([g_p * p1, g_p * (r * p1), jnp.zeros((6, t), F32)], axis=0)


def _route(x2, g, wr, br):
    n = x2.shape[0]
    t = ROUTER_TILE
    return pl.pallas_call(
        _router_kernel,
        grid=(n // t,),
        in_specs=[
            pl.BlockSpec((t, D_MODEL), lambda i: (i, 0)),
            pl.BlockSpec((1, D_MODEL), lambda i: (0, 0)),
            pl.BlockSpec((ROUTER_ROWS, D_MODEL), lambda i: (0, 0)),
            pl.BlockSpec((ROUTER_ROWS, 1), lambda i: (0, 0)),
        ],
        out_specs=[
            pl.BlockSpec((8, t), lambda i: (0, i)),
            pl.BlockSpec((8, t), lambda i: (0, i)),
            pl.BlockSpec((N_EXPERTS, 128), lambda i: (0, 0)),
        ],
        out_shape=[
            jax.ShapeDtypeStruct((8, n), I32),
            jax.ShapeDtypeStruct((8, n), F32),
            jax.ShapeDtypeStruct((N_EXPERTS, 128), I32),
        ],
        scratch_shapes=[pltpu.VMEM((N_EXPERTS, 128), F32), pltpu.VMEM((t, t), BF16)],
        compiler_params=pltpu.CompilerParams(dimension_semantics=("arbitrary",), vmem_limit_bytes=VMEM_LIMIT),
        name="moe_router",
    )(x2, g, wr, br)


def _positions_kernel(cnt_ref, meta_ref, pos_ref, tile_expert_ref, info_ref, tile_valid_ref, off_ref,
                      *, max_tiles):
    def per_expert(e, tiles_before):
        count = cnt_ref[e, 0]
        n_tiles = (count + (MOE_ROW_TILE - 1)) // MOE_ROW_TILE
        off_ref[e] = tiles_before * MOE_ROW_TILE

        def fill(k, c):
            tile_expert_ref[tiles_before + k] = e
            tile_valid_ref[tiles_before + k] = jnp.minimum(count - k * MOE_ROW_TILE, MOE_ROW_TILE)
            return c

        lax.fori_loop(0, n_tiles, fill, 0)
        return tiles_before + n_tiles

    used = lax.fori_loop(0, N_EXPERTS, per_expert, jnp.int32(0))
    info_ref[0] = used
    last_expert = tile_expert_ref[used - 1]

    def fill_tail(k, c):
        tile_expert_ref[k] = last_expert
        tile_valid_ref[k] = 0
        return c

    lax.fori_loop(used, max_tiles, fill_tail, 0)
    tile_expert_ref[max_tiles] = last_expert

    ids = meta_ref[0:2, :]
    pos = meta_ref[2:4, :]
    for e in range(N_EXPERTS):
        pos = pos + jnp.where(ids == e, off_ref[e], 0)
    pos_ref[...] = pos


def _positions(counts, meta, max_tiles):
    n = meta.shape[1]
    smem = pl.BlockSpec(memory_space=pltpu.SMEM)
    return pl.pallas_call(
        functools.partial(_positions_kernel, max_tiles=max_tiles),
        in_specs=[smem, pl.BlockSpec((8, n), lambda: (0, 0))],
        out_specs=[pl.BlockSpec((2, n), lambda: (0, 0)), smem, smem, smem],
        out_shape=[
            jax.ShapeDtypeStruct((2, n), I32),
            jax.ShapeDtypeStruct((max_tiles + 1,), I32),
            jax.ShapeDtypeStruct((1,), I32),
            jax.ShapeDtypeStruct((max_tiles,), I32),
        ],
        scratch_shapes=[pltpu.SMEM((N_EXPERTS,), I32)],
        name="moe_positions",
    )(counts, meta)


def _inverse_kernel(pos_ref, tile_expert_ref, tile_valid_ref, src_ref, dst_ref, *, n_tokens):
    tm = MOE_ROW_TILE

    def pad_tile(t, c):
        pad_base = 2 * n_tokens + tile_expert_ref[t] * tm

        def pad_row(r, c2):
            src_ref[t * tm + r] = 0
            dst_ref[t * tm + r] = pad_base + r
            return c2

        lax.fori_loop(tile_valid_ref[t], tm, pad_row, 0)
        return c

    lax.fori_loop(0, src_ref.shape[0] // tm, pad_tile, 0)

    def token(n, c):
        p1 = pos_ref[n]
        p2 = pos_ref[n_tokens + n]
        src_ref[p1] = n
        dst_ref[p1] = n
        src_ref[p2] = n
        dst_ref[p2] = n_tokens + n
        return c

    lax.fori_loop(0, n_tokens, token, 0, unroll=8)


def _inverse_map(pos_flat, tile_expert, tile_valid, n_rows):
    smem = pl.BlockSpec(memory_space=pltpu.SMEM)
    return pl.pallas_call(
        functools.partial(_inverse_kernel, n_tokens=pos_flat.shape[0] // 2),
        in_specs=[smem, smem, smem],
        out_specs=[smem, smem],
        out_shape=[jax.ShapeDtypeStruct((n_rows,), I32), jax.ShapeDtypeStruct((n_rows,), I32)],
        name="moe_inverse_map",
    )(pos_flat, tile_expert, tile_valid)


def _expert_kernel(tile_expert_ref, info_ref, src_ref, dst_ref, x_hbm, g_ref, wg_ref, wu_ref, wd_ref,
                   out_hbm, xbuf, ybuf, wg_bf, wu_bf, wd_bf, gsem, ssem, *, n_tokens):
    j = pl.program_id(0)
    used = info_ref[0]
    tm = MOE_ROW_TILE

    def start_gather(tile, s):
        for r in range(tm):
            pltpu.make_async_copy(x_hbm.at[pl.ds(src_ref[tile * tm + r], 1)], xbuf.at[s, pl.ds(r, 1)],
                                  gsem.at[s]).start()

    def wait_gather(s):
        pltpu.make_async_copy(x_hbm.at[pl.ds(0, tm)], xbuf.at[s], gsem.at[s]).wait()

    def start_scatter(tile, s):
        for r in range(tm):
            pltpu.make_async_copy(ybuf.at[s, pl.ds(r, 1)], out_hbm.at[pl.ds(dst_ref[tile * tm + r], 1)],
                                  ssem.at[s]).start(priority=1)

    def wait_scatter(s):
        pltpu.make_async_copy(ybuf.at[s], out_hbm.at[pl.ds(0, tm)], ssem.at[s]).wait()

    def cast_weights():
        wg_bf[...] = wg_ref[...].astype(BF16)
        wu_bf[...] = wu_ref[...].astype(BF16)
        wd_bf[...] = wd_ref[...].astype(BF16)

    def mlp(s):
        x = _rmsnorm_rows(xbuf[s], g_ref[...]).astype(BF16)
        gate = jnp.dot(x, wg_bf[...], preferred_element_type=F32)
        up = jnp.dot(x, wu_bf[...], preferred_element_type=F32)
        hid = (gate * jax.nn.sigmoid(gate) * up).astype(BF16)
        ybuf[s] = jnp.dot(hid, wd_bf[...], preferred_element_type=F32)

    @pl.when(j == 0)
    def _():
        def pad_rows_copy(e):
            return pltpu.make_async_copy(ybuf.at[1], out_hbm.at[pl.ds(2 * n_tokens + e * tm, tm)], ssem.at[1])

        ybuf[1] = jnp.zeros((tm, D_MODEL), F32)
        for e in range(N_EXPERTS):
            pad_rows_copy(e).start()
        start_gather(0, 0)
        for e in range(N_EXPERTS):
            pad_rows_copy(e).wait()
        wait_gather(0)
        cast_weights()
        start_gather(jnp.minimum(1, used - 1), 1)
        mlp(0)

    for s in range(2):
        @pl.when((j >= 1) & (j < used) & (lax.rem(j, 2) == s))
        def _():
            wait_gather(s)

            @pl.when(j >= 2)
            def _():
                wait_scatter(s)

            @pl.when(tile_expert_ref[j] != tile_expert_ref[j - 1])
            def _():
                cast_weights()

            start_gather(jnp.minimum(j + 1, used - 1), 1 - s)
            start_scatter(j - 1, 1 - s)
            mlp(s)

        @pl.when((j == used) & (lax.rem(j, 2) == s))
        def _():
            wait_gather(s)

            @pl.when(j >= 2)
            def _():
                wait_scatter(s)

            start_scatter(j - 1, 1 - s)
            wait_scatter(1 - s)


def _experts(tile_expert, info, src, dst, x2, g, w_gate, w_up, w_down, layer):
    n = x2.shape[0]
    tm = MOE_ROW_TILE
    wsel = lambda j, te, info, src, dst: (layer, te[j], 0, 0)
    return pl.pallas_call(
        functools.partial(_expert_kernel, n_tokens=n),
        grid_spec=pltpu.PrefetchScalarGridSpec(
            num_scalar_prefetch=4,
            grid=(src.shape[0] // tm + 1,),
            in_specs=[
                pl.BlockSpec(memory_space=pl.ANY),
                pl.BlockSpec((1, D_MODEL), lambda j, te, info, src, dst: (0, 0)),
                pl.BlockSpec((None, None, D_MODEL, D_EXPERT), wsel),
                pl.BlockSpec((None, None, D_MODEL, D_EXPERT), wsel),
                pl.BlockSpec((None, None, D_EXPERT, D_MODEL), wsel),
            ],
            out_specs=pl.BlockSpec(memory_space=pl.ANY),
            scratch_shapes=[
                pltpu.VMEM((2, tm, D_MODEL), F32),
                pltpu.VMEM((2, tm, D_MODEL), F32),
                pltpu.VMEM((D_MODEL, D_EXPERT), BF16),
                pltpu.VMEM((D_MODEL, D_EXPERT), BF16),
                pltpu.VMEM((D_EXPERT, D_MODEL), BF16),
                pltpu.SemaphoreType.DMA((2,)),
                pltpu.SemaphoreType.DMA((2,)),
            ],
        ),
        out_shape=jax.ShapeDtypeStruct((2 * n + N_EXPERTS * tm, D_MODEL), F32),
        compiler_params=pltpu.CompilerParams(
            dimension_semantics=("arbitrary",), vmem_limit_bytes=VMEM_LIMIT, has_side_effects=True),
        name="moe_experts",
    )(tile_expert, info, src, dst, x2, g, w_gate, w_up, w_down)


def _combine_kernel(x_ref, wts_ref, y1_ref, y2_ref, o_ref):
    tc = x_ref.shape[0]
    eye = jnp.where(lax.broadcasted_iota(I32, (tc, tc), 0) == lax.broadcasted_iota(I32, (tc, tc), 1),
                    1.0, 0.0).astype(BF16)
    w_hi, w_lo = _split_bf16(wts_ref[...])
    w_col = (lax.dot_general(eye, w_hi, _NT, preferred_element_type=F32)
             + lax.dot_general(eye, w_lo, _NT, preferred_element_type=F32))
    o_ref[...] = x_ref[...] + w_col[:, 0:1] * y1_ref[...] + w_col[:, 1:2] * y2_ref[...]


def _combine(x2, wts, ys):
    n = x2.shape[0]
    tc = COMBINE_TILE
    return pl.pallas_call(
        _combine_kernel,
        grid=(n // tc,),
        in_specs=[
            pl.BlockSpec((tc, D_MODEL), lambda i: (i, 0)),
            pl.BlockSpec((8, tc), lambda i: (0, i)),
            pl.BlockSpec((tc, D_MODEL), lambda i: (i, 0)),
            pl.BlockSpec((tc, D_MODEL), lambda i: (i + n // tc, 0)),
        ],
        out_specs=pl.BlockSpec((tc, D_MODEL), lambda i: (i, 0)),
        out_shape=jax.ShapeDtypeStruct((n, D_MODEL), F32),
        compiler_params=pltpu.CompilerParams(dimension_semantics=("arbitrary",), vmem_limit_bytes=VMEM_LIMIT),
        name="moe_combine",
    )(x2, wts, ys, ys)


def _hier_moe(x2, g, w_rg, b_rg, w_re, b_re, w_gate, w_up, w_down, layer):
    n = x2.shape[0]
    max_tiles = (2 * n) // MOE_ROW_TILE + N_EXPERTS
    wr = jnp.concatenate([
        jnp.transpose(w_re, (0, 2, 1)).reshape(N_EXPERTS, D_MODEL),
        w_rg.T,
        jnp.zeros((ROUTER_ROWS - N_EXPERTS - N_GROUPS, D_MODEL), F32)], axis=0)
    br = jnp.concatenate([
        b_re.reshape(N_EXPERTS), b_rg, jnp.zeros((ROUTER_ROWS - N_EXPERTS - N_GROUPS,), F32)])[:, None]

    meta, wts, counts = _route(x2, g, wr, br)
    pos, tile_expert, info, tile_valid = _positions(counts, meta, max_tiles)
    src, dst = _inverse_map(pos.reshape(2 * n), tile_expert, tile_valid, max_tiles * MOE_ROW_TILE)
    ys = _experts(tile_expert, info, src, dst, x2, g, w_gate, w_up, w_down, layer)
    return _combine(x2, wts, ys)


def _rope_tables(seq):
    pos = jnp.arange(seq, dtype=F32)
    inv_freq = ROPE_THETA ** (-jnp.arange(0, ROT_DIM, 2, dtype=F32) / ROT_DIM)
    ang = pos[:, None] * inv_freq[None, :]
    cos, sin = jnp.cos(ang), jnp.sin(ang)
    half = ROT_DIM // 2
    ones = jnp.ones((seq, HEAD_DIM - ROT_DIM), F32)
    zeros_h = jnp.zeros((seq, half), F32)
    zeros_r = jnp.zeros((seq, HEAD_DIM - ROT_DIM), F32)
    cos_c = jnp.concatenate([cos, cos, ones], axis=1)
    sinp_c = jnp.concatenate([zeros_h, sin, zeros_r], axis=1)
    sinm_c = jnp.concatenate([-sin, zeros_h, zeros_r], axis=1)
    dup = lambda t: jnp.concatenate([t, t], axis=1)
    return dup(cos_c), dup(sinp_c), dup(sinm_c)


def kernel(x, attn_norm, diff_w_in, diff_q_gain, diff_k_gain, diff_lambda_q1, diff_lambda_k1,
           diff_lambda_q2, diff_lambda_k2, diff_subln, diff_w_out,
           conv_norm, conv_w_in, conv_b_in, conv_dw_w, conv_dw_b, conv_ln_g, conv_ln_b,
           conv_w_out, conv_b_out,
           ffn_norm, router_group_w, router_group_b, router_expert_w, router_expert_b,
           moe_w_gate, moe_w_up, moe_w_down):
    batch, seq, d = x.shape
    assert d == D_MODEL and seq % ATTN_TILE == 0 and seq % CONV_TILE == 0
    n = batch * seq
    depth = ffn_norm.shape[0]
    cos_t, sinp_t, sinm_t = _rope_tables(seq)
    row = lambda v: v.reshape(1, -1)
    x2 = x.reshape(n, d)
    for i in range(depth):
        j = i // 2
        if i % 2 == 0:
            lambda_init = 0.8 - 0.6 * math.exp(-0.3 * i)
            dup = lambda v: jnp.concatenate([v, v]).reshape(1, HEAD_W)
            q2, k2, vt4 = _qkv_project(x2, row(attn_norm[j]), diff_w_in[j].astype(BF16),
                                       dup(diff_q_gain[j]), dup(diff_k_gain[j]), cos_t, sinp_t, sinm_t,
                                       batch, seq)
            lam_params = jnp.stack([diff_lambda_q1[j], diff_lambda_k1[j], diff_lambda_q2[j], diff_lambda_k2[j]])
            score_bound = (SCORE_BOUND_UNIT * jnp.max(jnp.abs(diff_q_gain[j]))
                           * jnp.max(jnp.abs(diff_k_gain[j]))).reshape(1)
            o3 = _diff_attention(score_bound, q2.reshape(batch, seq, d), k2.reshape(batch, seq, d), vt4,
                                 lam_params, row(diff_subln[j]), lambda_init)
            x2 = _proj_residual(x2, o3.reshape(n, d), diff_w_out[j].astype(BF16))
        else:
            dw_w = jnp.concatenate([conv_dw_w[j], jnp.zeros((CONV_HALO - CONV_WIDTH, d), F32)], axis=0)
            x2 = _conv_block(x2.reshape(batch, seq, d), row(conv_norm[j]), conv_w_in[j].astype(BF16),
                             row(conv_b_in[j]), dw_w, row(conv_dw_b[j]), row(conv_ln_g[j]), row(conv_ln_b[j]),
                             conv_w_out[j].astype(BF16), row(conv_b_out[j])).reshape(n, d)
        x2 = _hier_moe(x2, row(ffn_norm[i]), router_group_w[i], router_group_b[i],
                       router_expert_w[i], router_expert_b[i], moe_w_gate, moe_w_up, moe_w_down, i)
    return x2.reshape(batch, seq, d)
```

```python
import functools
import math

import jax
import jax.numpy as jnp
from jax import lax
from jax.experimental import pallas as pl
from jax.experimental.pallas import tpu as pltpu

F32 = jnp.float32
BF16 = jnp.bfloat16
I32 = jnp.int32

D_MODEL = 1024
HEAD_DIM = 64
HEAD_W = 2 * HEAD_DIM
N_HEADS = D_MODEL // HEAD_W
ROT_DIM = HEAD_DIM // 4
ROPE_THETA = 500000.0
CHUNK = 64
CONV_WIDTH = 31
SUBLANES = 8
CONV_HALO = 32
N_GROUPS = 4
EXPERTS_PER_GROUP = 8
N_EXPERTS = N_GROUPS * EXPERTS_PER_GROUP
D_EXPERT = D_MODEL // 2
EPS = 1e-6
ROUTER_ROWS = 64
LOG2_E = math.log2(math.e)
Q_SCALE = HEAD_DIM ** -0.5 * LOG2_E
SCORE_BOUND_UNIT = 1.02 * HEAD_DIM * Q_SCALE
UNSHIFTED_SOFTMAX_LIMIT = 50.0

ATTN_TILE = 512
ATTN_GROUPS = (4, 2, 1)
ATTN_KEY_BLOCK = 512
ROW_TILE = 512
CONV_TILE = 256
ROUTER_TILE = 512
MOE_ROW_TILE = 256
DISPATCH_TILE = 256
COMBINE_TILE = 256
VMEM_LIMIT = 48 * 1024 * 1024

_NT = (((1,), (1,)), ((), ()))


def _rmsnorm_rows(x, g):
    return x * lax.rsqrt(jnp.mean(x * x, axis=-1, keepdims=True) + EPS) * g


def _qkv_kernel(x_ref, g_ref, w_ref, qg_ref, kg_ref, cos_ref, sinp_ref, sinm_ref,
                q_ref, k_ref, vt_ref, *, col_tile):
    tm = x_ref.shape[0]
    h = _rmsnorm_rows(x_ref[...], g_ref[...]).astype(BF16)
    lane = lax.broadcasted_iota(I32, (tm, HEAD_W), 1)
    lo = lane < HEAD_DIM
    cosv, sinp, sinm = cos_ref[...], sinp_ref[...], sinm_ref[...]

    def norm_rope(y, gain, scale):
        y2 = y * y
        s_lo = jnp.sum(jnp.where(lo, y2, 0.0), axis=-1, keepdims=True)
        s_hi = jnp.sum(y2, axis=-1, keepdims=True) - s_lo
        r = jnp.where(lo, lax.rsqrt(s_lo * (1.0 / HEAD_DIM) + EPS), lax.rsqrt(s_hi * (1.0 / HEAD_DIM) + EPS))
        yn = y * r * gain
        half = ROT_DIM // 2
        out = yn * cosv + pltpu.roll(yn, half, 1) * sinp + pltpu.roll(yn, HEAD_W - half, 1) * sinm
        return out * scale

    for c in range(3 * D_MODEL // col_tile):
        y = jnp.dot(h, w_ref[:, c * col_tile:(c + 1) * col_tile], preferred_element_type=F32)
        for s in range(col_tile // HEAD_W):
            col = c * col_tile + s * HEAD_W
            ys = y[:, s * HEAD_W:(s + 1) * HEAD_W]
            if col < D_MODEL:
                q_ref[:, col:col + HEAD_W] = norm_rope(ys, qg_ref[...], Q_SCALE).astype(BF16)
            elif col < 2 * D_MODEL:
                k_ref[:, col - D_MODEL:col - D_MODEL + HEAD_W] = norm_rope(ys, kg_ref[...], 1.0).astype(BF16)
            else:
                vt_ref[col - 2 * D_MODEL:col - 2 * D_MODEL + HEAD_W, :] = ys.T.astype(BF16)


def _qkv_project(x2, g, w_bf, q_gain, k_gain, cos_t, sinp_t, sinm_t, batch, seq):
    n = x2.shape[0]
    tm = ATTN_TILE
    tiles_per_seq = seq // tm
    row = lambda i: (i, 0)
    fixed = lambda i: (0, 0)
    tab = lambda i: (i % tiles_per_seq, 0)
    return pl.pallas_call(
        functools.partial(_qkv_kernel, col_tile=512),
        grid=(n // tm,),
        in_specs=[
            pl.BlockSpec((tm, D_MODEL), row),
            pl.BlockSpec((1, D_MODEL), fixed),
            pl.BlockSpec((D_MODEL, 3 * D_MODEL), fixed),
            pl.BlockSpec((1, HEAD_W), fixed),
            pl.BlockSpec((1, HEAD_W), fixed),
            pl.BlockSpec((tm, HEAD_W), tab),
            pl.BlockSpec((tm, HEAD_W), tab),
            pl.BlockSpec((tm, HEAD_W), tab),
        ],
        out_specs=[
            pl.BlockSpec((tm, D_MODEL), row),
            pl.BlockSpec((tm, D_MODEL), row),
            pl.BlockSpec((None, None, D_MODEL, tm), lambda i: (i // tiles_per_seq, i % tiles_per_seq, 0, 0)),
        ],
        out_shape=[
            jax.ShapeDtypeStruct((n, D_MODEL), BF16),
            jax.ShapeDtypeStruct((n, D_MODEL), BF16),
            jax.ShapeDtypeStruct((batch, tiles_per_seq, D_MODEL, tm), BF16),
        ],
        compiler_params=pltpu.CompilerParams(dimension_semantics=("arbitrary",), vmem_limit_bytes=VMEM_LIMIT),
        name="qkv_project",
    )(x2, g, w_bf, q_gain, k_gain, cos_t, sinp_t, sinm_t)


def _attn_kernel(bound_ref, lam_ref, q_ref, k_ref, vt_ref, g_ref, o_ref, m_ref, l_ref, acc_ref, *, lambda_init):
    t = q_ref.shape[0]
    i = pl.program_id(2)
    q = q_ref[...]
    lane = lax.broadcasted_iota(I32, (t, HEAD_W), 1)
    zero = jnp.zeros_like(q)
    q_comp = (jnp.where(lane < HEAD_DIM, q, zero), jnp.where(lane >= HEAD_DIM, q, zero))

    l_ref[...] = jnp.zeros(l_ref.shape, F32)
    acc_ref[...] = jnp.zeros(acc_ref.shape, F32)

    def tiles(j):
        return k_ref[pl.ds(pl.multiple_of(j * t, t), t), :], vt_ref[j]

    def chunk_mask(key0=0, n_keys=t):
        key_chunk = (key0 + lax.broadcasted_iota(I32, (n_keys, t), 0)) // CHUNK
        qry_chunk = lax.broadcasted_iota(I32, (n_keys, t), 1) // CHUNK
        return key_chunk <= qry_chunk

    def unshifted_step(j, masked):
        kt, vt = tiles(j)
        kb = ATTN_KEY_BLOCK
        for c in range(2):
            l_new = l_ref[c]
            acc_new = acc_ref[c]
            for s0 in range(0, t, kb):
                s = lax.dot_general(kt[s0:s0 + kb], q_comp[c], _NT, preferred_element_type=F32)
                p = jnp.exp2(s)
                if masked:
                    p = jnp.where(chunk_mask(s0, kb), p, 0.0)
                l_new = l_new + jnp.sum(p, axis=0, keepdims=True)
                acc_new = acc_new + jnp.dot(vt[:, s0:s0 + kb], p.astype(BF16), preferred_element_type=F32)
            l_ref[c] = l_new
            acc_ref[c] = acc_new

    def online_step(j, masked):
        kt, vt = tiles(j)
        for c in range(2):
            s = lax.dot_general(kt, q_comp[c], _NT, preferred_element_type=F32)
            if masked:
                s = jnp.where(chunk_mask(), s, -jnp.inf)
            m_old = m_ref[c]
            m_new = jnp.maximum(m_old, jnp.max(s, axis=0, keepdims=True))
            alpha = jnp.exp2(m_old - m_new)
            p = jnp.exp2(s - m_new)
            l_ref[c] = alpha * l_ref[c] + jnp.sum(p, axis=0, keepdims=True)
            acc_ref[c] = alpha * acc_ref[c] + jnp.dot(vt, p.astype(BF16), preferred_element_type=F32)
            m_ref[c] = m_new

    def sweep(step, groups):
        done = 0
        for group in groups:
            def trip(jj, carry, group=group, done=done):
                for u in range(group):
                    step(done + jj * group + u, False)
                return carry

            n_trips = (i - done) // group
            lax.fori_loop(0, n_trips, trip, 0)
            done = done + n_trips * group
        step(i, True)

    unshifted_ok = bound_ref[0] <= UNSHIFTED_SOFTMAX_LIMIT

    @pl.when(unshifted_ok)
    def _():
        sweep(unshifted_step, ATTN_GROUPS)

    @pl.when(jnp.logical_not(unshifted_ok))
    def _():
        m_ref[...] = jnp.full(m_ref.shape, -jnp.inf, F32)
        sweep(online_step, (1,))

    lam_p = lam_ref[...]
    lam = (jnp.exp(jnp.sum(lam_p[0:1] * lam_p[1:2], axis=-1, keepdims=True))
           - jnp.exp(jnp.sum(lam_p[2:3] * lam_p[3:4], axis=-1, keepdims=True)) + lambda_init)
    ot = acc_ref[0] * (1.0 / l_ref[0]) - lam * (acc_ref[1] * (1.0 / l_ref[1]))
    ot = ot * lax.rsqrt(jnp.mean(ot * ot, axis=0, keepdims=True) + EPS)
    o_ref[...] = (ot.T * g_ref[...] * (1.0 - lambda_init)).astype(o_ref.dtype)


def _diff_attention(score_bound, q3, k3, vt4, lam_params, subln_g, lambda_init):
    batch, seq, _ = q3.shape
    t = ATTN_TILE
    nq = seq // t
    return pl.pallas_call(
        functools.partial(_attn_kernel, lambda_init=lambda_init),
        grid_spec=pltpu.PrefetchScalarGridSpec(
            num_scalar_prefetch=1,
            grid=(batch, N_HEADS, nq),
            in_specs=[
                pl.BlockSpec((4, HEAD_DIM), lambda b, h, i, bound: (0, 0)),
                pl.BlockSpec((None, t, HEAD_W), lambda b, h, i, bound: (b, i, h)),
                pl.BlockSpec((None, seq, HEAD_W), lambda b, h, i, bound: (b, 0, h)),
                pl.BlockSpec((None, nq, HEAD_W, t), lambda b, h, i, bound: (b, 0, h, 0)),
                pl.BlockSpec((1, HEAD_W), lambda b, h, i, bound: (0, 0)),
            ],
            out_specs=pl.BlockSpec((None, t, HEAD_W), lambda b, h, i, bound: (b, i, h)),
            scratch_shapes=[
                pltpu.VMEM((2, 1, t), F32),
                pltpu.VMEM((2, 1, t), F32),
                pltpu.VMEM((2, HEAD_W, t), F32),
            ],
        ),
        out_shape=jax.ShapeDtypeStruct((batch, seq, D_MODEL), BF16),
        compiler_params=pltpu.CompilerParams(
            dimension_semantics=("arbitrary", "arbitrary", "arbitrary"), vmem_limit_bytes=VMEM_LIMIT),
        name="diff_attention",
    )(score_bound, lam_params, q3, k3, vt4, subln_g)


def _proj_residual_kernel(x_ref, y_ref, w_ref, o_ref):
    o_ref[...] = x_ref[...] + jnp.dot(y_ref[...], w_ref[...], preferred_element_type=F32)


def _proj_residual(x2, y2, w_bf):
    n = x2.shape[0]
    tm = ROW_TILE
    return pl.pallas_call(
        _proj_residual_kernel,
        grid=(n // tm,),
        in_specs=[
            pl.BlockSpec((tm, D_MODEL), lambda i: (i, 0)),
            pl.BlockSpec((tm, D_MODEL), lambda i: (i, 0)),
            pl.BlockSpec((D_MODEL, D_MODEL), lambda i: (0, 0)),
        ],
        out_specs=pl.BlockSpec((tm, D_MODEL), lambda i: (i, 0)),
        out_shape=jax.ShapeDtypeStruct((n, D_MODEL), F32),
        compiler_params=pltpu.CompilerParams(dimension_semantics=("arbitrary",), vmem_limit_bytes=VMEM_LIMIT),
        name="attn_out_proj",
    )(x2, y2, w_bf)


def _conv_kernel(x_ref, g_ref, win_ref, bin_ref, dww_ref, dwb_ref, lng_ref, lnb_ref, wout_ref, bout_ref,
                 o_ref, u_ref):
    tm = x_ref.shape[0]
    x = x_ref[...]
    h = _rmsnorm_rows(x, g_ref[...]).astype(BF16)
    a = jnp.dot(h, win_ref[:, :D_MODEL], preferred_element_type=F32) + bin_ref[:, :D_MODEL]
    gate = jnp.dot(h, win_ref[:, D_MODEL:], preferred_element_type=F32) + bin_ref[:, D_MODEL:]
    u = a * jax.nn.sigmoid(gate)

    @pl.when(pl.program_id(1) == 0)
    def _():
        u_ref[0:CONV_HALO, :] = jnp.zeros((CONV_HALO, D_MODEL), F32)

    u_ref[CONV_HALO:CONV_HALO + tm, :] = u
    base = CONV_HALO - (CONV_WIDTH - 1)
    window = u_ref[...]
    n_rows = tm + CONV_HALO
    acc = jnp.zeros((tm, D_MODEL), F32) + dwb_ref[...]
    for phase in range(SUBLANES):
        rolled = window if phase == 0 else pltpu.roll(window, n_rows - phase, 0)
        for row0 in range(0, CONV_HALO + 1, SUBLANES):
            w = row0 + phase - base
            if 0 <= w < CONV_WIDTH:
                acc = acc + dww_ref[w:w + 1, :] * rolled[row0:row0 + tm, :]
    u_ref[0:CONV_HALO, :] = u_ref[tm:tm + CONV_HALO, :]

    mu = jnp.mean(acc, axis=-1, keepdims=True)
    xc = acc - mu
    var = jnp.mean(xc * xc, axis=-1, keepdims=True)
    v = xc * lax.rsqrt(var + EPS) * lng_ref[...] + lnb_ref[...]
    v = (v * jax.nn.sigmoid(v)).astype(BF16)
    o_ref[...] = x + jnp.dot(v, wout_ref[...], preferred_element_type=F32) + bout_ref[...]


def _conv_block(x3, g, win_bf, b_in, dw_w, dw_b, ln_g, ln_b, wout_bf, b_out):
    batch, seq, _ = x3.shape
    tm = CONV_TILE
    row = lambda b, s: (b, s, 0)
    fixed = lambda b, s: (0, 0)
    vec = pl.BlockSpec((1, D_MODEL), fixed)
    return pl.pallas_call(
        _conv_kernel,
        grid=(batch, seq // tm),
        in_specs=[
            pl.BlockSpec((None, tm, D_MODEL), row),
            vec,
            pl.BlockSpec((D_MODEL, 2 * D_MODEL), fixed),
            pl.BlockSpec((1, 2 * D_MODEL), fixed),
            pl.BlockSpec((CONV_HALO, D_MODEL), fixed),
            vec, vec, vec,
            pl.BlockSpec((D_MODEL, D_MODEL), fixed),
            vec,
        ],
        out_specs=pl.BlockSpec((None, tm, D_MODEL), row),
        out_shape=jax.ShapeDtypeStruct((batch, seq, D_MODEL), F32),
        scratch_shapes=[pltpu.VMEM((tm + CONV_HALO, D_MODEL), F32)],
        compiler_params=pltpu.CompilerParams(
            dimension_semantics=("arbitrary", "arbitrary"), vmem_limit_bytes=VMEM_LIMIT),
        name="conformer_conv",
    )(x3, g, win_bf, b_in, dw_w, dw_b, ln_g, ln_b, wout_bf, b_out)


def _store_token_major(ref, v, lead=()):
    t = v.shape[0]
    for s in range(SUBLANES):
        ref[lead + (pl.ds(s, t, stride=SUBLANES), slice(None))] = v[:, s * 128:(s + 1) * 128]


def _load_token_major(ref, t, lead=()):
    return jnp.concatenate(
        [ref[lead + (pl.ds(s, t, stride=SUBLANES), slice(None))] for s in range(SUBLANES)], axis=1)


def _split_bf16(v):
    hi = v.astype(BF16)
    lo = (v - hi.astype(F32)).astype(BF16)
    return hi, lo


def _router_kernel(x_ref, g_ref, wr_ref, br_ref, meta_ref, wts_ref, cnt_ref, h3_ref, carry_ref, tri_ref):
    t = x_ref.shape[0]

    @pl.when(pl.program_id(0) == 0)
    def _():
        carry_ref[...] = jnp.zeros(carry_ref.shape, F32)
        before = lax.broadcasted_iota(I32, (t, t), 0) < lax.broadcasted_iota(I32, (t, t), 1)
        tri_ref[...] = jnp.where(before, 1.0, 0.0).astype(BF16)

    h = _rmsnorm_rows(x_ref[...], g_ref[...])
    _store_token_major(h3_ref, h)
    h_hi, h_lo = _split_bf16(h)
    w_hi, w_lo = _split_bf16(wr_ref[...])
    logits = (lax.dot_general(w_hi, h_hi, _NT, preferred_element_type=F32)
              + lax.dot_general(w_hi, h_lo, _NT, preferred_element_type=F32)
              + lax.dot_general(w_lo, h_hi, _NT, preferred_element_type=F32)) + br_ref[...]

    gl = logits[N_EXPERTS:N_EXPERTS + N_GROUPS]
    gmax = jnp.max(gl, axis=0, keepdims=True)
    g_p = 1.0 / jnp.sum(jnp.exp(gl - gmax), axis=0, keepdims=True)
    row_g = lax.broadcasted_iota(I32, gl.shape, 0)
    g_idx = jnp.min(jnp.where(gl == gmax, row_g, N_GROUPS), axis=0, keepdims=True)

    e_sel = jnp.zeros((EXPERTS_PER_GROUP, t), F32)
    for grp in range(N_GROUPS):
        e_sel = e_sel + jnp.where(g_idx == grp, logits[grp * EXPERTS_PER_GROUP:(grp + 1) * EXPERTS_PER_GROUP], 0.0)
    row_e = lax.broadcasted_iota(I32, e_sel.shape, 0)
    m1 = jnp.max(e_sel, axis=0, keepdims=True)
    i1 = jnp.min(jnp.where(e_sel == m1, row_e, EXPERTS_PER_GROUP), axis=0, keepdims=True)
    rest = jnp.where(row_e == i1, -jnp.inf, e_sel)
    m2 = jnp.max(rest, axis=0, keepdims=True)
    i2 = jnp.min(jnp.where(rest == m2, row_e, EXPERTS_PER_GROUP), axis=0, keepdims=True)
    r = jnp.exp(m2 - m1)
    p1 = 1.0 / (1.0 + r)
    f1 = g_idx * EXPERTS_PER_GROUP + i1
    f2 = g_idx * EXPERTS_PER_GROUP + i2

    row_x = lax.broadcasted_iota(I32, (N_EXPERTS, t), 0)
    oh1 = row_x == f1
    oh2 = row_x == f2
    onehot = jnp.where(oh1 | oh2, 1.0, 0.0)
    prior = jnp.dot(onehot.astype(BF16), tri_ref[...], preferred_element_type=F32) + carry_ref[:, 0:1]
    rank1 = jnp.sum(jnp.where(oh1, prior, 0.0), axis=0, keepdims=True)
    rank2 = jnp.sum(jnp.where(oh2, prior, 0.0), axis=0, keepdims=True)
    total = carry_ref[...] + jnp.sum(onehot, axis=1, keepdims=True)
    carry_ref[...] = total
    cnt_ref[...] = total.astype(I32)

    zi = jnp.zeros((4, t), I32)
    meta_ref[...] = jnp.concatenate([f1, f2, rank1.astype(I32), rank2.astype(I32), zi], axis=0)
    wts_ref[...] = jnp.concatenate([g_p * p1, g_p * (r * p1), jnp.zeros((6, t), F32)], axis=0)


def _route(x2, g, wr, br):
    n = x2.shape[0]
    t = ROUTER_TILE
    return pl.pallas_call(
        _router_kernel,
        grid=(n // t,),
        in_specs=[
            pl.BlockSpec((t, D_MODEL), lambda i: (i, 0)),
            pl.BlockSpec((1, D_MODEL), lambda i: (0, 0)),
            pl.BlockSpec((ROUTER_ROWS, D_MODEL), lambda i: (0, 0)),
            pl.BlockSpec((ROUTER_ROWS, 1), lambda i: (0, 0)),
        ],
        out_specs=[
            pl.BlockSpec((8, t), lambda i: (0, i)),
            pl.BlockSpec((8, t), lambda i: (0, i)),
            pl.BlockSpec((N_EXPERTS, 128), lambda i: (0, 0)),
            pl.BlockSpec((t * SUBLANES, 128), lambda i: (i, 0)),
        ],
        out_shape=[
            jax.ShapeDtypeStruct((8, n), I32),
            jax.ShapeDtypeStruct((8, n), F32),
            jax.ShapeDtypeStruct((N_EXPERTS, 128), I32),
            jax.ShapeDtypeStruct((n * SUBLANES, 128), F32),
        ],
        scratch_shapes=[pltpu.VMEM((N_EXPERTS, 128), F32), pltpu.VMEM((t, t), BF16)],
        compiler_params=pltpu.CompilerParams(dimension_semantics=("arbitrary",), vmem_limit_bytes=VMEM_LIMIT),
        name="moe_router",
    )(x2, g, wr, br)


def _positions_kernel(cnt_ref, meta_ref, pos_ref, tile_expert_ref, info_ref, tile_valid_ref, off_ref,
                      *, max_tiles):
    def per_expert(e, tiles_before):
        count = cnt_ref[e, 0]
        n_tiles = (count + (MOE_ROW_TILE - 1)) // MOE_ROW_TILE
        off_ref[e] = tiles_before * MOE_ROW_TILE

        def fill(k, c):
            tile_expert_ref[tiles_before + k] = e
            tile_valid_ref[tiles_before + k] = jnp.minimum(count - k * MOE_ROW_TILE, MOE_ROW_TILE)
            return c

        lax.fori_loop(0, n_tiles, fill, 0)
        return tiles_before + n_tiles

    used = lax.fori_loop(0, N_EXPERTS, per_expert, jnp.int32(0))
    info_ref[0] = used
    last_expert = tile_expert_ref[used - 1]

    def fill_tail(k, c):
        tile_expert_ref[k] = last_expert
        tile_valid_ref[k] = 0
        return c

    lax.fori_loop(used, max_tiles, fill_tail, 0)
    tile_expert_ref[max_tiles] = last_expert

    ids = meta_ref[0:2, :]
    pos = meta_ref[2:4, :]
    for e in range(N_EXPERTS):
        pos = pos + jnp.where(ids == e, off_ref[e], 0)
    pos_ref[...] = pos


def _positions(counts, meta, max_tiles):
    n = meta.shape[1]
    smem = pl.BlockSpec(memory_space=pltpu.SMEM)
    return pl.pallas_call(
        functools.partial(_positions_kernel, max_tiles=max_tiles),
        in_specs=[smem, pl.BlockSpec((8, n), lambda: (0, 0))],
        out_specs=[pl.BlockSpec((2, n), lambda: (0, 0)), smem, smem, smem],
        out_shape=[
            jax.ShapeDtypeStruct((2, n), I32),
            jax.ShapeDtypeStruct((max_tiles + 1,), I32),
            jax.ShapeDtypeStruct((1,), I32),
            jax.ShapeDtypeStruct((max_tiles,), I32),
        ],
        scratch_shapes=[pltpu.SMEM((N_EXPERTS,), I32)],
        name="moe_positions",
    )(counts, meta)


def _inverse_kernel(pos_ref, tile_expert_ref, tile_valid_ref, dst_ref, *, n_tokens):
    tm = MOE_ROW_TILE

    def pad_tile(t, c):
        pad_base = 2 * n_tokens + tile_expert_ref[t] * tm

        def pad_row(r, c2):
            dst_ref[t * tm + r] = pad_base + r
            return c2

        lax.fori_loop(tile_valid_ref[t], tm, pad_row, 0)
        return c

    lax.fori_loop(0, dst_ref.shape[0] // tm, pad_tile, 0)

    def token(n, c):
        dst_ref[pos_ref[n]] = n
        dst_ref[pos_ref[n_tokens + n]] = n_tokens + n
        return c

    lax.fori_loop(0, n_tokens, token, 0, unroll=8)


def _inverse_map(pos_flat, tile_expert, tile_valid, n_rows):
    smem = pl.BlockSpec(memory_space=pltpu.SMEM)
    return pl.pallas_call(
        functools.partial(_inverse_kernel, n_tokens=pos_flat.shape[0] // 2),
        in_specs=[smem, smem, smem],
        out_specs=smem,
        out_shape=jax.ShapeDtypeStruct((n_rows,), I32),
        name="moe_inverse_map",
    )(pos_flat, tile_expert, tile_valid)


def _expert_kernel(tile_expert_ref, info_ref, dst_ref, x_hbm, wg_ref, wu_ref, wd_ref,
                   out_hbm, xbuf, ybuf, wg_bf, wu_bf, wd_bf, gsem, ssem, *, n_tokens):
    j = pl.program_id(0)
    used = info_ref[0]
    tm = MOE_ROW_TILE

    def token_rows(ref, token):
        return ref.at[pl.ds(pl.multiple_of(token * SUBLANES, SUBLANES), SUBLANES)]

    def start_gather(tile, s):
        for r in range(tm):
            token = dst_ref[tile * tm + r] & (n_tokens - 1)
            pltpu.make_async_copy(token_rows(x_hbm, token), xbuf.at[s, pl.ds(r * SUBLANES, SUBLANES)],
                                  gsem.at[s]).start()

    def wait_gather(s):
        pltpu.make_async_copy(x_hbm.at[pl.ds(0, tm * SUBLANES)], xbuf.at[s], gsem.at[s]).wait()

    def start_scatter(tile, s):
        for r in range(tm):
            pltpu.make_async_copy(ybuf.at[s, pl.ds(r * SUBLANES, SUBLANES)],
                                  token_rows(out_hbm, dst_ref[tile * tm + r]),
                                  ssem.at[s]).start(priority=1)

    def wait_scatter(s):
        pltpu.make_async_copy(ybuf.at[s], out_hbm.at[pl.ds(0, tm * SUBLANES)], ssem.at[s]).wait()

    def cast_weights():
        wg_bf[...] = wg_ref[...].astype(BF16)
        wu_bf[...] = wu_ref[...].astype(BF16)
        wd_bf[...] = wd_ref[...].astype(BF16)

    def mlp(s):
        x = _load_token_major(xbuf, tm, (s,)).astype(BF16)
        gate = jnp.dot(x, wg_bf[...], preferred_element_type=F32)
        up = jnp.dot(x, wu_bf[...], preferred_element_type=F32)
        hid = (gate * jax.nn.sigmoid(gate) * up).astype(BF16)
        _store_token_major(ybuf, jnp.dot(hid, wd_bf[...], preferred_element_type=F32), (s,))

    @pl.when(j == 0)
    def _():
        def pad_rows_copy(e):
            return pltpu.make_async_copy(
                ybuf.at[1], out_hbm.at[pl.ds((2 * n_tokens + e * tm) * SUBLANES, tm * SUBLANES)], ssem.at[1])

        ybuf[1] = jnp.zeros(ybuf.shape[1:], F32)
        for e in range(N_EXPERTS):
            pad_rows_copy(e).start()
        start_gather(0, 0)
        for e in range(N_EXPERTS):
            pad_rows_copy(e).wait()
        wait_gather(0)
        cast_weights()
        start_gather(jnp.minimum(1, used - 1), 1)
        mlp(0)

    for s in range(2):
        @pl.when((j >= 1) & (j < used) & (lax.rem(j, 2) == s))
        def _():
            wait_gather(s)

            @pl.when(j >= 2)
            def _():
                wait_scatter(s)

            @pl.when(tile_expert_ref[j] != tile_expert_ref[j - 1])
            def _():
                cast_weights()

            start_gather(jnp.minimum(j + 1, used - 1), 1 - s)
            start_scatter(j - 1, 1 - s)
            mlp(s)

        @pl.when((j == used) & (lax.rem(j, 2) == s))
        def _():
            wait_gather(s)

            @pl.when(j >= 2)
            def _():
                wait_scatter(s)

            start_scatter(j - 1, 1 - s)
            wait_scatter(1 - s)


def _experts(tile_expert, info, dst, h3, w_gate, w_up, w_down, layer):
    n = h3.shape[0] // SUBLANES
    assert n & (n - 1) == 0
    tm = MOE_ROW_TILE
    wsel = lambda j, te, info, dst: (layer, te[j], 0, 0)
    return pl.pallas_call(
        functools.partial(_expert_kernel, n_tokens=n),
        grid_spec=pltpu.PrefetchScalarGridSpec(
            num_scalar_prefetch=3,
            grid=(dst.shape[0] // tm + 1,),
            in_specs=[
                pl.BlockSpec(memory_space=pl.ANY),
                pl.BlockSpec((None, None, D_MODEL, D_EXPERT), wsel),
                pl.BlockSpec((None, None, D_MODEL, D_EXPERT), wsel),
                pl.BlockSpec((None, None, D_EXPERT, D_MODEL), wsel),
            ],
            out_specs=pl.BlockSpec(memory_space=pl.ANY),
            scratch_shapes=[
                pltpu.VMEM((2, tm * SUBLANES, 128), F32),
                pltpu.VMEM((2, tm * SUBLANES, 128), F32),
                pltpu.VMEM((D_MODEL, D_EXPERT), BF16),
                pltpu.VMEM((D_MODEL, D_EXPERT), BF16),
                pltpu.VMEM((D_EXPERT, D_MODEL), BF16),
                pltpu.SemaphoreType.DMA((2,)),
                pltpu.SemaphoreType.DMA((2,)),
            ],
        ),
        out_shape=jax.ShapeDtypeStruct(((2 * n + N_EXPERTS * tm) * SUBLANES, 128), F32),
        compiler_params=pltpu.CompilerParams(
            dimension_semantics=("arbitrary",), vmem_limit_bytes=VMEM_LIMIT, has_side_effects=True),
        name="moe_experts",
    )(tile_expert, info, dst, h3, w_gate, w_up, w_down)


def _combine_kernel(x_ref, wts_ref, y1_ref, y2_ref, o_ref):
    tc = x_ref.shape[0]
    eye = jnp.where(lax.broadcasted_iota(I32, (tc, tc), 0) == lax.broadcasted_iota(I32, (tc, tc), 1),
                    1.0, 0.0).astype(BF16)
    w_hi, w_lo = _split_bf16(wts_ref[...])
    w_col = (lax.dot_general(eye, w_hi, _NT, preferred_element_type=F32)
             + lax.dot_general(eye, w_lo, _NT, preferred_element_type=F32))
    o_ref[...] = (x_ref[...] + w_col[:, 0:1] * _load_token_major(y1_ref, tc)
                  + w_col[:, 1:2] * _load_token_major(y2_ref, tc))


def _combine(x2, wts, ys):
    n = x2.shape[0]
    tc = COMBINE_TILE
    return pl.pallas_call(
        _combine_kernel,
        grid=(n // tc,),
        in_specs=[
            pl.BlockSpec((tc, D_MODEL), lambda i: (i, 0)),
            pl.BlockSpec((8, tc), lambda i: (0, i)),
            pl.BlockSpec((tc * SUBLANES, 128), lambda i: (i, 0)),
            pl.BlockSpec((tc * SUBLANES, 128), lambda i: (i + n // tc, 0)),
        ],
        out_specs=pl.BlockSpec((tc, D_MODEL), lambda i: (i, 0)),
        out_shape=jax.ShapeDtypeStruct((n, D_MODEL), F32),
        compiler_params=pltpu.CompilerParams(dimension_semantics=("arbitrary",), vmem_limit_bytes=VMEM_LIMIT),
        name="moe_combine",
    )(x2, wts, ys, ys)


def _hier_moe(x2, g, w_rg, b_rg, w_re, b_re, w_gate, w_up, w_down, layer):
    n = x2.shape[0]
    max_tiles = (2 * n) // MOE_ROW_TILE + N_EXPERTS
    wr = jnp.concatenate([
        jnp.transpose(w_re, (0, 2, 1)).reshape(N_EXPERTS, D_MODEL),
        w_rg.T,
        jnp.zeros((ROUTER_ROWS - N_EXPERTS - N_GROUPS, D_MODEL), F32)], axis=0)
    br = jnp.concatenate([
        b_re.reshape(N_EXPERTS), b_rg, jnp.zeros((ROUTER_ROWS - N_EXPERTS - N_GROUPS,), F32)])[:, None]

    meta, wts, counts, h3 = _route(x2, g, wr, br)
    pos, tile_expert, info, tile_valid = _positions(counts, meta, max_tiles)
    dst = _inverse_map(pos.reshape(2 * n), tile_expert, tile_valid, max_tiles * MOE_ROW_TILE)
    ys = _experts(tile_expert, info, dst, h3, w_gate, w_up, w_down, layer)
    return _combine(x2, wts, ys)


def _rope_tables(seq):
    pos = jnp.arange(seq, dtype=F32)
    inv_freq = ROPE_THETA ** (-jnp.arange(0, ROT_DIM, 2, dtype=F32) / ROT_DIM)
    ang = pos[:, None] * inv_freq[None, :]
    cos, sin = jnp.cos(ang), jnp.sin(ang)
    half = ROT_DIM // 2
    ones = jnp.ones((seq, HEAD_DIM - ROT_DIM), F32)
    zeros_h = jnp.zeros((seq, half), F32)
    zeros_r = jnp.zeros((seq, HEAD_DIM - ROT_DIM), F32)
    cos_c = jnp.concatenate([cos, cos, ones], axis=1)
    sinp_c = jnp.concatenate([zeros_h, sin, zeros_r], axis=1)
    sinm_c = jnp.concatenate([-sin, zeros_h, zeros_r], axis=1)
    dup = lambda t: jnp.concatenate([t, t], axis=1)
    return dup(cos_c), dup(sinp_c), dup(sinm_c)


def kernel(x, attn_norm, diff_w_in, diff_q_gain, diff_k_gain, diff_lambda_q1, diff_lambda_k1,
           diff_lambda_q2, diff_lambda_k2, diff_subln, diff_w_out,
           conv_norm, conv_w_in, conv_b_in, conv_dw_w, conv_dw_b, conv_ln_g, conv_ln_b,
           conv_w_out, conv_b_out,
           ffn_norm, router_group_w, router_group_b, router_expert_w, router_expert_b,
           moe_w_gate, moe_w_up, moe_w_down):
    batch, seq, d = x.shape
    assert d == D_MODEL and seq % ATTN_TILE == 0 and seq % CONV_TILE == 0
    n = batch * seq
    depth = ffn_norm.shape[0]
    cos_t, sinp_t, sinm_t = _rope_tables(seq)
    row = lambda v: v.reshape(1, -1)
    x2 = x.reshape(n, d)
    for i in range(depth):
        j = i // 2
        if i % 2 == 0:
            lambda_init = 0.8 - 0.6 * math.exp(-0.3 * i)
            dup = lambda v: jnp.concatenate([v, v]).reshape(1, HEAD_W)
            q2, k2, vt4 = _qkv_project(x2, row(attn_norm[j]), diff_w_in[j].astype(BF16),
                                       dup(diff_q_gain[j]), dup(diff_k_gain[j]), cos_t, sinp_t, sinm_t,
                                       batch, seq)
            lam_params = jnp.stack([diff_lambda_q1[j], diff_lambda_k1[j], diff_lambda_q2[j], diff_lambda_k2[j]])
            score_bound = (SCORE_BOUND_UNIT * jnp.max(jnp.abs(diff_q_gain[j]))
                           * jnp.max(jnp.abs(diff_k_gain[j]))).reshape(1)
            o3 = _diff_attention(score_bound, q2.reshape(batch, seq, d), k2.reshape(batch, seq, d), vt4,
                                 lam_params, row(diff_subln[j]), lambda_init)
            x2 = _proj_residual(x2, o3.reshape(n, d), diff_w_out[j].astype(BF16))
        else:
            dw_w = jnp.concatenate([conv_dw_w[j], jnp.zeros((CONV_HALO - CONV_WIDTH, d), F32)], axis=0)
            x2 = _conv_block(x2.reshape(batch, seq, d), row(conv_norm[j]), conv_w_in[j].astype(BF16),
                             row(conv_b_in[j]), dw_w, row(conv_dw_b[j]), row(conv_ln_g[j]), row(conv_ln_b[j]),
                             conv_w_out[j].astype(BF16), row(conv_b_out[j])).reshape(n, d)
        x2 = _hier_moe(x2, row(ffn_norm[i]), router_group_w[i], router_group_b[i],
                       router_expert_w[i], router_expert_b[i], moe_w_gate, moe_w_up, moe_w_down, i)
    return x2.reshape(batch, seq, d)
```

```python
import functools
import math

import jax
import jax.numpy as jnp
from jax import lax
from jax.experimental import pallas as pl
from jax.experimental.pallas import tpu as pltpu

F32 = jnp.float32
BF16 = jnp.bfloat16
I32 = jnp.int32

D_MODEL = 1024
HEAD_DIM = 64
HEAD_W = 2 * HEAD_DIM
N_HEADS = D_MODEL // HEAD_W
ROT_DIM = HEAD_DIM // 4
ROPE_THETA = 500000.0
CHUNK = 64
CONV_WIDTH = 31
SUBLANES = 8
CONV_HALO = 32
N_GROUPS = 4
EXPERTS_PER_GROUP = 8
N_EXPERTS = N_GROUPS * EXPERTS_PER_GROUP
D_EXPERT = D_MODEL // 2
EPS = 1e-6
ROUTER_ROWS = 64
LOG2_E = math.log2(math.e)
Q_SCALE = HEAD_DIM ** -0.5 * LOG2_E
SCORE_BOUND_UNIT = 1.02 * HEAD_DIM * Q_SCALE
UNSHIFTED_SOFTMAX_LIMIT = 50.0

ATTN_TILE = 1024
ATTN_GROUPS = (2, 1)
ATTN_KEY_BLOCK = 1024
ROW_TILE = 512
CONV_TILE = 256
ROUTER_TILE = 512
MOE_ROW_TILE = 256
DISPATCH_TILE = 256
COMBINE_TILE = 256
VMEM_LIMIT = 48 * 1024 * 1024

_NT = (((1,), (1,)), ((), ()))


def _rmsnorm_rows(x, g):
    return x * lax.rsqrt(jnp.mean(x * x, axis=-1, keepdims=True) + EPS) * g


def _qkv_kernel(x_ref, g_ref, w_ref, qg_ref, kg_ref, cos_ref, sinp_ref, sinm_ref,
                q_ref, k_ref, vt_ref, *, col_tile):
    tm = x_ref.shape[0]
    h = _rmsnorm_rows(x_ref[...], g_ref[...]).astype(BF16)
    lane = lax.broadcasted_iota(I32, (tm, HEAD_W), 1)
    lo = lane < HEAD_DIM
    cosv, sinp, sinm = cos_ref[...], sinp_ref[...], sinm_ref[...]

    def norm_rope(y, gain, scale):
        y2 = y * y
        s_lo = jnp.sum(jnp.where(lo, y2, 0.0), axis=-1, keepdims=True)
        s_hi = jnp.sum(y2, axis=-1, keepdims=True) - s_lo
        r = jnp.where(lo, lax.rsqrt(s_lo * (1.0 / HEAD_DIM) + EPS), lax.rsqrt(s_hi * (1.0 / HEAD_DIM) + EPS))
        yn = y * r * gain
        half = ROT_DIM // 2
        out = yn * cosv + pltpu.roll(yn, half, 1) * sinp + pltpu.roll(yn, HEAD_W - half, 1) * sinm
        return out * scale

    for c in range(3 * D_MODEL // col_tile):
        y = jnp.dot(h, w_ref[:, c * col_tile:(c + 1) * col_tile], preferred_element_type=F32)
        for s in range(col_tile // HEAD_W):
            col = c * col_tile + s * HEAD_W
            ys = y[:, s * HEAD_W:(s + 1) * HEAD_W]
            if col < D_MODEL:
                q_ref[:, col:col + HEAD_W] = norm_rope(ys, qg_ref[...], Q_SCALE).astype(BF16)
            elif col < 2 * D_MODEL:
                k_ref[:, col - D_MODEL:col - D_MODEL + HEAD_W] = norm_rope(ys, kg_ref[...], 1.0).astype(BF16)
            else:
                vt_ref[col - 2 * D_MODEL:col - 2 * D_MODEL + HEAD_W, :] = ys.T.astype(BF16)


def _qkv_project(x2, g, w_bf, q_gain, k_gain, cos_t, sinp_t, sinm_t, batch, seq):
    n = x2.shape[0]
    tm = ATTN_TILE
    tiles_per_seq = seq // tm
    row = lambda i: (i, 0)
    fixed = lambda i: (0, 0)
    tab = lambda i: (i % tiles_per_seq, 0)
    return pl.pallas_call(
        functools.partial(_qkv_kernel, col_tile=512),
        grid=(n // tm,),
        in_specs=[
            pl.BlockSpec((tm, D_MODEL), row),
            pl.BlockSpec((1, D_MODEL), fixed),
            pl.BlockSpec((D_MODEL, 3 * D_MODEL), fixed),
            pl.BlockSpec((1, HEAD_W), fixed),
            pl.BlockSpec((1, HEAD_W), fixed),
            pl.BlockSpec((tm, HEAD_W), tab),
            pl.BlockSpec((tm, HEAD_W), tab),
            pl.BlockSpec((tm, HEAD_W), tab),
        ],
        out_specs=[
            pl.BlockSpec((tm, D_MODEL), row),
            pl.BlockSpec((tm, D_MODEL), row),
            pl.BlockSpec((None, None, D_MODEL, tm), lambda i: (i // tiles_per_seq, i % tiles_per_seq, 0, 0)),
        ],
        out_shape=[
            jax.ShapeDtypeStruct((n, D_MODEL), BF16),
            jax.ShapeDtypeStruct((n, D_MODEL), BF16),
            jax.ShapeDtypeStruct((batch, tiles_per_seq, D_MODEL, tm), BF16),
        ],
        compiler_params=pltpu.CompilerParams(dimension_semantics=("arbitrary",), vmem_limit_bytes=VMEM_LIMIT),
        name="qkv_project",
    )(x2, g, w_bf, q_gain, k_gain, cos_t, sinp_t, sinm_t)


def _attn_kernel(bound_ref, lam_ref, q_ref, k_ref, vt_ref, g_ref, o_ref, m_ref, l_ref, acc_ref, *, lambda_init):
    t = q_ref.shape[0]
    i = pl.program_id(2)
    q = q_ref[...]
    lane = lax.broadcasted_iota(I32, (t, HEAD_W), 1)
    zero = jnp.zeros_like(q)
    q_comp = (jnp.where(lane < HEAD_DIM, q, zero), jnp.where(lane >= HEAD_DIM, q, zero))

    l_ref[...] = jnp.zeros(l_ref.shape, F32)
    acc_ref[...] = jnp.zeros(acc_ref.shape, F32)

    def tiles(j):
        return k_ref[pl.ds(pl.multiple_of(j * t, t), t), :], vt_ref[j]

    def chunk_mask(key0=0, n_keys=t):
        key_chunk = (key0 + lax.broadcasted_iota(I32, (n_keys, t), 0)) // CHUNK
        qry_chunk = lax.broadcasted_iota(I32, (n_keys, t), 1) // CHUNK
        return key_chunk <= qry_chunk

    def unshifted_step(j, masked):
        kt, vt = tiles(j)
        kb = ATTN_KEY_BLOCK
        for c in range(2):
            l_new = l_ref[c]
            acc_new = acc_ref[c]
            for s0 in range(0, t, kb):
                s = lax.dot_general(kt[s0:s0 + kb], q_comp[c], _NT, preferred_element_type=F32)
                p = jnp.exp2(s)
                if masked:
                    p = jnp.where(chunk_mask(s0, kb), p, 0.0)
                l_new = l_new + jnp.sum(p, axis=0, keepdims=True)
                acc_new = acc_new + jnp.dot(vt[:, s0:s0 + kb], p.astype(BF16), preferred_element_type=F32)
            l_ref[c] = l_new
            acc_ref[c] = acc_new

    def online_step(j, masked):
        kt, vt = tiles(j)
        for c in range(2):
            s = lax.dot_general(kt, q_comp[c], _NT, preferred_element_type=F32)
            if masked:
                s = jnp.where(chunk_mask(), s, -jnp.inf)
            m_old = m_ref[c]
            m_new = jnp.maximum(m_old, jnp.max(s, axis=0, keepdims=True))
            alpha = jnp.exp2(m_old - m_new)
            p = jnp.exp2(s - m_new)
            l_ref[c] = alpha * l_ref[c] + jnp.sum(p, axis=0, keepdims=True)
            acc_ref[c] = alpha * acc_ref[c] + jnp.dot(vt, p.astype(BF16), preferred_element_type=F32)
            m_ref[c] = m_new

    def sweep(step, groups):
        done = 0
        for group in groups:
            def trip(jj, carry, group=group, done=done):
                for u in range(group):
                    step(done + jj * group + u, False)
                return carry

            n_trips = (i - done) // group
            lax.fori_loop(0, n_trips, trip, 0)
            done = done + n_trips * group
        step(i, True)

    unshifted_ok = bound_ref[0] <= UNSHIFTED_SOFTMAX_LIMIT

    @pl.when(unshifted_ok)
    def _():
        sweep(unshifted_step, ATTN_GROUPS)

    @pl.when(jnp.logical_not(unshifted_ok))
    def _():
        m_ref[...] = jnp.full(m_ref.shape, -jnp.inf, F32)
        sweep(online_step, (1,))

    lam_p = lam_ref[...]
    lam = (jnp.exp(jnp.sum(lam_p[0:1] * lam_p[1:2], axis=-1, keepdims=True))
           - jnp.exp(jnp.sum(lam_p[2:3] * lam_p[3:4], axis=-1, keepdims=True)) + lambda_init)
    ot = acc_ref[0] * (1.0 / l_ref[0]) - lam * (acc_ref[1] * (1.0 / l_ref[1]))
    ot = ot * lax.rsqrt(jnp.mean(ot * ot, axis=0, keepdims=True) + EPS)
    o_ref[...] = (ot.T * g_ref[...] * (1.0 - lambda_init)).astype(o_ref.dtype)


def _diff_attention(score_bound, q3, k3, vt4, lam_params, subln_g, lambda_init):
    batch, seq, _ = q3.shape
    t = ATTN_TILE
    nq = seq // t
    return pl.pallas_call(
        functools.partial(_attn_kernel, lambda_init=lambda_init),
        grid_spec=pltpu.PrefetchScalarGridSpec(
            num_scalar_prefetch=1,
            grid=(batch, N_HEADS, nq),
            in_specs=[
                pl.BlockSpec((4, HEAD_DIM), lambda b, h, i, bound: (0, 0)),
                pl.BlockSpec((None, t, HEAD_W), lambda b, h, i, bound: (b, i, h)),
                pl.BlockSpec((None, seq, HEAD_W), lambda b, h, i, bound: (b, 0, h)),
                pl.BlockSpec((None, nq, HEAD_W, t), lambda b, h, i, bound: (b, 0, h, 0)),
                pl.BlockSpec((1, HEAD_W), lambda b, h, i, bound: (0, 0)),
            ],
            out_specs=pl.BlockSpec((None, t, HEAD_W), lambda b, h, i, bound: (b, i, h)),
            scratch_shapes=[
                pltpu.VMEM((2, 1, t), F32),
                pltpu.VMEM((2, 1, t), F32),
                pltpu.VMEM((2, HEAD_W, t), F32),
            ],
        ),
        out_shape=jax.ShapeDtypeStruct((batch, seq, D_MODEL), BF16),
        compiler_params=pltpu.CompilerParams(
            dimension_semantics=("arbitrary", "arbitrary", "arbitrary"), vmem_limit_bytes=VMEM_LIMIT),
        name="diff_attention",
    )(score_bound, lam_params, q3, k3, vt4, subln_g)


def _proj_residual_kernel(x_ref, y_ref, w_ref, o_ref):
    o_ref[...] = x_ref[...] + jnp.dot(y_ref[...], w_ref[...], preferred_element_type=F32)


def _proj_residual(x2, y2, w_bf):
    n = x2.shape[0]
    tm = ROW_TILE
    return pl.pallas_call(
        _proj_residual_kernel,
        grid=(n // tm,),
        in_specs=[
            pl.BlockSpec((tm, D_MODEL), lambda i: (i, 0)),
            pl.BlockSpec((tm, D_MODEL), lambda i: (i, 0)),
            pl.BlockSpec((D_MODEL, D_MODEL), lambda i: (0, 0)),
        ],
        out_specs=pl.BlockSpec((tm, D_MODEL), lambda i: (i, 0)),
        out_shape=jax.ShapeDtypeStruct((n, D_MODEL), F32),
        compiler_params=pltpu.CompilerParams(dimension_semantics=("arbitrary",), vmem_limit_bytes=VMEM_LIMIT),
        name="attn_out_proj",
    )(x2, y2, w_bf)


def _conv_kernel(x_ref, g_ref, win_ref, bin_ref, dww_ref, dwb_ref, lng_ref, lnb_ref, wout_ref, bout_ref,
                 o_ref, u_ref):
    tm = x_ref.shape[0]
    x = x_ref[...]
    h = _rmsnorm_rows(x, g_ref[...]).astype(BF16)
    a = jnp.dot(h, win_ref[:, :D_MODEL], preferred_element_type=F32) + bin_ref[:, :D_MODEL]
    gate = jnp.dot(h, win_ref[:, D_MODEL:], preferred_element_type=F32) + bin_ref[:, D_MODEL:]
    u = a * jax.nn.sigmoid(gate)

    @pl.when(pl.program_id(1) == 0)
    def _():
        u_ref[0:CONV_HALO, :] = jnp.zeros((CONV_HALO, D_MODEL), F32)

    u_ref[CONV_HALO:CONV_HALO + tm, :] = u
    base = CONV_HALO - (CONV_WIDTH - 1)
    window = u_ref[...]
    n_rows = tm + CONV_HALO
    acc = jnp.zeros((tm, D_MODEL), F32) + dwb_ref[...]
    for phase in range(SUBLANES):
        rolled = window if phase == 0 else pltpu.roll(window, n_rows - phase, 0)
        for row0 in range(0, CONV_HALO + 1, SUBLANES):
            w = row0 + phase - base
            if 0 <= w < CONV_WIDTH:
                acc = acc + dww_ref[w:w + 1, :] * rolled[row0:row0 + tm, :]
    u_ref[0:CONV_HALO, :] = u_ref[tm:tm + CONV_HALO, :]

    mu = jnp.mean(acc, axis=-1, keepdims=True)
    xc = acc - mu
    var = jnp.mean(xc * xc, axis=-1, keepdims=True)
    v = xc * lax.rsqrt(var + EPS) * lng_ref[...] + lnb_ref[...]
    v = (v * jax.nn.sigmoid(v)).astype(BF16)
    o_ref[...] = x + jnp.dot(v, wout_ref[...], preferred_element_type=F32) + bout_ref[...]


def _conv_block(x3, g, win_bf, b_in, dw_w, dw_b, ln_g, ln_b, wout_bf, b_out):
    batch, seq, _ = x3.shape
    tm = CONV_TILE
    row = lambda b, s: (b, s, 0)
    fixed = lambda b, s: (0, 0)
    vec = pl.BlockSpec((1, D_MODEL), fixed)
    return pl.pallas_call(
        _conv_kernel,
        grid=(batch, seq // tm),
        in_specs=[
            pl.BlockSpec((None, tm, D_MODEL), row),
            vec,
            pl.BlockSpec((D_MODEL, 2 * D_MODEL), fixed),
            pl.BlockSpec((1, 2 * D_MODEL), fixed),
            pl.BlockSpec((CONV_HALO, D_MODEL), fixed),
            vec, vec, vec,
            pl.BlockSpec((D_MODEL, D_MODEL), fixed),
            vec,
        ],
        out_specs=pl.BlockSpec((None, tm, D_MODEL), row),
        out_shape=jax.ShapeDtypeStruct((batch, seq, D_MODEL), F32),
        scratch_shapes=[pltpu.VMEM((tm + CONV_HALO, D_MODEL), F32)],
        compiler_params=pltpu.CompilerParams(
            dimension_semantics=("arbitrary", "arbitrary"), vmem_limit_bytes=VMEM_LIMIT),
        name="conformer_conv",
    )(x3, g, win_bf, b_in, dw_w, dw_b, ln_g, ln_b, wout_bf, b_out)


def _store_token_major(ref, v, lead=()):
    t = v.shape[0]
    for s in range(SUBLANES):
        ref[lead + (pl.ds(s, t, stride=SUBLANES), slice(None))] = v[:, s * 128:(s + 1) * 128]


def _load_token_major(ref, t, lead=()):
    return jnp.concatenate(
        [ref[lead + (pl.ds(s, t, stride=SUBLANES), slice(None))] for s in range(SUBLANES)], axis=1)


def _split_bf16(v):
    hi = v.astype(BF16)
    lo = (v - hi.astype(F32)).astype(BF16)
    return hi, lo


def _router_kernel(x_ref, g_ref, wr_ref, br_ref, meta_ref, wts_ref, cnt_ref, h3_ref, carry_ref, tri_ref):
    t = x_ref.shape[0]

    @pl.when(pl.program_id(0) == 0)
    def _():
        carry_ref[...] = jnp.zeros(carry_ref.shape, F32)
        before = lax.broadcasted_iota(I32, (t, t), 0) < lax.broadcasted_iota(I32, (t, t), 1)
        tri_ref[...] = jnp.where(before, 1.0, 0.0).astype(BF16)

    h = _rmsnorm_rows(x_ref[...], g_ref[...])
    _store_token_major(h3_ref, h)
    h_hi, h_lo = _split_bf16(h)
    w_hi, w_lo = _split_bf16(wr_ref[...])
    logits = (lax.dot_general(w_hi, h_hi, _NT, preferred_element_type=F32)
              + lax.dot_general(w_hi, h_lo, _NT, preferred_element_type=F32)
              + lax.dot_general(w_lo, h_hi, _NT, preferred_element_type=F32)) + br_ref[...]

    gl = logits[N_EXPERTS:N_EXPERTS + N_GROUPS]
    gmax = jnp.max(gl, axis=0, keepdims=True)
    g_p = 1.0 / jnp.sum(jnp.exp(gl - gmax), axis=0, keepdims=True)
    row_g = lax.broadcasted_iota(I32, gl.shape, 0)
    g_idx = jnp.min(jnp.where(gl == gmax, row_g, N_GROUPS), axis=0, keepdims=True)

    e_sel = jnp.zeros((EXPERTS_PER_GROUP, t), F32)
    for grp in range(N_GROUPS):
        e_sel = e_sel + jnp.where(g_idx == grp, logits[grp * EXPERTS_PER_GROUP:(grp + 1) * EXPERTS_PER_GROUP], 0.0)
    row_e = lax.broadcasted_iota(I32, e_sel.shape, 0)
    m1 = jnp.max(e_sel, axis=0, keepdims=True)
    i1 = jnp.min(jnp.where(e_sel == m1, row_e, EXPERTS_PER_GROUP), axis=0, keepdims=True)
    rest = jnp.where(row_e == i1, -jnp.inf, e_sel)
    m2 = jnp.max(rest, axis=0, keepdims=True)
    i2 = jnp.min(jnp.where(rest == m2, row_e, EXPERTS_PER_GROUP), axis=0, keepdims=True)
    r = jnp.exp(m2 - m1)
    p1 = 1.0 / (1.0 + r)
    f1 = g_idx * EXPERTS_PER_GROUP + i1
    f2 = g_idx * EXPERTS_PER_GROUP + i2

    row_x = lax.broadcasted_iota(I32, (N_EXPERTS, t), 0)
    oh1 = row_x == f1
    oh2 = row_x == f2
    onehot = jnp.where(oh1 | oh2, 1.0, 0.0)
    prior = jnp.dot(onehot.astype(BF16), tri_ref[...], preferred_element_type=F32) + carry_ref[:, 0:1]
    rank1 = jnp.sum(jnp.where(oh1, prior, 0.0), axis=0, keepdims=True)
    rank2 = jnp.sum(jnp.where(oh2, prior, 0.0), axis=0, keepdims=True)
    total = carry_ref[...] + jnp.sum(onehot, axis=1, keepdims=True)
    carry_ref[...] = total
    cnt_ref[...] = total.astype(I32)

    zi = jnp.zeros((4, t), I32)
    meta_ref[...] = jnp.concatenate([f1, f2, rank1.astype(I32), rank2.astype(I32), zi], axis=0)
    wts_ref[...] = jnp.concatenate([g_p * p1, g_p * (r * p1), jnp.zeros((6, t), F32)], axis=0)


def _route(x2, g, wr, br):
    n = x2.shape[0]
    t = ROUTER_TILE
    return pl.pallas_call(
        _router_kernel,
        grid=(n // t,),
        in_specs=[
            pl.BlockSpec((t, D_MODEL), lambda i: (i, 0)),
            pl.BlockSpec((1, D_MODEL), lambda i: (0, 0)),
            pl.BlockSpec((ROUTER_ROWS, D_MODEL), lambda i: (0, 0)),
            pl.BlockSpec((ROUTER_ROWS, 1), lambda i: (0, 0)),
        ],
        out_specs=[
            pl.BlockSpec((8, t), lambda i: (0, i)),
            pl.BlockSpec((8, t), lambda i: (0, i)),
            pl.BlockSpec((N_EXPERTS, 128), lambda i: (0, 0)),
            pl.BlockSpec((t * SUBLANES, 128), lambda i: (i, 0)),
        ],
        out_shape=[
            jax.ShapeDtypeStruct((8, n), I32),
            jax.ShapeDtypeStruct((8, n), F32),
            jax.ShapeDtypeStruct((N_EXPERTS, 128), I32),
            jax.ShapeDtypeStruct((n * SUBLANES, 128), F32),
        ],
        scratch_shapes=[pltpu.VMEM((N_EXPERTS, 128), F32), pltpu.VMEM((t, t), BF16)],
        compiler_params=pltpu.CompilerParams(dimension_semantics=("arbitrary",), vmem_limit_bytes=VMEM_LIMIT),
        name="moe_router",
    )(x2, g, wr, br)


def _positions_kernel(cnt_ref, meta_ref, pos_ref, tile_expert_ref, info_ref, tile_valid_ref, off_ref,
                      *, max_tiles):
    def per_expert(e, tiles_before):
        count = cnt_ref[e, 0]
        n_tiles = (count + (MOE_ROW_TILE - 1)) // MOE_ROW_TILE
        off_ref[e] = tiles_before * MOE_ROW_TILE

        def fill(k, c):
            tile_expert_ref[tiles_before + k] = e
            tile_valid_ref[tiles_before + k] = jnp.minimum(count - k * MOE_ROW_TILE, MOE_ROW_TILE)
            return c

        lax.fori_loop(0, n_tiles, fill, 0)
        return tiles_before + n_tiles

    used = lax.fori_loop(0, N_EXPERTS, per_expert, jnp.int32(0))
    info_ref[0] = used
    last_expert = tile_expert_ref[used - 1]

    def fill_tail(k, c):
        tile_expert_ref[k] = last_expert
        tile_valid_ref[k] = 0
        return c

    lax.fori_loop(used, max_tiles, fill_tail, 0)
    tile_expert_ref[max_tiles] = last_expert

    ids = meta_ref[0:2, :]
    pos = meta_ref[2:4, :]
    for e in range(N_EXPERTS):
        pos = pos + jnp.where(ids == e, off_ref[e], 0)
    pos_ref[...] = pos


def _positions(counts, meta, max_tiles):
    n = meta.shape[1]
    smem = pl.BlockSpec(memory_space=pltpu.SMEM)
    return pl.pallas_call(
        functools.partial(_positions_kernel, max_tiles=max_tiles),
        in_specs=[smem, pl.BlockSpec((8, n), lambda: (0, 0))],
        out_specs=[pl.BlockSpec((2, n), lambda: (0, 0)), smem, smem, smem],
        out_shape=[
            jax.ShapeDtypeStruct((2, n), I32),
            jax.ShapeDtypeStruct((max_tiles + 1,), I32),
            jax.ShapeDtypeStruct((1,), I32),
            jax.ShapeDtypeStruct((max_tiles,), I32),
        ],
        scratch_shapes=[pltpu.SMEM((N_EXPERTS,), I32)],
        name="moe_positions",
    )(counts, meta)


def _inverse_kernel(pos_ref, tile_expert_ref, tile_valid_ref, dst_ref, *, n_tokens):
    tm = MOE_ROW_TILE

    def pad_tile(t, c):
        pad_base = 2 * n_tokens + tile_expert_ref[t] * tm

        def pad_row(r, c2):
            dst_ref[t * tm + r] = pad_base + r
            return c2

        lax.fori_loop(tile_valid_ref[t], tm, pad_row, 0)
        return c

    lax.fori_loop(0, dst_ref.shape[0] // tm, pad_tile, 0)

    def token(n, c):
        dst_ref[pos_ref[n]] = n
        dst_ref[pos_ref[n_tokens + n]] = n_tokens + n
        return c

    lax.fori_loop(0, n_tokens, token, 0, unroll=8)


def _inverse_map(pos_flat, tile_expert, tile_valid, n_rows):
    smem = pl.BlockSpec(memory_space=pltpu.SMEM)
    return pl.pallas_call(
        functools.partial(_inverse_kernel, n_tokens=pos_flat.shape[0] // 2),
        in_specs=[smem, smem, smem],
        out_specs=smem,
        out_shape=jax.ShapeDtypeStruct((n_rows,), I32),
        name="moe_inverse_map",
    )(pos_flat, tile_expert, tile_valid)


def _expert_kernel(tile_expert_ref, info_ref, dst_ref, x_hbm, wg_ref, wu_ref, wd_ref,
                   out_hbm, xbuf, ybuf, wg_bf, wu_bf, wd_bf, gsem, ssem, *, n_tokens):
    j = pl.program_id(0)
    used = info_ref[0]
    tm = MOE_ROW_TILE

    def token_rows(ref, token):
        return ref.at[pl.ds(pl.multiple_of(token * SUBLANES, SUBLANES), SUBLANES)]

    def start_gather(tile, s):
        for r in range(tm):
            token = dst_ref[tile * tm + r] & (n_tokens - 1)
            pltpu.make_async_copy(token_rows(x_hbm, token), xbuf.at[s, pl.ds(r * SUBLANES, SUBLANES)],
                                  gsem.at[s]).start()

    def wait_gather(s):
        pltpu.make_async_copy(x_hbm.at[pl.ds(0, tm * SUBLANES)], xbuf.at[s], gsem.at[s]).wait()

    def start_scatter(tile, s):
        for r in range(tm):
            pltpu.make_async_copy(ybuf.at[s, pl.ds(r * SUBLANES, SUBLANES)],
                                  token_rows(out_hbm, dst_ref[tile * tm + r]),
                                  ssem.at[s]).start(priority=1)

    def wait_scatter(s):
        pltpu.make_async_copy(ybuf.at[s], out_hbm.at[pl.ds(0, tm * SUBLANES)], ssem.at[s]).wait()

    def cast_weights():
        wg_bf[...] = wg_ref[...].astype(BF16)
        wu_bf[...] = wu_ref[...].astype(BF16)
        wd_bf[...] = wd_ref[...].astype(BF16)

    def mlp(s):
        x = _load_token_major(xbuf, tm, (s,)).astype(BF16)
        gate = jnp.dot(x, wg_bf[...], preferred_element_type=F32)
        up = jnp.dot(x, wu_bf[...], preferred_element_type=F32)
        hid = (gate * jax.nn.sigmoid(gate) * up).astype(BF16)
        _store_token_major(ybuf, jnp.dot(hid, wd_bf[...], preferred_element_type=F32), (s,))

    @pl.when(j == 0)
    def _():
        def pad_rows_copy(e):
            return pltpu.make_async_copy(
                ybuf.at[1], out_hbm.at[pl.ds((2 * n_tokens + e * tm) * SUBLANES, tm * SUBLANES)], ssem.at[1])

        ybuf[1] = jnp.zeros(ybuf.shape[1:], F32)
        for e in range(N_EXPERTS):
            pad_rows_copy(e).start()
        start_gather(0, 0)
        for e in range(N_EXPERTS):
            pad_rows_copy(e).wait()
        wait_gather(0)
        cast_weights()
        start_gather(jnp.minimum(1, used - 1), 1)
        mlp(0)

    for s in range(2):
        @pl.when((j >= 1) & (j < used) & (lax.rem(j, 2) == s))
        def _():
            wait_gather(s)

            @pl.when(j >= 2)
            def _():
                wait_scatter(s)

            @pl.when(tile_expert_ref[j] != tile_expert_ref[j - 1])
            def _():
                cast_weights()

            start_gather(jnp.minimum(j + 1, used - 1), 1 - s)
            start_scatter(j - 1, 1 - s)
            mlp(s)

        @pl.when((j == used) & (lax.rem(j, 2) == s))
        def _():
            wait_gather(s)

            @pl.when(j >= 2)
            def _():
                wait_scatter(s)

            start_scatter(j - 1, 1 - s)
            wait_scatter(1 - s)


def _experts(tile_expert, info, dst, h3, w_gate, w_up, w_down, layer):
    n = h3.shape[0] // SUBLANES
    assert n & (n - 1) == 0
    tm = MOE_ROW_TILE
    wsel = lambda j, te, info, dst: (layer, te[j], 0, 0)
    return pl.pallas_call(
        functools.partial(_expert_kernel, n_tokens=n),
        grid_spec=pltpu.PrefetchScalarGridSpec(
            num_scalar_prefetch=3,
            grid=(dst.shape[0] // tm + 1,),
            in_specs=[
                pl.BlockSpec(memory_space=pl.ANY),
                pl.BlockSpec((None, None, D_MODEL, D_EXPERT), wsel),
                pl.BlockSpec((None, None, D_MODEL, D_EXPERT), wsel),
                pl.BlockSpec((None, None, D_EXPERT, D_MODEL), wsel),
            ],
            out_specs=pl.BlockSpec(memory_space=pl.ANY),
            scratch_shapes=[
                pltpu.VMEM((2, tm * SUBLANES, 128), F32),
                pltpu.VMEM((2, tm * SUBLANES, 128), F32),
                pltpu.VMEM((D_MODEL, D_EXPERT), BF16),
                pltpu.VMEM((D_MODEL, D_EXPERT), BF16),
                pltpu.VMEM((D_EXPERT, D_MODEL), BF16),
                pltpu.SemaphoreType.DMA((2,)),
                pltpu.SemaphoreType.DMA((2,)),
            ],
        ),
        out_shape=jax.ShapeDtypeStruct(((2 * n + N_EXPERTS * tm) * SUBLANES, 128), F32),
        compiler_params=pltpu.CompilerParams(
            dimension_semantics=("arbitrary",), vmem_limit_bytes=VMEM_LIMIT, has_side_effects=True),
        name="moe_experts",
    )(tile_expert, info, dst, h3, w_gate, w_up, w_down)


def _combine_kernel(x_ref, wts_ref, y1_ref, y2_ref, o_ref):
    tc = x_ref.shape[0]
    eye = jnp.where(lax.broadcasted_iota(I32, (tc, tc), 0) == lax.broadcasted_iota(I32, (tc, tc), 1),
                    1.0, 0.0).astype(BF16)
    w_hi, w_lo = _split_bf16(wts_ref[...])
    w_col = (lax.dot_general(eye, w_hi, _NT, preferred_element_type=F32)
             + lax.dot_general(eye, w_lo, _NT, preferred_element_type=F32))
    o_ref[...] = (x_ref[...] + w_col[:, 0:1] * _load_token_major(y1_ref, tc)
                  + w_col[:, 1:2] * _load_token_major(y2_ref, tc))


def _combine(x2, wts, ys):
    n = x2.shape[0]
    tc = COMBINE_TILE
    return pl.pallas_call(
        _combine_kernel,
        grid=(n // tc,),
        in_specs=[
            pl.BlockSpec((tc, D_MODEL), lambda i: (i, 0)),
            pl.BlockSpec((8, tc), lambda i: (0, i)),
            pl.BlockSpec((tc * SUBLANES, 128), lambda i: (i, 0)),
            pl.BlockSpec((tc * SUBLANES, 128), lambda i: (i + n // tc, 0)),
        ],
        out_specs=pl.BlockSpec((tc, D_MODEL), lambda i: (i, 0)),
        out_shape=jax.ShapeDtypeStruct((n, D_MODEL), F32),
        compiler_params=pltpu.CompilerParams(dimension_semantics=("arbitrary",), vmem_limit_bytes=VMEM_LIMIT),
        name="moe_combine",
    )(x2, wts, ys, ys)


def _hier_moe(x2, g, w_rg, b_rg, w_re, b_re, w_gate, w_up, w_down, layer):
    n = x2.shape[0]
    max_tiles = (2 * n) // MOE_ROW_TILE + N_EXPERTS
    wr = jnp.concatenate([
        jnp.transpose(w_re, (0, 2, 1)).reshape(N_EXPERTS, D_MODEL),
        w_rg.T,
        jnp.zeros((ROUTER_ROWS - N_EXPERTS - N_GROUPS, D_MODEL), F32)], axis=0)
    br = jnp.concatenate([
        b_re.reshape(N_EXPERTS), b_rg, jnp.zeros((ROUTER_ROWS - N_EXPERTS - N_GROUPS,), F32)])[:, None]

    meta, wts, counts, h3 = _route(x2, g, wr, br)
    pos, tile_expert, info, tile_valid = _positions(counts, meta, max_tiles)
    dst = _inverse_map(pos.reshape(2 * n), tile_expert, tile_valid, max_tiles * MOE_ROW_TILE)
    ys = _experts(tile_expert, info, dst, h3, w_gate, w_up, w_down, layer)
    return _combine(x2, wts, ys)


def _rope_tables(seq):
    half = ROT_DIM // 2
    offset = jnp.arange(HEAD_W) % HEAD_DIM
    inv_freq = ROPE_THETA ** (-(2 * (offset % half)).astype(F32) / ROT_DIM)
    lane_freq = jnp.where(offset < ROT_DIM, inv_freq, 0.0)
    ang = jnp.arange(seq, dtype=F32)[:, None] * lane_freq[None, :]
    cos, sin = jnp.cos(ang), jnp.sin(ang)
    sinp = jnp.where((offset >= half) & (offset < ROT_DIM), sin, 0.0)
    sinm = jnp.where(offset < half, -sin, 0.0)
    return cos, sinp, sinm


def kernel(x, attn_norm, diff_w_in, diff_q_gain, diff_k_gain, diff_lambda_q1, diff_lambda_k1,
           diff_lambda_q2, diff_lambda_k2, diff_subln, diff_w_out,
           conv_norm, conv_w_in, conv_b_in, conv_dw_w, conv_dw_b, conv_ln_g, conv_ln_b,
           conv_w_out, conv_b_out,
           ffn_norm, router_group_w, router_group_b, router_expert_w, router_expert_b,
           moe_w_gate, moe_w_up, moe_w_down):
    batch, seq, d = x.shape
    assert d == D_MODEL and seq % ATTN_TILE == 0 and seq % CONV_TILE == 0
    n = batch * seq
    depth = ffn_norm.shape[0]
    cos_t, sinp_t, sinm_t = _rope_tables(seq)
    row = lambda v: v.reshape(1, -1)
    x2 = x.reshape(n, d)
    for i in range(depth):
        j = i // 2
        if i % 2 == 0:
            lambda_init = 0.8 - 0.6 * math.exp(-0.3 * i)
            dup = lambda v: jnp.concatenate([v, v]).reshape(1, HEAD_W)
            q2, k2, vt4 = _qkv_project(x2, row(attn_norm[j]), diff_w_in[j].astype(BF16),
                                       dup(diff_q_gain[j]), dup(diff_k_gain[j]), cos_t, sinp_t, sinm_t,
                                       batch, seq)
            lam_params = jnp.stack([diff_lambda_q1[j], diff_lambda_k1[j], diff_lambda_q2[j], diff_lambda_k2[j]])
            score_bound = (SCORE_BOUND_UNIT * jnp.max(jnp.abs(diff_q_gain[j]))
                           * jnp.max(jnp.abs(diff_k_gain[j]))).reshape(1)
            o3 = _diff_attention(score_bound, q2.reshape(batch, seq, d), k2.reshape(batch, seq, d), vt4,
                                 lam_params, row(diff_subln[j]), lambda_init)
            x2 = _proj_residual(x2, o3.reshape(n, d), diff_w_out[j].astype(BF16))
        else:
            dw_w = jnp.concatenate([conv_dw_w[j], jnp.zeros((CONV_HALO - CONV_WIDTH, d), F32)], axis=0)
            x2 = _conv_block(x2.reshape(batch, seq, d), row(conv_norm[j]), conv_w_in[j].astype(BF16),
                             row(conv_b_in[j]), dw_w, row(conv_dw_b[j]), row(conv_ln_g[j]), row(conv_ln_b[j]),
                             conv_w_out[j].astype(BF16), row(conv_b_out[j])).reshape(n, d)
        x2 = _hier_moe(x2, row(ffn_norm[i]), router_group_w[i], router_group_b[i],
                       router_expert_w[i], router_expert_b[i], moe_w_gate, moe_w_up, moe_w_down, i)
    return x2.reshape(batch, seq, d)
```

```python
import functools
import math

import jax
import jax.numpy as jnp
from jax import lax
from jax.experimental import pallas as pl
from jax.experimental.pallas import tpu as pltpu

F32 = jnp.float32
BF16 = jnp.bfloat16
I32 = jnp.int32

D_MODEL = 1024
HEAD_DIM = 64
HEAD_W = 2 * HEAD_DIM
N_HEADS = D_MODEL // HEAD_W
ROT_DIM = HEAD_DIM // 4
ROPE_THETA = 500000.0
CHUNK = 64
CONV_WIDTH = 31
SUBLANES = 8
CONV_HALO = 32
N_GROUPS = 4
EXPERTS_PER_GROUP = 8
N_EXPERTS = N_GROUPS * EXPERTS_PER_GROUP
D_EXPERT = D_MODEL // 2
EPS = 1e-6
ROUTER_ROWS = 64
LOG2_E = math.log2(math.e)
Q_SCALE = HEAD_DIM ** -0.5 * LOG2_E
SCORE_BOUND_UNIT = 1.02 * HEAD_DIM * Q_SCALE
UNSHIFTED_SOFTMAX_LIMIT = 50.0

ATTN_TILE = 1024
ATTN_GROUPS = (2, 1)
ROW_TILE = 512
CONV_TILE = 512
ROUTER_TILE = 512
MOE_ROW_TILE = 256
COMBINE_TILE = 256
VMEM_LIMIT = 48 * 1024 * 1024

_NT = (((1,), (1,)), ((), ()))


def _rmsnorm_rows(x, g):
    return x * lax.rsqrt(jnp.mean(x * x, axis=-1, keepdims=True) + EPS) * g


def _qkv_kernel(x_ref, g_ref, w_ref, qg_ref, kg_ref, cos_ref, sinp_ref, sinm_ref,
                q_ref, k_ref, vt_ref, *, col_tile):
    tm = x_ref.shape[0]
    h = _rmsnorm_rows(x_ref[...], g_ref[...]).astype(BF16)
    lane = lax.broadcasted_iota(I32, (tm, HEAD_W), 1)
    lo = lane < HEAD_DIM
    cosv, sinp, sinm = cos_ref[...], sinp_ref[...], sinm_ref[...]

    def norm_rope(y, gain, scale):
        y2 = y * y
        s_lo = jnp.sum(jnp.where(lo, y2, 0.0), axis=-1, keepdims=True)
        s_hi = jnp.sum(y2, axis=-1, keepdims=True) - s_lo
        r = jnp.where(lo, lax.rsqrt(s_lo * (1.0 / HEAD_DIM) + EPS), lax.rsqrt(s_hi * (1.0 / HEAD_DIM) + EPS))
        yn = y * r * gain
        half = ROT_DIM // 2
        out = yn * cosv + pltpu.roll(yn, half, 1) * sinp + pltpu.roll(yn, HEAD_W - half, 1) * sinm
        return out * scale

    for c in range(3 * D_MODEL // col_tile):
        y = jnp.dot(h, w_ref[:, c * col_tile:(c + 1) * col_tile], preferred_element_type=F32)
        for s in range(col_tile // HEAD_W):
            col = c * col_tile + s * HEAD_W
            ys = y[:, s * HEAD_W:(s + 1) * HEAD_W]
            if col < D_MODEL:
                q_ref[:, col:col + HEAD_W] = norm_rope(ys, qg_ref[...], Q_SCALE).astype(BF16)
            elif col < 2 * D_MODEL:
                k_ref[:, col - D_MODEL:col - D_MODEL + HEAD_W] = norm_rope(ys, kg_ref[...], 1.0).astype(BF16)
            else:
                vt_ref[col - 2 * D_MODEL:col - 2 * D_MODEL + HEAD_W, :] = ys.T.astype(BF16)


def _qkv_project(x2, g, w_bf, q_gain, k_gain, cos_t, sinp_t, sinm_t, batch, seq):
    n = x2.shape[0]
    tm = ATTN_TILE
    tiles_per_seq = seq // tm
    row = lambda i: (i, 0)
    fixed = lambda i: (0, 0)
    tab = lambda i: (i % tiles_per_seq, 0)
    return pl.pallas_call(
        functools.partial(_qkv_kernel, col_tile=512),
        grid=(n // tm,),
        in_specs=[
            pl.BlockSpec((tm, D_MODEL), row),
            pl.BlockSpec((1, D_MODEL), fixed),
            pl.BlockSpec((D_MODEL, 3 * D_MODEL), fixed),
            pl.BlockSpec((1, HEAD_W), fixed),
            pl.BlockSpec((1, HEAD_W), fixed),
            pl.BlockSpec((tm, HEAD_W), tab),
            pl.BlockSpec((tm, HEAD_W), tab),
            pl.BlockSpec((tm, HEAD_W), tab),
        ],
        out_specs=[
            pl.BlockSpec((tm, D_MODEL), row),
            pl.BlockSpec((tm, D_MODEL), row),
            pl.BlockSpec((None, None, D_MODEL, tm), lambda i: (i // tiles_per_seq, i % tiles_per_seq, 0, 0)),
        ],
        out_shape=[
            jax.ShapeDtypeStruct((n, D_MODEL), BF16),
            jax.ShapeDtypeStruct((n, D_MODEL), BF16),
            jax.ShapeDtypeStruct((batch, tiles_per_seq, D_MODEL, tm), BF16),
        ],
        compiler_params=pltpu.CompilerParams(dimension_semantics=("arbitrary",), vmem_limit_bytes=VMEM_LIMIT),
        name="qkv_project",
    )(x2, g, w_bf, q_gain, k_gain, cos_t, sinp_t, sinm_t)


def _attn_kernel(bound_ref, lam_ref, q_ref, k_ref, vt_ref, g_ref, o_ref, m_ref, l_ref, acc_ref, *, lambda_init):
    t = q_ref.shape[0]
    i = pl.program_id(2)
    q = q_ref[...]
    lane = lax.broadcasted_iota(I32, (t, HEAD_W), 1)
    zero = jnp.zeros_like(q)
    q_comp = (jnp.where(lane < HEAD_DIM, q, zero), jnp.where(lane >= HEAD_DIM, q, zero))

    l_ref[...] = jnp.zeros(l_ref.shape, F32)
    acc_ref[...] = jnp.zeros(acc_ref.shape, F32)

    def tiles(j):
        return k_ref[pl.ds(pl.multiple_of(j * t, t), t), :], vt_ref[j]

    def chunk_mask(n_keys=t, n_queries=t):
        key_chunk = lax.broadcasted_iota(I32, (n_keys, n_queries), 0) // CHUNK
        qry_chunk = lax.broadcasted_iota(I32, (n_keys, n_queries), 1) // CHUNK
        return key_chunk <= qry_chunk

    def unshifted_block(kt, vt, c, q0, masked):
        n_keys = kt.shape[0]
        s = lax.dot_general(kt, q_comp[c][q0:], _NT, preferred_element_type=F32)
        p = jnp.exp2(s)
        if masked:
            p = jnp.where(chunk_mask(n_keys, t - q0), p, 0.0)
        l_ref[c, :, q0:] += jnp.sum(p, axis=0, keepdims=True)
        acc_ref[c, :, q0:] += jnp.dot(vt, p.astype(BF16), preferred_element_type=F32)

    def unshifted_step(j, masked):
        kt, vt = tiles(j)
        half = t // 2
        for c in range(2):
            if masked:
                unshifted_block(kt[:half], vt[:, :half], c, 0, True)
                unshifted_block(kt[half:], vt[:, half:], c, half, True)
            else:
                unshifted_block(kt, vt, c, 0, False)

    def online_step(j, masked):
        kt, vt = tiles(j)
        for c in range(2):
            s = lax.dot_general(kt, q_comp[c], _NT, preferred_element_type=F32)
            if masked:
                s = jnp.where(chunk_mask(), s, -jnp.inf)
            m_old = m_ref[c]
            m_new = jnp.maximum(m_old, jnp.max(s, axis=0, keepdims=True))
            alpha = jnp.exp2(m_old - m_new)
            p = jnp.exp2(s - m_new)
            l_ref[c] = alpha * l_ref[c] + jnp.sum(p, axis=0, keepdims=True)
            acc_ref[c] = alpha * acc_ref[c] + jnp.dot(vt, p.astype(BF16), preferred_element_type=F32)
            m_ref[c] = m_new

    def sweep(step, groups):
        done = 0
        for group in groups:
            def trip(jj, carry, group=group, done=done):
                for u in range(group):
                    step(done + jj * group + u, False)
                return carry

            n_trips = (i - done) // group
            lax.fori_loop(0, n_trips, trip, 0)
            done = done + n_trips * group
        step(i, True)

    unshifted_ok = bound_ref[0] <= UNSHIFTED_SOFTMAX_LIMIT

    @pl.when(unshifted_ok)
    def _():
        sweep(unshifted_step, ATTN_GROUPS)

    @pl.when(jnp.logical_not(unshifted_ok))
    def _():
        m_ref[...] = jnp.full(m_ref.shape, -jnp.inf, F32)
        sweep(online_step, (1,))

    lam_p = lam_ref[...]
    lam = (jnp.exp(jnp.sum(lam_p[0:1] * lam_p[1:2], axis=-1, keepdims=True))
           - jnp.exp(jnp.sum(lam_p[2:3] * lam_p[3:4], axis=-1, keepdims=True)) + lambda_init)
    ot = acc_ref[0] * (1.0 / l_ref[0]) - lam * (acc_ref[1] * (1.0 / l_ref[1]))
    ot = ot * lax.rsqrt(jnp.mean(ot * ot, axis=0, keepdims=True) + EPS)
    o_ref[...] = (ot.T * g_ref[...] * (1.0 - lambda_init)).astype(o_ref.dtype)


def _diff_attention(score_bound, q3, k3, vt4, lam_params, subln_g, lambda_init):
    batch, seq, _ = q3.shape
    t = ATTN_TILE
    nq = seq // t
    return pl.pallas_call(
        functools.partial(_attn_kernel, lambda_init=lambda_init),
        grid_spec=pltpu.PrefetchScalarGridSpec(
            num_scalar_prefetch=1,
            grid=(batch, N_HEADS, nq),
            in_specs=[
                pl.BlockSpec((4, HEAD_DIM), lambda b, h, i, bound: (0, 0)),
                pl.BlockSpec((None, t, HEAD_W), lambda b, h, i, bound: (b, i, h)),
                pl.BlockSpec((None, seq, HEAD_W), lambda b, h, i, bound: (b, 0, h)),
                pl.BlockSpec((None, nq, HEAD_W, t), lambda b, h, i, bound: (b, 0, h, 0)),
                pl.BlockSpec((1, HEAD_W), lambda b, h, i, bound: (0, 0)),
            ],
            out_specs=pl.BlockSpec((None, t, HEAD_W), lambda b, h, i, bound: (b, i, h)),
            scratch_shapes=[
                pltpu.VMEM((2, 1, t), F32),
                pltpu.VMEM((2, 1, t), F32),
                pltpu.VMEM((2, HEAD_W, t), F32),
            ],
        ),
        out_shape=jax.ShapeDtypeStruct((batch, seq, D_MODEL), BF16),
        compiler_params=pltpu.CompilerParams(
            dimension_semantics=("arbitrary", "arbitrary", "arbitrary"), vmem_limit_bytes=VMEM_LIMIT),
        name="diff_attention",
    )(score_bound, lam_params, q3, k3, vt4, subln_g)


def _proj_residual_kernel(x_ref, y_ref, w_ref, o_ref):
    o_ref[...] = x_ref[...] + jnp.dot(y_ref[...], w_ref[...], preferred_element_type=F32)


def _proj_residual(x2, y2, w_bf):
    n = x2.shape[0]
    tm = ROW_TILE
    return pl.pallas_call(
        _proj_residual_kernel,
        grid=(n // tm,),
        in_specs=[
            pl.BlockSpec((tm, D_MODEL), lambda i: (i, 0)),
            pl.BlockSpec((tm, D_MODEL), lambda i: (i, 0)),
            pl.BlockSpec((D_MODEL, D_MODEL), lambda i: (0, 0)),
        ],
        out_specs=pl.BlockSpec((tm, D_MODEL), lambda i: (i, 0)),
        out_shape=jax.ShapeDtypeStruct((n, D_MODEL), F32),
        compiler_params=pltpu.CompilerParams(dimension_semantics=("arbitrary",), vmem_limit_bytes=VMEM_LIMIT),
        name="attn_out_proj",
    )(x2, y2, w_bf)


def _conv_kernel(x_ref, g_ref, win_ref, bin_ref, dww_ref, dwb_ref, lng_ref, lnb_ref, wout_ref, bout_ref,
                 o_ref, u_ref):
    tm = x_ref.shape[0]
    x = x_ref[...]
    h = _rmsnorm_rows(x, g_ref[...]).astype(BF16)
    a = jnp.dot(h, win_ref[:, :D_MODEL], preferred_element_type=F32) + bin_ref[:, :D_MODEL]
    gate = jnp.dot(h, win_ref[:, D_MODEL:], preferred_element_type=F32) + bin_ref[:, D_MODEL:]
    u = a * jax.nn.sigmoid(gate)

    @pl.when(pl.program_id(1) == 0)
    def _():
        u_ref[0:CONV_HALO, :] = jnp.zeros((CONV_HALO, D_MODEL), F32)

    u_ref[CONV_HALO:CONV_HALO + tm, :] = u
    base = CONV_HALO - (CONV_WIDTH - 1)
    window = u_ref[...]
    n_rows = tm + CONV_HALO
    acc = jnp.zeros((tm, D_MODEL), F32) + dwb_ref[...]
    for phase in range(SUBLANES):
        rolled = window if phase == 0 else pltpu.roll(window, n_rows - phase, 0)
        for row0 in range(0, CONV_HALO + 1, SUBLANES):
            w = row0 + phase - base
            if 0 <= w < CONV_WIDTH:
                acc = acc + dww_ref[w:w + 1, :] * rolled[row0:row0 + tm, :]
    u_ref[0:CONV_HALO, :] = u_ref[tm:tm + CONV_HALO, :]

    mu = jnp.mean(acc, axis=-1, keepdims=True)
    xc = acc - mu
    var = jnp.mean(xc * xc, axis=-1, keepdims=True)
    v = xc * lax.rsqrt(var + EPS) * lng_ref[...] + lnb_ref[...]
    v = (v * jax.nn.sigmoid(v)).astype(BF16)
    o_ref[...] = x + jnp.dot(v, wout_ref[...], preferred_element_type=F32) + bout_ref[...]


def _conv_block(x3, g, win_bf, b_in, dw_w, dw_b, ln_g, ln_b, wout_bf, b_out):
    batch, seq, _ = x3.shape
    tm = CONV_TILE
    row = lambda b, s: (b, s, 0)
    fixed = lambda b, s: (0, 0)
    vec = pl.BlockSpec((1, D_MODEL), fixed)
    return pl.pallas_call(
        _conv_kernel,
        grid=(batch, seq // tm),
        in_specs=[
            pl.BlockSpec((None, tm, D_MODEL), row),
            vec,
            pl.BlockSpec((D_MODEL, 2 * D_MODEL), fixed),
            pl.BlockSpec((1, 2 * D_MODEL), fixed),
            pl.BlockSpec((CONV_HALO, D_MODEL), fixed),
            vec, vec, vec,
            pl.BlockSpec((D_MODEL, D_MODEL), fixed),
            vec,
        ],
        out_specs=pl.BlockSpec((None, tm, D_MODEL), row),
        out_shape=jax.ShapeDtypeStruct((batch, seq, D_MODEL), F32),
        scratch_shapes=[pltpu.VMEM((tm + CONV_HALO, D_MODEL), F32)],
        compiler_params=pltpu.CompilerParams(
            dimension_semantics=("arbitrary", "arbitrary"), vmem_limit_bytes=VMEM_LIMIT),
        name="conformer_conv",
    )(x3, g, win_bf, b_in, dw_w, dw_b, ln_g, ln_b, wout_bf, b_out)


def _store_token_major(ref, v, lead=()):
    t = v.shape[0]
    for s in range(SUBLANES):
        ref[lead + (pl.ds(s, t, stride=SUBLANES), slice(None))] = v[:, s * 128:(s + 1) * 128]


def _load_token_major(ref, t, lead=()):
    return jnp.concatenate(
        [ref[lead + (pl.ds(s, t, stride=SUBLANES), slice(None))] for s in range(SUBLANES)], axis=1)


def _split_bf16(v):
    hi = v.astype(BF16)
    lo = (v - hi.astype(F32)).astype(BF16)
    return hi, lo


def _router_kernel(x_ref, g_ref, wr_ref, br_ref, meta_ref, wts_ref, cnt_ref, h3_ref, carry_ref, tri_ref):
    t = x_ref.shape[0]

    @pl.when(pl.program_id(0) == 0)
    def _():
        carry_ref[...] = jnp.zeros(carry_ref.shape, F32)
        before = lax.broadcasted_iota(I32, (t, t), 0) < lax.broadcasted_iota(I32, (t, t), 1)
        tri_ref[...] = jnp.where(before, 1.0, 0.0).astype(BF16)

    h = _rmsnorm_rows(x_ref[...], g_ref[...])
    _store_token_major(h3_ref, h)
    h_hi, h_lo = _split_bf16(h)
    w_hi, w_lo = _split_bf16(wr_ref[...])
    logits = (lax.dot_general(w_hi, h_hi, _NT, preferred_element_type=F32)
              + lax.dot_general(w_hi, h_lo, _NT, preferred_element_type=F32)
              + lax.dot_general(w_lo, h_hi, _NT, preferred_element_type=F32)) + br_ref[...]

    gl = logits[N_EXPERTS:N_EXPERTS + N_GROUPS]
    gmax = jnp.max(gl, axis=0, keepdims=True)
    g_p = 1.0 / jnp.sum(jnp.exp(gl - gmax), axis=0, keepdims=True)
    row_g = lax.broadcasted_iota(I32, gl.shape, 0)
    g_idx = jnp.min(jnp.where(gl == gmax, row_g, N_GROUPS), axis=0, keepdims=True)

    e_sel = jnp.zeros((EXPERTS_PER_GROUP, t), F32)
    for grp in range(N_GROUPS):
        e_sel = e_sel + jnp.where(g_idx == grp, logits[grp * EXPERTS_PER_GROUP:(grp + 1) * EXPERTS_PER_GROUP], 0.0)
    row_e = lax.broadcasted_iota(I32, e_sel.shape, 0)
    m1 = jnp.max(e_sel, axis=0, keepdims=True)
    i1 = jnp.min(jnp.where(e_sel == m1, row_e, EXPERTS_PER_GROUP), axis=0, keepdims=True)
    rest = jnp.where(row_e == i1, -jnp.inf, e_sel)
    m2 = jnp.max(rest, axis=0, keepdims=True)
    i2 = jnp.min(jnp.where(rest == m2, row_e, EXPERTS_PER_GROUP), axis=0, keepdims=True)
    r = jnp.exp(m2 - m1)
    p1 = 1.0 / (1.0 + r)
    f1 = g_idx * EXPERTS_PER_GROUP + i1
    f2 = g_idx * EXPERTS_PER_GROUP + i2

    row_x = lax.broadcasted_iota(I32, (N_EXPERTS, t), 0)
    oh1 = row_x == f1
    oh2 = row_x == f2
    onehot = jnp.where(oh1 | oh2, 1.0, 0.0)
    prior = jnp.dot(onehot.astype(BF16), tri_ref[...], preferred_element_type=F32) + carry_ref[:, 0:1]
    rank1 = jnp.sum(jnp.where(oh1, prior, 0.0), axis=0, keepdims=True)
    rank2 = jnp.sum(jnp.where(oh2, prior, 0.0), axis=0, keepdims=True)
    total = carry_ref[...] + jnp.sum(onehot, axis=1, keepdims=True)
    carry_ref[...] = total
    cnt_ref[...] = total.astype(I32)

    zi = jnp.zeros((4, t), I32)
    meta_ref[...] = jnp.concatenate([f1, f2, rank1.astype(I32), rank2.astype(I32), zi], axis=0)
    wts_ref[...] = jnp.concatenate([g_p * p1, g_p * (r * p1), jnp.zeros((6, t), F32)], axis=0)


def _route(x2, g, wr, br):
    n = x2.shape[0]
    t = ROUTER_TILE
    return pl.pallas_call(
        _router_kernel,
        grid=(n // t,),
        in_specs=[
            pl.BlockSpec((t, D_MODEL), lambda i: (i, 0)),
            pl.BlockSpec((1, D_MODEL), lambda i: (0, 0)),
            pl.BlockSpec((ROUTER_ROWS, D_MODEL), lambda i: (0, 0)),
            pl.BlockSpec((ROUTER_ROWS, 1), lambda i: (0, 0)),
        ],
        out_specs=[
            pl.BlockSpec((8, t), lambda i: (0, i)),
            pl.BlockSpec((8, t), lambda i: (0, i)),
            pl.BlockSpec((N_EXPERTS, 128), lambda i: (0, 0)),
            pl.BlockSpec((t * SUBLANES, 128), lambda i: (i, 0)),
        ],
        out_shape=[
            jax.ShapeDtypeStruct((8, n), I32),
            jax.ShapeDtypeStruct((8, n), F32),
            jax.ShapeDtypeStruct((N_EXPERTS, 128), I32),
            jax.ShapeDtypeStruct((n * SUBLANES, 128), F32),
        ],
        scratch_shapes=[pltpu.VMEM((N_EXPERTS, 128), F32), pltpu.VMEM((t, t), BF16)],
        compiler_params=pltpu.CompilerParams(dimension_semantics=("arbitrary",), vmem_limit_bytes=VMEM_LIMIT),
        name="moe_router",
    )(x2, g, wr, br)


def _positions_kernel(cnt_ref, meta_ref, pos_ref, tile_expert_ref, info_ref, tile_valid_ref, off_ref,
                      *, max_tiles):
    def per_expert(e, tiles_before):
        count = cnt_ref[e, 0]
        n_tiles = (count + (MOE_ROW_TILE - 1)) // MOE_ROW_TILE
        off_ref[e] = tiles_before * MOE_ROW_TILE

        def fill(k, c):
            tile_expert_ref[tiles_before + k] = e
            tile_valid_ref[tiles_before + k] = jnp.minimum(count - k * MOE_ROW_TILE, MOE_ROW_TILE)
            return c

        lax.fori_loop(0, n_tiles, fill, 0)
        return tiles_before + n_tiles

    used = lax.fori_loop(0, N_EXPERTS, per_expert, jnp.int32(0))
    info_ref[0] = used
    last_expert = tile_expert_ref[used - 1]

    def fill_tail(k, c):
        tile_expert_ref[k] = last_expert
        tile_valid_ref[k] = 0
        return c

    lax.fori_loop(used, max_tiles, fill_tail, 0)
    tile_expert_ref[max_tiles] = last_expert

    ids = meta_ref[0:2, :]
    pos = meta_ref[2:4, :]
    for e in range(N_EXPERTS):
        pos = pos + jnp.where(ids == e, off_ref[e], 0)
    pos_ref[...] = pos


def _positions(counts, meta, max_tiles):
    n = meta.shape[1]
    smem = pl.BlockSpec(memory_space=pltpu.SMEM)
    return pl.pallas_call(
        functools.partial(_positions_kernel, max_tiles=max_tiles),
        in_specs=[smem, pl.BlockSpec((8, n), lambda: (0, 0))],
        out_specs=[pl.BlockSpec((2, n), lambda: (0, 0)), smem, smem, smem],
        out_shape=[
            jax.ShapeDtypeStruct((2, n), I32),
            jax.ShapeDtypeStruct((max_tiles + 1,), I32),
            jax.ShapeDtypeStruct((1,), I32),
            jax.ShapeDtypeStruct((max_tiles,), I32),
        ],
        scratch_shapes=[pltpu.SMEM((N_EXPERTS,), I32)],
        name="moe_positions",
    )(counts, meta)


def _inverse_kernel(pos_ref, tile_expert_ref, tile_valid_ref, dst_ref, *, n_tokens):
    tm = MOE_ROW_TILE

    def pad_tile(t, c):
        pad_base = 2 * n_tokens + tile_expert_ref[t] * tm

        def pad_row(r, c2):
            dst_ref[t * tm + r] = pad_base + r
            return c2

        lax.fori_loop(tile_valid_ref[t], tm, pad_row, 0)
        return c

    lax.fori_loop(0, dst_ref.shape[0] // tm, pad_tile, 0)

    def token(n, c):
        dst_ref[pos_ref[n]] = n
        dst_ref[pos_ref[n_tokens + n]] = n_tokens + n
        return c

    lax.fori_loop(0, n_tokens, token, 0, unroll=8)


def _inverse_map(pos_flat, tile_expert, tile_valid, n_rows):
    smem = pl.BlockSpec(memory_space=pltpu.SMEM)
    return pl.pallas_call(
        functools.partial(_inverse_kernel, n_tokens=pos_flat.shape[0] // 2),
        in_specs=[smem, smem, smem],
        out_specs=smem,
        out_shape=jax.ShapeDtypeStruct((n_rows,), I32),
        name="moe_inverse_map",
    )(pos_flat, tile_expert, tile_valid)


def _expert_kernel(tile_expert_ref, info_ref, dst_ref, x_hbm, wg_ref, wu_ref, wd_ref,
                   out_hbm, xbuf, ybuf, wg_bf, wu_bf, wd_bf, gsem, ssem, *, n_tokens):
    j = pl.program_id(0)
    used = info_ref[0]
    tm = MOE_ROW_TILE

    def token_rows(ref, token):
        return ref.at[pl.ds(pl.multiple_of(token * SUBLANES, SUBLANES), SUBLANES)]


    def start_gather(tile, s):
        for r in range(tm):
            token = dst_ref[tile * tm + r] & (n_tokens - 1)
            pltpu.make_async_copy(token_rows(x_hbm, token), xbuf.at[s, pl.ds(r * SUBLANES, SUBLANES)],
                                  gsem.at[s]).start(priority=r % 2)

    def wait_gather(s):
        pltpu.make_async_copy(x_hbm.at[pl.ds(0, tm * SUBLANES)], xbuf.at[s], gsem.at[s]).wait()

    def start_scatter(tile, s):
        for r in range(tm):
            pltpu.make_async_copy(ybuf.at[s, pl.ds(r * SUBLANES, SUBLANES)],
                                  token_rows(out_hbm, dst_ref[tile * tm + r]),
                                  ssem.at[s]).start(priority=(r + 1) % 2)

    def wait_scatter(s):
        pltpu.make_async_copy(ybuf.at[s], out_hbm.at[pl.ds(0, tm * SUBLANES)], ssem.at[s]).wait()

    def cast_weights():
        wg_bf[...] = wg_ref[...].astype(BF16)
        wu_bf[...] = wu_ref[...].astype(BF16)
        wd_bf[...] = wd_ref[...].astype(BF16)

    def mlp(s):
        x = _load_token_major(xbuf, tm, (s,)).astype(BF16)
        gate = jnp.dot(x, wg_bf[...], preferred_element_type=F32)
        up = jnp.dot(x, wu_bf[...], preferred_element_type=F32)
        hid = (gate * jax.nn.sigmoid(gate) * up).astype(BF16)
        _store_token_major(ybuf, jnp.dot(hid, wd_bf[...], preferred_element_type=F32), (s,))

    @pl.when(j == 0)
    def _():
        def pad_rows_copy(e):
            return pltpu.make_async_copy(
                ybuf.at[1], out_hbm.at[pl.ds((2 * n_tokens + e * tm) * SUBLANES, tm * SUBLANES)], ssem.at[1])

        ybuf[1] = jnp.zeros(ybuf.shape[1:], F32)
        for e in range(N_EXPERTS):
            pad_rows_copy(e).start()
        start_gather(0, 0)
        for e in range(N_EXPERTS):
            pad_rows_copy(e).wait()
        wait_gather(0)
        cast_weights()
        start_gather(jnp.minimum(1, used - 1), 1)
        mlp(0)

    for s in range(2):
        @pl.when((j >= 1) & (j < used) & (lax.rem(j, 2) == s))
        def _():
            wait_gather(s)

            @pl.when(j >= 2)
            def _():
                wait_scatter(s)

            @pl.when(tile_expert_ref[j] != tile_expert_ref[j - 1])
            def _():
                cast_weights()

            start_gather(jnp.minimum(j + 1, used - 1), 1 - s)
            start_scatter(j - 1, 1 - s)
            mlp(s)

        @pl.when((j == used) & (lax.rem(j, 2) == s))
        def _():
            wait_gather(s)

            @pl.when(j >= 2)
            def _():
                wait_scatter(s)

            start_scatter(j - 1, 1 - s)
            wait_scatter(1 - s)


def _experts(tile_expert, info, dst, h3, w_gate, w_up, w_down, layer):
    n = h3.shape[0] // SUBLANES
    assert n & (n - 1) == 0
    tm = MOE_ROW_TILE
    wsel = lambda j, te, info, dst: (layer, te[j], 0, 0)
    return pl.pallas_call(
        functools.partial(_expert_kernel, n_tokens=n),
        grid_spec=pltpu.PrefetchScalarGridSpec(
            num_scalar_prefetch=3,
            grid=(dst.shape[0] // tm + 1,),
            in_specs=[
                pl.BlockSpec(memory_space=pl.ANY),
                pl.BlockSpec((None, None, D_MODEL, D_EXPERT), wsel),
                pl.BlockSpec((None, None, D_MODEL, D_EXPERT), wsel),
                pl.BlockSpec((None, None, D_EXPERT, D_MODEL), wsel),
            ],
            out_specs=pl.BlockSpec(memory_space=pl.ANY),
            scratch_shapes=[
                pltpu.VMEM((2, tm * SUBLANES, 128), F32),
                pltpu.VMEM((2, tm * SUBLANES, 128), F32),
                pltpu.VMEM((D_MODEL, D_EXPERT), BF16),
                pltpu.VMEM((D_MODEL, D_EXPERT), BF16),
                pltpu.VMEM((D_EXPERT, D_MODEL), BF16),
                pltpu.SemaphoreType.DMA((2,)),
                pltpu.SemaphoreType.DMA((2,)),
            ],
        ),
        out_shape=jax.ShapeDtypeStruct(((2 * n + N_EXPERTS * tm) * SUBLANES, 128), F32),
        compiler_params=pltpu.CompilerParams(
            dimension_semantics=("arbitrary",), vmem_limit_bytes=VMEM_LIMIT, has_side_effects=True),
        name="moe_experts",
    )(tile_expert, info, dst, h3, w_gate, w_up, w_down)


def _combine_kernel(x_ref, wts_ref, y1_ref, y2_ref, o_ref):
    tc = x_ref.shape[0]
    eye = jnp.where(lax.broadcasted_iota(I32, (tc, tc), 0) == lax.broadcasted_iota(I32, (tc, tc), 1),
                    1.0, 0.0).astype(BF16)
    w_hi, w_lo = _split_bf16(wts_ref[...])
    w_col = (lax.dot_general(eye, w_hi, _NT, preferred_element_type=F32)
             + lax.dot_general(eye, w_lo, _NT, preferred_element_type=F32))
    o_ref[...] = (x_ref[...] + w_col[:, 0:1] * _load_token_major(y1_ref, tc)
                  + w_col[:, 1:2] * _load_token_major(y2_ref, tc))


def _combine(x2, wts, ys):
    n = x2.shape[0]
    tc = COMBINE_TILE
    return pl.pallas_call(
        _combine_kernel,
        grid=(n // tc,),
        in_specs=[
            pl.BlockSpec((tc, D_MODEL), lambda i: (i, 0)),
            pl.BlockSpec((8, tc), lambda i: (0, i)),
            pl.BlockSpec((tc * SUBLANES, 128), lambda i: (i, 0)),
            pl.BlockSpec((tc * SUBLANES, 128), lambda i: (i + n // tc, 0)),
        ],
        out_specs=pl.BlockSpec((tc, D_MODEL), lambda i: (i, 0)),
        out_shape=jax.ShapeDtypeStruct((n, D_MODEL), F32),
        compiler_params=pltpu.CompilerParams(dimension_semantics=("arbitrary",), vmem_limit_bytes=VMEM_LIMIT),
        name="moe_combine",
    )(x2, wts, ys, ys)


def _hier_moe(x2, g, w_rg, b_rg, w_re, b_re, w_gate, w_up, w_down, layer):
    n = x2.shape[0]
    max_tiles = (2 * n) // MOE_ROW_TILE + N_EXPERTS
    wr = jnp.concatenate([
        jnp.transpose(w_re, (0, 2, 1)).reshape(N_EXPERTS, D_MODEL),
        w_rg.T,
        jnp.zeros((ROUTER_ROWS - N_EXPERTS - N_GROUPS, D_MODEL), F32)], axis=0)
    br = jnp.concatenate([
        b_re.reshape(N_EXPERTS), b_rg, jnp.zeros((ROUTER_ROWS - N_EXPERTS - N_GROUPS,), F32)])[:, None]

    meta, wts, counts, h3 = _route(x2, g, wr, br)
    pos, tile_expert, info, tile_valid = _positions(counts, meta, max_tiles)
    dst = _inverse_map(pos.reshape(2 * n), tile_expert, tile_valid, max_tiles * MOE_ROW_TILE)
    ys = _experts(tile_expert, info, dst, h3, w_gate, w_up, w_down, layer)
    return _combine(x2, wts, ys)


def _rope_tables(seq):
    half = ROT_DIM // 2
    offset = jnp.arange(HEAD_W) % HEAD_DIM
    inv_freq = ROPE_THETA ** (-(2 * (offset % half)).astype(F32) / ROT_DIM)
    lane_freq = jnp.where(offset < ROT_DIM, inv_freq, 0.0)
    ang = jnp.arange(seq, dtype=F32)[:, None] * lane_freq[None, :]
    cos, sin = jnp.cos(ang), jnp.sin(ang)
    sinp = jnp.where((offset >= half) & (offset < ROT_DIM), sin, 0.0)
    sinm = jnp.where(offset < half, -sin, 0.0)
    return cos, sinp, sinm


def kernel(x, attn_norm, diff_w_in, diff_q_gain, diff_k_gain, diff_lambda_q1, diff_lambda_k1,
           diff_lambda_q2, diff_lambda_k2, diff_subln, diff_w_out,
           conv_norm, conv_w_in, conv_b_in, conv_dw_w, conv_dw_b, conv_ln_g, conv_ln_b,
           conv_w_out, conv_b_out,
           ffn_norm, router_group_w, router_group_b, router_expert_w, router_expert_b,
           moe_w_gate, moe_w_up, moe_w_down):
    batch, seq, d = x.shape
    assert d == D_MODEL and seq % ATTN_TILE == 0 and seq % CONV_TILE == 0
    n = batch * seq
    depth = ffn_norm.shape[0]
    cos_t, sinp_t, sinm_t = _rope_tables(seq)
    row = lambda v: v.reshape(1, -1)
    x2 = x.reshape(n, d)
    for i in range(depth):
        j = i // 2
        if i % 2 == 0:
            lambda_init = 0.8 - 0.6 * math.exp(-0.3 * i)
            dup = lambda v: jnp.concatenate([v, v]).reshape(1, HEAD_W)
            q2, k2, vt4 = _qkv_project(x2, row(attn_norm[j]), diff_w_in[j].astype(BF16),
                                       dup(diff_q_gain[j]), dup(diff_k_gain[j]), cos_t, sinp_t, sinm_t,
                                       batch, seq)
            lam_params = jnp.stack([diff_lambda_q1[j], diff_lambda_k1[j], diff_lambda_q2[j], diff_lambda_k2[j]])
            score_bound = (SCORE_BOUND_UNIT * jnp.max(jnp.abs(diff_q_gain[j]))
                           * jnp.max(jnp.abs(diff_k_gain[j]))).reshape(1)
            o3 = _diff_attention(score_bound, q2.reshape(batch, seq, d), k2.reshape(batch, seq, d), vt4,
                                 lam_params, row(diff_subln[j]), lambda_init)
            x2 = _proj_residual(x2, o3.reshape(n, d), diff_w_out[j].astype(BF16))
        else:
            dw_w = jnp.concatenate([conv_dw_w[j], jnp.zeros((CONV_HALO - CONV_WIDTH, d), F32)], axis=0)
            x2 = _conv_block(x2.reshape(batch, seq, d), row(conv_norm[j]), conv_w_in[j].astype(BF16),
                             row(conv_b_in[j]), dw_w, row(conv_dw_b[j]), row(conv_ln_g[j]), row(conv_ln_b[j]),
                             conv_w_out[j].astype(BF16), row(conv_b_out[j])).reshape(n, d)
        x2 = _hier_moe(x2, row(ffn_norm[i]), router_group_w[i], router_group_b[i],
                       router_expert_w[i], router_expert_b[i], moe_w_gate, moe_w_up, moe_w_down, i)
    return x2.reshape(batch, seq, d)
```

```python
import functools
import math

import jax
import jax.numpy as jnp
from jax import lax
from jax.experimental import pallas as pl
from jax.experimental.pallas import tpu as pltpu

F32 = jnp.float32
BF16 = jnp.bfloat16
I32 = jnp.int32

D_MODEL = 1024
HEAD_DIM = 64
HEAD_W = 2 * HEAD_DIM
N_HEADS = D_MODEL // HEAD_W
ROT_DIM = HEAD_DIM // 4
ROPE_THETA = 500000.0
CHUNK = 64
CONV_WIDTH = 31
SUBLANES = 8
CONV_HALO = 32
N_GROUPS = 4
EXPERTS_PER_GROUP = 8
N_EXPERTS = N_GROUPS * EXPERTS_PER_GROUP
D_EXPERT = D_MODEL // 2
EPS = 1e-6
ROUTER_ROWS = 64
LOG2_E = math.log2(math.e)
Q_SCALE = HEAD_DIM ** -0.5 * LOG2_E
SCORE_BOUND_UNIT = 1.02 * HEAD_DIM * Q_SCALE
UNSHIFTED_SOFTMAX_LIMIT = 50.0

ATTN_TILE = 1024
ATTN_GROUPS = (2, 1)
ROW_TILE = 512
CONV_TILE = 512
ROUTER_TILE = 512
MOE_ROW_TILE = 256
COMBINE_TILE = 256
VMEM_LIMIT = 48 * 1024 * 1024

_NT = (((1,), (1,)), ((), ()))


def _rmsnorm_rows(x, g):
    return x * lax.rsqrt(jnp.mean(x * x, axis=-1, keepdims=True) + EPS) * g


def _qkv_kernel(x_ref, g_ref, w_ref, qg_ref, kg_ref, cos_ref, sinp_ref, sinm_ref,
                q_ref, k_ref, vt_ref, *, col_tile):
    tm = x_ref.shape[0]
    h = _rmsnorm_rows(x_ref[...], g_ref[...]).astype(BF16)
    lane = lax.broadcasted_iota(I32, (tm, HEAD_W), 1)
    lo = lane < HEAD_DIM
    cosv, sinp, sinm = cos_ref[...], sinp_ref[...], sinm_ref[...]

    def norm_rope(y, gain, scale):
        y2 = y * y
        s_lo = jnp.sum(jnp.where(lo, y2, 0.0), axis=-1, keepdims=True)
        s_hi = jnp.sum(y2, axis=-1, keepdims=True) - s_lo
        r = jnp.where(lo, lax.rsqrt(s_lo * (1.0 / HEAD_DIM) + EPS), lax.rsqrt(s_hi * (1.0 / HEAD_DIM) + EPS))
        yn = y * r * gain
        half = ROT_DIM // 2
        out = yn * cosv + pltpu.roll(yn, half, 1) * sinp + pltpu.roll(yn, HEAD_W - half, 1) * sinm
        return out * scale

    for c in range(3 * D_MODEL // col_tile):
        y = jnp.dot(h, w_ref[:, c * col_tile:(c + 1) * col_tile], preferred_element_type=F32)
        for s in range(col_tile // HEAD_W):
            col = c * col_tile + s * HEAD_W
            ys = y[:, s * HEAD_W:(s + 1) * HEAD_W]
            if col < D_MODEL:
                q_ref[:, col:col + HEAD_W] = norm_rope(ys, qg_ref[...], Q_SCALE).astype(BF16)
            elif col < 2 * D_MODEL:
                k_ref[:, col - D_MODEL:col - D_MODEL + HEAD_W] = norm_rope(ys, kg_ref[...], 1.0).astype(BF16)
            else:
                vt_ref[col - 2 * D_MODEL:col - 2 * D_MODEL + HEAD_W, :] = ys.T.astype(BF16)


def _qkv_project(x2, g, w_bf, q_gain, k_gain, cos_t, sinp_t, sinm_t, batch, seq):
    n = x2.shape[0]
    tm = ATTN_TILE
    tiles_per_seq = seq // tm
    row = lambda i: (i, 0)
    fixed = lambda i: (0, 0)
    tab = lambda i: (i % tiles_per_seq, 0)
    return pl.pallas_call(
        functools.partial(_qkv_kernel, col_tile=512),
        grid=(n // tm,),
        in_specs=[
            pl.BlockSpec((tm, D_MODEL), row),
            pl.BlockSpec((1, D_MODEL), fixed),
            pl.BlockSpec((D_MODEL, 3 * D_MODEL), fixed),
            pl.BlockSpec((1, HEAD_W), fixed),
            pl.BlockSpec((1, HEAD_W), fixed),
            pl.BlockSpec((tm, HEAD_W), tab),
            pl.BlockSpec((tm, HEAD_W), tab),
            pl.BlockSpec((tm, HEAD_W), tab),
        ],
        out_specs=[
            pl.BlockSpec((tm, D_MODEL), row),
            pl.BlockSpec((tm, D_MODEL), row),
            pl.BlockSpec((None, None, D_MODEL, tm), lambda i: (i // tiles_per_seq, i % tiles_per_seq, 0, 0)),
        ],
        out_shape=[
            jax.ShapeDtypeStruct((n, D_MODEL), BF16),
            jax.ShapeDtypeStruct((n, D_MODEL), BF16),
            jax.ShapeDtypeStruct((batch, tiles_per_seq, D_MODEL, tm), BF16),
        ],
        compiler_params=pltpu.CompilerParams(dimension_semantics=("arbitrary",), vmem_limit_bytes=VMEM_LIMIT),
        name="qkv_project",
    )(x2, g, w_bf, q_gain, k_gain, cos_t, sinp_t, sinm_t)


def _attn_kernel(bound_ref, lam_ref, q_ref, k_ref, vt_ref, g_ref, o_ref, m_ref, l_ref, acc_ref, *, lambda_init):
    t = q_ref.shape[0]
    i = pl.program_id(2)
    q = q_ref[...]
    lane = lax.broadcasted_iota(I32, (t, HEAD_W), 1)
    zero = jnp.zeros_like(q)
    q_comp = (jnp.where(lane < HEAD_DIM, q, zero), jnp.where(lane >= HEAD_DIM, q, zero))

    l_ref[...] = jnp.zeros(l_ref.shape, F32)
    acc_ref[...] = jnp.zeros(acc_ref.shape, F32)

    def tiles(j):
        return k_ref[pl.ds(pl.multiple_of(j * t, t), t), :], vt_ref[j]

    def chunk_mask(n_keys=t, n_queries=t):
        key_chunk = lax.broadcasted_iota(I32, (n_keys, n_queries), 0) // CHUNK
        qry_chunk = lax.broadcasted_iota(I32, (n_keys, n_queries), 1) // CHUNK
        return key_chunk <= qry_chunk

    def unshifted_block(kt, vt, c, q0, masked):
        n_keys = kt.shape[0]
        s = lax.dot_general(kt, q_comp[c][q0:], _NT, preferred_element_type=F32)
        p = jnp.exp2(s)
        if masked:
            p = jnp.where(chunk_mask(n_keys, t - q0), p, 0.0)
        l_ref[c, :, q0:] += jnp.sum(p, axis=0, keepdims=True)
        acc_ref[c, :, q0:] += jnp.dot(vt, p.astype(BF16), preferred_element_type=F32)

    def unshifted_step(j, masked):
        kt, vt = tiles(j)
        half = t // 2
        for c in range(2):
            if masked:
                unshifted_block(kt[:half], vt[:, :half], c, 0, True)
                unshifted_block(kt[half:], vt[:, half:], c, half, True)
            else:
                unshifted_block(kt, vt, c, 0, False)

    def online_step(j, masked):
        kt, vt = tiles(j)
        for c in range(2):
            s = lax.dot_general(kt, q_comp[c], _NT, preferred_element_type=F32)
            if masked:
                s = jnp.where(chunk_mask(), s, -jnp.inf)
            m_old = m_ref[c]
            m_new = jnp.maximum(m_old, jnp.max(s, axis=0, keepdims=True))
            alpha = jnp.exp2(m_old - m_new)
            p = jnp.exp2(s - m_new)
            l_ref[c] = alpha * l_ref[c] + jnp.sum(p, axis=0, keepdims=True)
            acc_ref[c] = alpha * acc_ref[c] + jnp.dot(vt, p.astype(BF16), preferred_element_type=F32)
            m_ref[c] = m_new

    def sweep(step, groups):
        done = 0
        for group in groups:
            def trip(jj, carry, group=group, done=done):
                for u in range(group):
                    step(done + jj * group + u, False)
                return carry

            n_trips = (i - done) // group
            lax.fori_loop(0, n_trips, trip, 0)
            done = done + n_trips * group
        step(i, True)

    unshifted_ok = bound_ref[0] <= UNSHIFTED_SOFTMAX_LIMIT

    @pl.when(unshifted_ok)
    def _():
        sweep(unshifted_step, ATTN_GROUPS)

    @pl.when(jnp.logical_not(unshifted_ok))
    def _():
        m_ref[...] = jnp.full(m_ref.shape, -jnp.inf, F32)
        sweep(online_step, (1,))

    lam_p = lam_ref[...]
    lam = (jnp.exp(jnp.sum(lam_p[0:1] * lam_p[1:2], axis=-1, keepdims=True))
           - jnp.exp(jnp.sum(lam_p[2:3] * lam_p[3:4], axis=-1, keepdims=True)) + lambda_init)
    ot = acc_ref[0] * (1.0 / l_ref[0]) - lam * (acc_ref[1] * (1.0 / l_ref[1]))
    ot = ot * lax.rsqrt(jnp.mean(ot * ot, axis=0, keepdims=True) + EPS)
    o_ref[...] = (ot.T * g_ref[...] * (1.0 - lambda_init)).astype(o_ref.dtype)


def _diff_attention(score_bound, q3, k3, vt4, lam_params, subln_g, lambda_init):
    batch, seq, _ = q3.shape
    t = ATTN_TILE
    nq = seq // t
    return pl.pallas_call(
        functools.partial(_attn_kernel, lambda_init=lambda_init),
        grid_spec=pltpu.PrefetchScalarGridSpec(
            num_scalar_prefetch=1,
            grid=(batch, N_HEADS, nq),
            in_specs=[
                pl.BlockSpec((4, HEAD_DIM), lambda b, h, i, bound: (0, 0)),
                pl.BlockSpec((None, t, HEAD_W), lambda b, h, i, bound: (b, i, h)),
                pl.BlockSpec((None, seq, HEAD_W), lambda b, h, i, bound: (b, 0, h)),
                pl.BlockSpec((None, nq, HEAD_W, t), lambda b, h, i, bound: (b, 0, h, 0)),
                pl.BlockSpec((1, HEAD_W), lambda b, h, i, bound: (0, 0)),
            ],
            out_specs=pl.BlockSpec((None, t, HEAD_W), lambda b, h, i, bound: (b, i, h)),
            scratch_shapes=[
                pltpu.VMEM((2, 1, t), F32),
                pltpu.VMEM((2, 1, t), F32),
                pltpu.VMEM((2, HEAD_W, t), F32),
            ],
        ),
        out_shape=jax.ShapeDtypeStruct((batch, seq, D_MODEL), BF16),
        compiler_params=pltpu.CompilerParams(
            dimension_semantics=("arbitrary", "arbitrary", "arbitrary"), vmem_limit_bytes=VMEM_LIMIT),
        name="diff_attention",
    )(score_bound, lam_params, q3, k3, vt4, subln_g)


def _proj_residual_kernel(x_ref, y_ref, w_ref, o_ref):
    o_ref[...] = x_ref[...] + jnp.dot(y_ref[...], w_ref[...], preferred_element_type=F32)


def _proj_residual(x2, y2, w_bf):
    n = x2.shape[0]
    tm = ROW_TILE
    return pl.pallas_call(
        _proj_residual_kernel,
        grid=(n // tm,),
        in_specs=[
            pl.BlockSpec((tm, D_MODEL), lambda i: (i, 0)),
            pl.BlockSpec((tm, D_MODEL), lambda i: (i, 0)),
            pl.BlockSpec((D_MODEL, D_MODEL), lambda i: (0, 0)),
        ],
        out_specs=pl.BlockSpec((tm, D_MODEL), lambda i: (i, 0)),
        out_shape=jax.ShapeDtypeStruct((n, D_MODEL), F32),
        compiler_params=pltpu.CompilerParams(dimension_semantics=("arbitrary",), vmem_limit_bytes=VMEM_LIMIT),
        name="attn_out_proj",
    )(x2, y2, w_bf)


def _conv_kernel(x_ref, g_ref, win_ref, bin_ref, dww_ref, dwb_ref, lng_ref, lnb_ref, wout_ref, bout_ref,
                 o_ref, u_ref):
    tm = x_ref.shape[0]
    x = x_ref[...]
    h = _rmsnorm_rows(x, g_ref[...]).astype(BF16)
    a = jnp.dot(h, win_ref[:, :D_MODEL], preferred_element_type=F32) + bin_ref[:, :D_MODEL]
    gate = jnp.dot(h, win_ref[:, D_MODEL:], preferred_element_type=F32) + bin_ref[:, D_MODEL:]
    u = a * jax.nn.sigmoid(gate)

    @pl.when(pl.program_id(1) == 0)
    def _():
        u_ref[0:CONV_HALO, :] = jnp.zeros((CONV_HALO, D_MODEL), F32)

    u_ref[CONV_HALO:CONV_HALO + tm, :] = u
    base = CONV_HALO - (CONV_WIDTH - 1)
    window = u_ref[...]
    n_rows = tm + CONV_HALO
    acc = jnp.zeros((tm, D_MODEL), F32) + dwb_ref[...]
    for phase in range(SUBLANES):
        rolled = window if phase == 0 else pltpu.roll(window, n_rows - phase, 0)
        for row0 in range(0, CONV_HALO + 1, SUBLANES):
            w = row0 + phase - base
            if 0 <= w < CONV_WIDTH:
                acc = acc + dww_ref[w:w + 1, :] * rolled[row0:row0 + tm, :]
    u_ref[0:CONV_HALO, :] = u_ref[tm:tm + CONV_HALO, :]

    mu = jnp.mean(acc, axis=-1, keepdims=True)
    xc = acc - mu
    var = jnp.mean(xc * xc, axis=-1, keepdims=True)
    v = xc * lax.rsqrt(var + EPS) * lng_ref[...] + lnb_ref[...]
    v = (v * jax.nn.sigmoid(v)).astype(BF16)
    o_ref[...] = x + jnp.dot(v, wout_ref[...], preferred_element_type=F32) + bout_ref[...]


def _conv_block(x3, g, win_bf, b_in, dw_w, dw_b, ln_g, ln_b, wout_bf, b_out):
    batch, seq, _ = x3.shape
    tm = CONV_TILE
    row = lambda b, s: (b, s, 0)
    fixed = lambda b, s: (0, 0)
    vec = pl.BlockSpec((1, D_MODEL), fixed)
    return pl.pallas_call(
        _conv_kernel,
        grid=(batch, seq // tm),
        in_specs=[
            pl.BlockSpec((None, tm, D_MODEL), row),
            vec,
            pl.BlockSpec((D_MODEL, 2 * D_MODEL), fixed),
            pl.BlockSpec((1, 2 * D_MODEL), fixed),
            pl.BlockSpec((CONV_HALO, D_MODEL), fixed),
            vec, vec, vec,
            pl.BlockSpec((D_MODEL, D_MODEL), fixed),
            vec,
        ],
        out_specs=pl.BlockSpec((None, tm, D_MODEL), row),
        out_shape=jax.ShapeDtypeStruct((batch, seq, D_MODEL), F32),
        scratch_shapes=[pltpu.VMEM((tm + CONV_HALO, D_MODEL), F32)],
        compiler_params=pltpu.CompilerParams(
            dimension_semantics=("arbitrary", "arbitrary"), vmem_limit_bytes=VMEM_LIMIT),
        name="conformer_conv",
    )(x3, g, win_bf, b_in, dw_w, dw_b, ln_g, ln_b, wout_bf, b_out)


def _store_token_major(ref, v, lead=()):
    t = v.shape[0]
    for s in range(SUBLANES):
        ref[lead + (pl.ds(s, t, stride=SUBLANES), slice(None))] = v[:, s * 128:(s + 1) * 128]


def _load_token_major(ref, t, lead=()):
    return jnp.concatenate(
        [ref[lead + (pl.ds(s, t, stride=SUBLANES), slice(None))] for s in range(SUBLANES)], axis=1)


def _split_bf16(v):
    hi = v.astype(BF16)
    lo = (v - hi.astype(F32)).astype(BF16)
    return hi, lo


def _router_kernel(x_ref, g_ref, wr_ref, br_ref, meta_ref, wts_ref, cnt_ref, h3_ref, carry_ref, tri_ref):
    t = x_ref.shape[0]

    @pl.when(pl.program_id(0) == 0)
    def _():
        carry_ref[...] = jnp.zeros(carry_ref.shape, F32)
        before = lax.broadcasted_iota(I32, (t, t), 0) < lax.broadcasted_iota(I32, (t, t), 1)
        tri_ref[...] = jnp.where(before, 1.0, 0.0).astype(BF16)

    h = _rmsnorm_rows(x_ref[...], g_ref[...])
    _store_token_major(h3_ref, h)
    h_hi, h_lo = _split_bf16(h)
    w_hi, w_lo = _split_bf16(wr_ref[...])
    logits = (lax.dot_general(w_hi, h_hi, _NT, preferred_element_type=F32)
              + lax.dot_general(w_hi, h_lo, _NT, preferred_element_type=F32)
              + lax.dot_general(w_lo, h_hi, _NT, preferred_element_type=F32)) + br_ref[...]

    gl = logits[N_EXPERTS:N_EXPERTS + N_GROUPS]
    gmax = jnp.max(gl, axis=0, keepdims=True)
    g_p = 1.0 / jnp.sum(jnp.exp(gl - gmax), axis=0, keepdims=True)
    row_g = lax.broadcasted_iota(I32, gl.shape, 0)
    g_idx = jnp.min(jnp.where(gl == gmax, row_g, N_GROUPS), axis=0, keepdims=True)

    e_sel = jnp.zeros((EXPERTS_PER_GROUP, t), F32)
    for grp in range(N_GROUPS):
        e_sel = e_sel + jnp.where(g_idx == grp, logits[grp * EXPERTS_PER_GROUP:(grp + 1) * EXPERTS_PER_GROUP], 0.0)
    row_e = lax.broadcasted_iota(I32, e_sel.shape, 0)
    m1 = jnp.max(e_sel, axis=0, keepdims=True)
    i1 = jnp.min(jnp.where(e_sel == m1, row_e, EXPERTS_PER_GROUP), axis=0, keepdims=True)
    rest = jnp.where(row_e == i1, -jnp.inf, e_sel)
    m2 = jnp.max(rest, axis=0, keepdims=True)
    i2 = jnp.min(jnp.where(rest == m2, row_e, EXPERTS_PER_GROUP), axis=0, keepdims=True)
    r = jnp.exp(m2 - m1)
    p1 = 1.0 / (1.0 + r)
    f1 = g_idx * EXPERTS_PER_GROUP + i1
    f2 = g_idx * EXPERTS_PER_GROUP + i2

    row_x = lax.broadcasted_iota(I32, (N_EXPERTS, t), 0)
    oh1 = row_x == f1
    oh2 = row_x == f2
    onehot = jnp.where(oh1 | oh2, 1.0, 0.0)
    prior = jnp.dot(onehot.astype(BF16), tri_ref[...], preferred_element_type=F32) + carry_ref[:, 0:1]
    rank1 = jnp.sum(jnp.where(oh1, prior, 0.0), axis=0, keepdims=True)
    rank2 = jnp.sum(jnp.where(oh2, prior, 0.0), axis=0, keepdims=True)
    total = carry_ref[...] + jnp.sum(onehot, axis=1, keepdims=True)
    carry_ref[...] = total
    cnt_ref[...] = total.astype(I32)

    zi = jnp.zeros((4, t), I32)
    meta_ref[...] = jnp.concatenate([f1, f2, rank1.astype(I32), rank2.astype(I32), zi], axis=0)
    wts_ref[...] = jnp.concatenate([g_p * p1, g_p * (r * p1), jnp.zeros((6, t), F32)], axis=0)


def _route(x2, g, wr, br):
    n = x2.shape[0]
    t = ROUTER_TILE
    return pl.pallas_call(
        _router_kernel,
        grid=(n // t,),
        in_specs=[
            pl.BlockSpec((t, D_MODEL), lambda i: (i, 0)),
            pl.BlockSpec((1, D_MODEL), lambda i: (0, 0)),
            pl.BlockSpec((ROUTER_ROWS, D_MODEL), lambda i: (0, 0)),
            pl.BlockSpec((ROUTER_ROWS, 1), lambda i: (0, 0)),
        ],
        out_specs=[
            pl.BlockSpec((8, t), lambda i: (0, i)),
            pl.BlockSpec((8, t), lambda i: (0, i)),
            pl.BlockSpec((N_EXPERTS, 128), lambda i: (0, 0)),
            pl.BlockSpec((t * SUBLANES, 128), lambda i: (i, 0)),
        ],
        out_shape=[
            jax.ShapeDtypeStruct((8, n), I32),
            jax.ShapeDtypeStruct((8, n), F32),
            jax.ShapeDtypeStruct((N_EXPERTS, 128), I32),
            jax.ShapeDtypeStruct((n * SUBLANES, 128), F32),
        ],
        scratch_shapes=[pltpu.VMEM((N_EXPERTS, 128), F32), pltpu.VMEM((t, t), BF16)],
        compiler_params=pltpu.CompilerParams(dimension_semantics=("arbitrary",), vmem_limit_bytes=VMEM_LIMIT),
        name="moe_router",
    )(x2, g, wr, br)


def _positions_kernel(cnt_ref, meta_ref, pos_ref, tile_expert_ref, info_ref, tile_valid_ref, off_ref,
                      *, max_tiles):
    def per_expert(e, tiles_before):
        count = cnt_ref[e, 0]
        n_tiles = (count + (MOE_ROW_TILE - 1)) // MOE_ROW_TILE
        off_ref[e] = tiles_before * MOE_ROW_TILE

        def fill(k, c):
            tile_expert_ref[tiles_before + k] = e
            tile_valid_ref[tiles_before + k] = jnp.minimum(count - k * MOE_ROW_TILE, MOE_ROW_TILE)
            return c

        lax.fori_loop(0, n_tiles, fill, 0)
        return tiles_before + n_tiles

    used = lax.fori_loop(0, N_EXPERTS, per_expert, jnp.int32(0))
    info_ref[0] = used
    last_expert = tile_expert_ref[used - 1]

    def fill_tail(k, c):
        tile_expert_ref[k] = last_expert
        tile_valid_ref[k] = 0
        return c

    lax.fori_loop(used, max_tiles, fill_tail, 0)

    ids = meta_ref[0:2, :]
    pos = meta_ref[2:4, :]
    for e in range(N_EXPERTS):
        pos = pos + jnp.where(ids == e, off_ref[e], 0)
    pos_ref[...] = pos


def _positions(counts, meta, max_tiles):
    n = meta.shape[1]
    smem = pl.BlockSpec(memory_space=pltpu.SMEM)
    return pl.pallas_call(
        functools.partial(_positions_kernel, max_tiles=max_tiles),
        in_specs=[smem, pl.BlockSpec((8, n), lambda: (0, 0))],
        out_specs=[pl.BlockSpec((2, n), lambda: (0, 0)), smem, smem, smem],
        out_shape=[
            jax.ShapeDtypeStruct((2, n), I32),
            jax.ShapeDtypeStruct((max_tiles,), I32),
            jax.ShapeDtypeStruct((1,), I32),
            jax.ShapeDtypeStruct((max_tiles,), I32),
        ],
        scratch_shapes=[pltpu.SMEM((N_EXPERTS,), I32)],
        name="moe_positions",
    )(counts, meta)


def _inverse_kernel(pos_ref, tile_valid_ref, src_ref):
    tm = MOE_ROW_TILE
    n_tokens = pos_ref.shape[0] // 2

    def pad_tile(t, c):
        def pad_row(r, c2):
            src_ref[t * tm + r] = 0
            return c2

        lax.fori_loop(tile_valid_ref[t], tm, pad_row, 0)
        return c

    lax.fori_loop(0, src_ref.shape[0] // tm, pad_tile, 0)

    def token(n, c):
        src_ref[pos_ref[n]] = n
        src_ref[pos_ref[n_tokens + n]] = n
        return c

    lax.fori_loop(0, n_tokens, token, 0, unroll=8)


def _inverse_map(pos_flat, tile_valid, n_rows):
    smem = pl.BlockSpec(memory_space=pltpu.SMEM)
    return pl.pallas_call(
        _inverse_kernel,
        in_specs=[smem, smem],
        out_specs=smem,
        out_shape=jax.ShapeDtypeStruct((n_rows,), I32),
        name="moe_inverse_map",
    )(pos_flat, tile_valid)


def _token_rows(ref, token):
    return ref.at[pl.ds(pl.multiple_of(token * SUBLANES, SUBLANES), SUBLANES)]


def _expert_kernel(tile_expert_ref, info_ref, src_ref, x_hbm, wg_ref, wu_ref, wd_ref, ys_ref,
                   xbuf, wg_bf, wu_bf, wd_bf, gsem):
    j = pl.program_id(0)
    used = info_ref[0]
    tm = MOE_ROW_TILE

    def start_gather(tile, s):
        for r in range(tm):
            pltpu.make_async_copy(_token_rows(x_hbm, src_ref[tile * tm + r]),
                                  xbuf.at[s, pl.ds(r * SUBLANES, SUBLANES)], gsem.at[s]).start(priority=r % 2)

    def wait_gather(s):
        pltpu.make_async_copy(x_hbm.at[pl.ds(0, tm * SUBLANES)], xbuf.at[s], gsem.at[s]).wait()

    def cast_weights():
        wg_bf[...] = wg_ref[...].astype(BF16)
        wu_bf[...] = wu_ref[...].astype(BF16)
        wd_bf[...] = wd_ref[...].astype(BF16)

    def mlp(s):
        x = _load_token_major(xbuf, tm, (s,)).astype(BF16)
        gate = jnp.dot(x, wg_bf[...], preferred_element_type=F32)
        up = jnp.dot(x, wu_bf[...], preferred_element_type=F32)
        hid = (gate * jax.nn.sigmoid(gate) * up).astype(BF16)
        _store_token_major(ys_ref, jnp.dot(hid, wd_bf[...], preferred_element_type=F32))

    @pl.when(j == 0)
    def _():
        start_gather(0, 0)
        wait_gather(0)
        cast_weights()
        start_gather(jnp.minimum(1, used - 1), 1)
        mlp(0)

    for s in range(2):
        @pl.when((j >= 1) & (j < used) & (lax.rem(j, 2) == s))
        def _():
            wait_gather(s)

            @pl.when(tile_expert_ref[j] != tile_expert_ref[j - 1])
            def _():
                cast_weights()

            start_gather(jnp.minimum(j + 1, used - 1), 1 - s)
            mlp(s)

        @pl.when((j == used - 1) & (lax.rem(j, 2) == s))
        def _():
            wait_gather(1 - s)

    @pl.when(j >= used)
    def _():
        ys_ref[...] = jnp.zeros(ys_ref.shape, F32)


def _experts(tile_expert, info, src, h3, w_gate, w_up, w_down, layer):
    tm = MOE_ROW_TILE
    wsel = lambda j, te, info, src: (layer, te[j], 0, 0)
    return pl.pallas_call(
        _expert_kernel,
        grid_spec=pltpu.PrefetchScalarGridSpec(
            num_scalar_prefetch=3,
            grid=(src.shape[0] // tm,),
            in_specs=[
                pl.BlockSpec(memory_space=pl.ANY),
                pl.BlockSpec((None, None, D_MODEL, D_EXPERT), wsel),
                pl.BlockSpec((None, None, D_MODEL, D_EXPERT), wsel),
                pl.BlockSpec((None, None, D_EXPERT, D_MODEL), wsel),
            ],
            out_specs=pl.BlockSpec((tm * SUBLANES, 128), lambda j, te, info, src: (j, 0)),
            scratch_shapes=[
                pltpu.VMEM((2, tm * SUBLANES, 128), F32),
                pltpu.VMEM((D_MODEL, D_EXPERT), BF16),
                pltpu.VMEM((D_MODEL, D_EXPERT), BF16),
                pltpu.VMEM((D_EXPERT, D_MODEL), BF16),
                pltpu.SemaphoreType.DMA((2,)),
            ],
        ),
        out_shape=jax.ShapeDtypeStruct((src.shape[0] * SUBLANES, 128), F32),
        compiler_params=pltpu.CompilerParams(dimension_semantics=("arbitrary",), vmem_limit_bytes=VMEM_LIMIT),
        name="moe_experts",
    )(tile_expert, info, src, h3, w_gate, w_up, w_down)


def _combine_kernel(pos_ref, x_ref, wts_ref, ys_hbm, o_ref, rows, sem, *, n_tokens, n_tiles):
    i = pl.program_id(0)
    tc = x_ref.shape[0]

    def start_gather(s):
        for r in range(tc):
            for k in range(2):
                pltpu.make_async_copy(_token_rows(ys_hbm, pos_ref[k * n_tokens + i * tc + r]),
                                      rows.at[s, k, pl.ds(r * SUBLANES, SUBLANES)], sem.at[s]).start(priority=k)

    def wait_gather(s):
        for k in range(2):
            pltpu.make_async_copy(ys_hbm.at[pl.ds(0, tc * SUBLANES)], rows.at[s, k], sem.at[s]).wait()

    def finish(s):
        eye = jnp.where(lax.broadcasted_iota(I32, (tc, tc), 0) == lax.broadcasted_iota(I32, (tc, tc), 1),
                        1.0, 0.0).astype(BF16)
        w_hi, w_lo = _split_bf16(wts_ref[...])
        w_col = (lax.dot_general(eye, w_hi, _NT, preferred_element_type=F32)
                 + lax.dot_general(eye, w_lo, _NT, preferred_element_type=F32))
        o_ref[...] = (x_ref[...] + w_col[:, 0:1] * _load_token_major(rows, tc, (s, 0))
                      + w_col[:, 1:2] * _load_token_major(rows, tc, (s, 1)))

    for s in range(2):
        @pl.when((i < n_tiles) & (lax.rem(i, 2) == s))
        def _():
            start_gather(s)

        @pl.when((i >= 1) & (lax.rem(i, 2) != s))
        def _():
            wait_gather(s)
            finish(s)


def _combine(pos_flat, x2, wts, ys):
    n = x2.shape[0]
    tc = COMBINE_TILE
    n_tiles = n // tc
    lagged = lambda i, pos: (jnp.maximum(i - 1, 0), 0)
    return pl.pallas_call(
        functools.partial(_combine_kernel, n_tokens=n, n_tiles=n_tiles),
        grid_spec=pltpu.PrefetchScalarGridSpec(
            num_scalar_prefetch=1,
            grid=(n_tiles + 1,),
            in_specs=[
                pl.BlockSpec((tc, D_MODEL), lagged),
                pl.BlockSpec((8, tc), lambda i, pos: (0, jnp.maximum(i - 1, 0))),
                pl.BlockSpec(memory_space=pl.ANY),
            ],
            out_specs=pl.BlockSpec((tc, D_MODEL), lagged),
            scratch_shapes=[pltpu.VMEM((2, 2, tc * SUBLANES, 128), F32), pltpu.SemaphoreType.DMA((2,))],
        ),
        out_shape=jax.ShapeDtypeStruct((n, D_MODEL), F32),
        compiler_params=pltpu.CompilerParams(dimension_semantics=("arbitrary",), vmem_limit_bytes=VMEM_LIMIT),
        name="moe_combine",
    )(pos_flat, x2, wts, ys)


def _hier_moe(x2, g, w_rg, b_rg, w_re, b_re, w_gate, w_up, w_down, layer):
    n = x2.shape[0]
    max_tiles = (2 * n) // MOE_ROW_TILE + N_EXPERTS
    wr = jnp.concatenate([
        jnp.transpose(w_re, (0, 2, 1)).reshape(N_EXPERTS, D_MODEL),
        w_rg.T,
        jnp.zeros((ROUTER_ROWS - N_EXPERTS - N_GROUPS, D_MODEL), F32)], axis=0)
    br = jnp.concatenate([
        b_re.reshape(N_EXPERTS), b_rg, jnp.zeros((ROUTER_ROWS - N_EXPERTS - N_GROUPS,), F32)])[:, None]

    meta, wts, counts, h3 = _route(x2, g, wr, br)
    pos, tile_expert, info, tile_valid = _positions(counts, meta, max_tiles)
    pos_flat = pos.reshape(2 * n)
    src = _inverse_map(pos_flat, tile_valid, max_tiles * MOE_ROW_TILE)
    ys = _experts(tile_expert, info, src, h3, w_gate, w_up, w_down, layer)
    return _combine(pos_flat, x2, wts, ys)


def _rope_tables(seq):
    half = ROT_DIM // 2
    offset = jnp.arange(HEAD_W) % HEAD_DIM
    inv_freq = ROPE_THETA ** (-(2 * (offset % half)).astype(F32) / ROT_DIM)
    lane_freq = jnp.where(offset < ROT_DIM, inv_freq, 0.0)
    ang = jnp.arange(seq, dtype=F32)[:, None] * lane_freq[None, :]
    cos, sin = jnp.cos(ang), jnp.sin(ang)
    sinp = jnp.where((offset >= half) & (offset < ROT_DIM), sin, 0.0)
    sinm = jnp.where(offset < half, -sin, 0.0)
    return cos, sinp, sinm


def kernel(x, attn_norm, diff_w_in, diff_q_gain, diff_k_gain, diff_lambda_q1, diff_lambda_k1,
           diff_lambda_q2, diff_lambda_k2, diff_subln, diff_w_out,
           conv_norm, conv_w_in, conv_b_in, conv_dw_w, conv_dw_b, conv_ln_g, conv_ln_b,
           conv_w_out, conv_b_out,
           ffn_norm, router_group_w, router_group_b, router_expert_w, router_expert_b,
           moe_w_gate, moe_w_up, moe_w_down):
    batch, seq, d = x.shape
    assert d == D_MODEL and seq % ATTN_TILE == 0 and seq % CONV_TILE == 0
    n = batch * seq
    depth = ffn_norm.shape[0]
    cos_t, sinp_t, sinm_t = _rope_tables(seq)
    row = lambda v: v.reshape(1, -1)
    x2 = x.reshape(n, d)
    for i in range(depth):
        j = i // 2
        if i % 2 == 0:
            lambda_init = 0.8 - 0.6 * math.exp(-0.3 * i)
            dup = lambda v: jnp.concatenate([v, v]).reshape(1, HEAD_W)
            q2, k2, vt4 = _qkv_project(x2, row(attn_norm[j]), diff_w_in[j].astype(BF16),
                                       dup(diff_q_gain[j]), dup(diff_k_gain[j]), cos_t, sinp_t, sinm_t,
                                       batch, seq)
            lam_params = jnp.stack([diff_lambda_q1[j], diff_lambda_k1[j], diff_lambda_q2[j], diff_lambda_k2[j]])
            score_bound = (SCORE_BOUND_UNIT * jnp.max(jnp.abs(diff_q_gain[j]))
                           * jnp.max(jnp.abs(diff_k_gain[j]))).reshape(1)
            o3 = _diff_attention(score_bound, q2.reshape(batch, seq, d), k2.reshape(batch, seq, d), vt4,
                                 lam_params, row(diff_subln[j]), lambda_init)
            x2 = _proj_residual(x2, o3.reshape(n, d), diff_w_out[j].astype(BF16))
        else:
            dw_w = jnp.concatenate([conv_dw_w[j], jnp.zeros((CONV_HALO - CONV_WIDTH, d), F32)], axis=0)
            x2 = _conv_block(x2.reshape(batch, seq, d), row(conv_norm[j]), conv_w_in[j].astype(BF16),
                             row(conv_b_in[j]), dw_w, row(conv_dw_b[j]), row(conv_ln_g[j]), row(conv_ln_b[j]),
                             conv_w_out[j].astype(BF16), row(conv_b_out[j])).reshape(n, d)
        x2 = _hier_moe(x2, row(ffn_norm[i]), router_group_w[i], router_group_b[i],
                       router_expert_w[i], router_expert_b[i], moe_w_gate, moe_w_up, moe_w_down, i)
    return x2.reshape(batch, seq, d)
```

```python
import functools
import math

import jax
import jax.numpy as jnp
from jax import lax
from jax.experimental import pallas as pl
from jax.experimental.pallas import tpu as pltpu

F32 = jnp.float32
BF16 = jnp.bfloat16
I32 = jnp.int32

D_MODEL = 1024
HEAD_DIM = 64
HEAD_W = 2 * HEAD_DIM
N_HEADS = D_MODEL // HEAD_W
ROT_DIM = HEAD_DIM // 4
ROPE_THETA = 500000.0
CHUNK = 64
CONV_WIDTH = 31
SUBLANES = 8
CONV_HALO = 32
N_GROUPS = 4
EXPERTS_PER_GROUP = 8
N_EXPERTS = N_GROUPS * EXPERTS_PER_GROUP
D_EXPERT = D_MODEL // 2
EPS = 1e-6
ROUTER_ROWS = 64
LOG2_E = math.log2(math.e)
Q_SCALE = HEAD_DIM ** -0.5 * LOG2_E
SCORE_BOUND_UNIT = 1.02 * HEAD_DIM * Q_SCALE
UNSHIFTED_SOFTMAX_LIMIT = 50.0

ATTN_TILE = 1024
ATTN_GROUPS = (2, 1)
ROW_TILE = 512
CONV_TILE = 512
ROUTER_TILE = 512
MOE_ROW_TILE = 256
COMBINE_TILE = 256
VMEM_LIMIT = 48 * 1024 * 1024

_NT = (((1,), (1,)), ((), ()))


def _rmsnorm_rows(x, g):
    return x * lax.rsqrt(jnp.mean(x * x, axis=-1, keepdims=True) + EPS) * g


def _qkv_kernel(x_ref, g_ref, w_ref, qg_ref, kg_ref, cos_ref, sinp_ref, sinm_ref,
                q_ref, k_ref, vt_ref, *, col_tile):
    tm = x_ref.shape[0]
    h = _rmsnorm_rows(x_ref[...], g_ref[...]).astype(BF16)
    lane = lax.broadcasted_iota(I32, (tm, HEAD_W), 1)
    lo = lane < HEAD_DIM
    cosv, sinp, sinm = cos_ref[...], sinp_ref[...], sinm_ref[...]

    def norm_rope(y, gain, scale):
        y2 = y * y
        s_lo = jnp.sum(jnp.where(lo, y2, 0.0), axis=-1, keepdims=True)
        s_hi = jnp.sum(y2, axis=-1, keepdims=True) - s_lo
        r = jnp.where(lo, lax.rsqrt(s_lo * (1.0 / HEAD_DIM) + EPS), lax.rsqrt(s_hi * (1.0 / HEAD_DIM) + EPS))
        yn = y * r * gain
        half = ROT_DIM // 2
        out = yn * cosv + pltpu.roll(yn, half, 1) * sinp + pltpu.roll(yn, HEAD_W - half, 1) * sinm
        return out * scale

    for c in range(3 * D_MODEL // col_tile):
        y = jnp.dot(h, w_ref[:, c * col_tile:(c + 1) * col_tile], preferred_element_type=F32)
        for s in range(col_tile // HEAD_W):
            col = c * col_tile + s * HEAD_W
            ys = y[:, s * HEAD_W:(s + 1) * HEAD_W]
            if col < D_MODEL:
                q_ref[:, col:col + HEAD_W] = norm_rope(ys, qg_ref[...], Q_SCALE).astype(BF16)
            elif col < 2 * D_MODEL:
                k_ref[:, col - D_MODEL:col - D_MODEL + HEAD_W] = norm_rope(ys, kg_ref[...], 1.0).astype(BF16)
            else:
                vt_ref[col - 2 * D_MODEL:col - 2 * D_MODEL + HEAD_W, :] = ys.T.astype(BF16)


def _qkv_project(x2, g, w_bf, q_gain, k_gain, cos_t, sinp_t, sinm_t, batch, seq):
    n = x2.shape[0]
    tm = ATTN_TILE
    tiles_per_seq = seq // tm
    row = lambda i: (i, 0)
    fixed = lambda i: (0, 0)
    tab = lambda i: (i % tiles_per_seq, 0)
    return pl.pallas_call(
        functools.partial(_qkv_kernel, col_tile=512),
        grid=(n // tm,),
        in_specs=[
            pl.BlockSpec((tm, D_MODEL), row),
            pl.BlockSpec((1, D_MODEL), fixed),
            pl.BlockSpec((D_MODEL, 3 * D_MODEL), fixed),
            pl.BlockSpec((1, HEAD_W), fixed),
            pl.BlockSpec((1, HEAD_W), fixed),
            pl.BlockSpec((tm, HEAD_W), tab),
            pl.BlockSpec((tm, HEAD_W), tab),
            pl.BlockSpec((tm, HEAD_W), tab),
        ],
        out_specs=[
            pl.BlockSpec((tm, D_MODEL), row),
            pl.BlockSpec((tm, D_MODEL), row),
            pl.BlockSpec((None, None, D_MODEL, tm), lambda i: (i // tiles_per_seq, i % tiles_per_seq, 0, 0)),
        ],
        out_shape=[
            jax.ShapeDtypeStruct((n, D_MODEL), BF16),
            jax.ShapeDtypeStruct((n, D_MODEL), BF16),
            jax.ShapeDtypeStruct((batch, tiles_per_seq, D_MODEL, tm), BF16),
        ],
        compiler_params=pltpu.CompilerParams(dimension_semantics=("arbitrary",), vmem_limit_bytes=VMEM_LIMIT),
        name="qkv_project",
    )(x2, g, w_bf, q_gain, k_gain, cos_t, sinp_t, sinm_t)


def _attn_kernel(bound_ref, lam_ref, q_ref, k_ref, vt_ref, g_ref, o_ref, m_ref, l_ref, acc_ref, *, lambda_init):
    t = q_ref.shape[0]
    i = pl.program_id(2)
    q = q_ref[...]
    lane = lax.broadcasted_iota(I32, (t, HEAD_W), 1)
    zero = jnp.zeros_like(q)
    q_comp = (jnp.where(lane < HEAD_DIM, q, zero), jnp.where(lane >= HEAD_DIM, q, zero))

    l_ref[...] = jnp.zeros(l_ref.shape, F32)
    acc_ref[...] = jnp.zeros(acc_ref.shape, F32)

    def tiles(j):
        return k_ref[pl.ds(pl.multiple_of(j * t, t), t), :], vt_ref[j]

    def chunk_mask(n_keys=t, n_queries=t):
        key_chunk = lax.broadcasted_iota(I32, (n_keys, n_queries), 0) // CHUNK
        qry_chunk = lax.broadcasted_iota(I32, (n_keys, n_queries), 1) // CHUNK
        return key_chunk <= qry_chunk

    def unshifted_block(kt, vt, c, q0, masked):
        n_keys = kt.shape[0]
        s = lax.dot_general(kt, q_comp[c][q0:], _NT, preferred_element_type=F32)
        p = jnp.exp2(s)
        if masked:
            p = jnp.where(chunk_mask(n_keys, t - q0), p, 0.0)
        l_ref[c, :, q0:] += jnp.sum(p, axis=0, keepdims=True)
        acc_ref[c, :, q0:] += jnp.dot(vt, p.astype(BF16), preferred_element_type=F32)

    def unshifted_step(j, masked):
        kt, vt = tiles(j)
        half = t // 2
        for c in range(2):
            if masked:
                unshifted_block(kt[:half], vt[:, :half], c, 0, True)
                unshifted_block(kt[half:], vt[:, half:], c, half, True)
            else:
                unshifted_block(kt, vt, c, 0, False)

    def online_step(j, masked):
        kt, vt = tiles(j)
        for c in range(2):
            s = lax.dot_general(kt, q_comp[c], _NT, preferred_element_type=F32)
            if masked:
                s = jnp.where(chunk_mask(), s, -jnp.inf)
            m_old = m_ref[c]
            m_new = jnp.maximum(m_old, jnp.max(s, axis=0, keepdims=True))
            alpha = jnp.exp2(m_old - m_new)
            p = jnp.exp2(s - m_new)
            l_ref[c] = alpha * l_ref[c] + jnp.sum(p, axis=0, keepdims=True)
            acc_ref[c] = alpha * acc_ref[c] + jnp.dot(vt, p.astype(BF16), preferred_element_type=F32)
            m_ref[c] = m_new

    def sweep(step, groups):
        done = 0
        for group in groups:
            def trip(jj, carry, group=group, done=done):
                for u in range(group):
                    step(done + jj * group + u, False)
                return carry

            n_trips = (i - done) // group
            lax.fori_loop(0, n_trips, trip, 0)
            done = done + n_trips * group
        step(i, True)

    unshifted_ok = bound_ref[0] <= UNSHIFTED_SOFTMAX_LIMIT

    @pl.when(unshifted_ok)
    def _():
        sweep(unshifted_step, ATTN_GROUPS)

    @pl.when(jnp.logical_not(unshifted_ok))
    def _():
        m_ref[...] = jnp.full(m_ref.shape, -jnp.inf, F32)
        sweep(online_step, (1,))

    lam_p = lam_ref[...]
    lam = (jnp.exp(jnp.sum(lam_p[0:1] * lam_p[1:2], axis=-1, keepdims=True))
           - jnp.exp(jnp.sum(lam_p[2:3] * lam_p[3:4], axis=-1, keepdims=True)) + lambda_init)
    ot = acc_ref[0] * (1.0 / l_ref[0]) - lam * (acc_ref[1] * (1.0 / l_ref[1]))
    ot = ot * lax.rsqrt(jnp.mean(ot * ot, axis=0, keepdims=True) + EPS)
    o_ref[...] = (ot.T * g_ref[...] * (1.0 - lambda_init)).astype(o_ref.dtype)


def _diff_attention(score_bound, q3, k3, vt4, lam_params, subln_g, lambda_init):
    batch, seq, _ = q3.shape
    t = ATTN_TILE
    nq = seq // t
    return pl.pallas_call(
        functools.partial(_attn_kernel, lambda_init=lambda_init),
        grid_spec=pltpu.PrefetchScalarGridSpec(
            num_scalar_prefetch=1,
            grid=(batch, N_HEADS, nq),
            in_specs=[
                pl.BlockSpec((4, HEAD_DIM), lambda b, h, i, bound: (0, 0)),
                pl.BlockSpec((None, t, HEAD_W), lambda b, h, i, bound: (b, i, h)),
                pl.BlockSpec((None, seq, HEAD_W), lambda b, h, i, bound: (b, 0, h)),
                pl.BlockSpec((None, nq, HEAD_W, t), lambda b, h, i, bound: (b, 0, h, 0)),
                pl.BlockSpec((1, HEAD_W), lambda b, h, i, bound: (0, 0)),
            ],
            out_specs=pl.BlockSpec((None, t, HEAD_W), lambda b, h, i, bound: (b, i, h)),
            scratch_shapes=[
                pltpu.VMEM((2, 1, t), F32),
                pltpu.VMEM((2, 1, t), F32),
                pltpu.VMEM((2, HEAD_W, t), F32),
            ],
        ),
        out_shape=jax.ShapeDtypeStruct((batch, seq, D_MODEL), BF16),
        compiler_params=pltpu.CompilerParams(
            dimension_semantics=("arbitrary", "arbitrary", "arbitrary"), vmem_limit_bytes=VMEM_LIMIT),
        name="diff_attention",
    )(score_bound, lam_params, q3, k3, vt4, subln_g)


def _proj_residual_kernel(x_ref, y_ref, w_ref, o_ref):
    o_ref[...] = x_ref[...] + jnp.dot(y_ref[...], w_ref[...], preferred_element_type=F32)


def _proj_residual(x2, y2, w_bf):
    n = x2.shape[0]
    tm = ROW_TILE
    return pl.pallas_call(
        _proj_residual_kernel,
        grid=(n // tm,),
        in_specs=[
            pl.BlockSpec((tm, D_MODEL), lambda i: (i, 0)),
            pl.BlockSpec((tm, D_MODEL), lambda i: (i, 0)),
            pl.BlockSpec((D_MODEL, D_MODEL), lambda i: (0, 0)),
        ],
        out_specs=pl.BlockSpec((tm, D_MODEL), lambda i: (i, 0)),
        out_shape=jax.ShapeDtypeStruct((n, D_MODEL), F32),
        compiler_params=pltpu.CompilerParams(dimension_semantics=("arbitrary",), vmem_limit_bytes=VMEM_LIMIT),
        name="attn_out_proj",
    )(x2, y2, w_bf)


def _conv_kernel(x_ref, g_ref, win_ref, bin_ref, dww_ref, dwb_ref, lng_ref, lnb_ref, wout_ref, bout_ref,
                 o_ref, u_ref):
    tm = x_ref.shape[0]
    x = x_ref[...]
    h = _rmsnorm_rows(x, g_ref[...]).astype(BF16)
    a = jnp.dot(h, win_ref[:, :D_MODEL], preferred_element_type=F32) + bin_ref[:, :D_MODEL]
    gate = jnp.dot(h, win_ref[:, D_MODEL:], preferred_element_type=F32) + bin_ref[:, D_MODEL:]
    u = a * jax.nn.sigmoid(gate)

    @pl.when(pl.program_id(1) == 0)
    def _():
        u_ref[0:CONV_HALO, :] = jnp.zeros((CONV_HALO, D_MODEL), F32)

    u_ref[CONV_HALO:CONV_HALO + tm, :] = u
    base = CONV_HALO - (CONV_WIDTH - 1)
    window = u_ref[...]
    n_rows = tm + CONV_HALO
    acc = jnp.zeros((tm, D_MODEL), F32) + dwb_ref[...]
    for phase in range(SUBLANES):
        rolled = window if phase == 0 else pltpu.roll(window, n_rows - phase, 0)
        for row0 in range(0, CONV_HALO + 1, SUBLANES):
            w = row0 + phase - base
            if 0 <= w < CONV_WIDTH:
                acc = acc + dww_ref[w:w + 1, :] * rolled[row0:row0 + tm, :]
    u_ref[0:CONV_HALO, :] = u_ref[tm:tm + CONV_HALO, :]

    mu = jnp.mean(acc, axis=-1, keepdims=True)
    xc = acc - mu
    var = jnp.mean(xc * xc, axis=-1, keepdims=True)
    v = xc * lax.rsqrt(var + EPS) * lng_ref[...] + lnb_ref[...]
    v = (v * jax.nn.sigmoid(v)).astype(BF16)
    o_ref[...] = x + jnp.dot(v, wout_ref[...], preferred_element_type=F32) + bout_ref[...]


def _conv_block(x3, g, win_bf, b_in, dw_w, dw_b, ln_g, ln_b, wout_bf, b_out):
    batch, seq, _ = x3.shape
    tm = CONV_TILE
    row = lambda b, s: (b, s, 0)
    fixed = lambda b, s: (0, 0)
    vec = pl.BlockSpec((1, D_MODEL), fixed)
    return pl.pallas_call(
        _conv_kernel,
        grid=(batch, seq // tm),
        in_specs=[
            pl.BlockSpec((None, tm, D_MODEL), row),
            vec,
            pl.BlockSpec((D_MODEL, 2 * D_MODEL), fixed),
            pl.BlockSpec((1, 2 * D_MODEL), fixed),
            pl.BlockSpec((CONV_HALO, D_MODEL), fixed),
            vec, vec, vec,
            pl.BlockSpec((D_MODEL, D_MODEL), fixed),
            vec,
        ],
        out_specs=pl.BlockSpec((None, tm, D_MODEL), row),
        out_shape=jax.ShapeDtypeStruct((batch, seq, D_MODEL), F32),
        scratch_shapes=[pltpu.VMEM((tm + CONV_HALO, D_MODEL), F32)],
        compiler_params=pltpu.CompilerParams(
            dimension_semantics=("arbitrary", "arbitrary"), vmem_limit_bytes=VMEM_LIMIT),
        name="conformer_conv",
    )(x3, g, win_bf, b_in, dw_w, dw_b, ln_g, ln_b, wout_bf, b_out)


def _store_token_major(ref, v, lead=()):
    t = v.shape[0]
    for s in range(SUBLANES):
        ref[lead + (pl.ds(s, t, stride=SUBLANES), slice(None))] = v[:, s * 128:(s + 1) * 128]


def _load_token_major(ref, t, lead=()):
    return jnp.concatenate(
        [ref[lead + (pl.ds(s, t, stride=SUBLANES), slice(None))] for s in range(SUBLANES)], axis=1)


def _split_bf16(v):
    hi = v.astype(BF16)
    lo = (v - hi.astype(F32)).astype(BF16)
    return hi, lo


def _router_kernel(x_ref, g_ref, wr_ref, br_ref, meta_ref, wts_ref, cnt_ref, h3_ref, carry_ref, tri_ref):
    t = x_ref.shape[0]

    @pl.when(pl.program_id(0) == 0)
    def _():
        carry_ref[...] = jnp.zeros(carry_ref.shape, F32)
        before = lax.broadcasted_iota(I32, (t, t), 0) < lax.broadcasted_iota(I32, (t, t), 1)
        tri_ref[...] = jnp.where(before, 1.0, 0.0).astype(BF16)

    h = _rmsnorm_rows(x_ref[...], g_ref[...])
    _store_token_major(h3_ref, h)
    h_hi, h_lo = _split_bf16(h)
    w_hi, w_lo = _split_bf16(wr_ref[...])
    logits = (lax.dot_general(w_hi, h_hi, _NT, preferred_element_type=F32)
              + lax.dot_general(w_hi, h_lo, _NT, preferred_element_type=F32)
              + lax.dot_general(w_lo, h_hi, _NT, preferred_element_type=F32)) + br_ref[...]

    gl = logits[N_EXPERTS:N_EXPERTS + N_GROUPS]
    gmax = jnp.max(gl, axis=0, keepdims=True)
    g_p = 1.0 / jnp.sum(jnp.exp(gl - gmax), axis=0, keepdims=True)
    row_g = lax.broadcasted_iota(I32, gl.shape, 0)
    g_idx = jnp.min(jnp.where(gl == gmax, row_g, N_GROUPS), axis=0, keepdims=True)

    e_sel = jnp.zeros((EXPERTS_PER_GROUP, t), F32)
    for grp in range(N_GROUPS):
        e_sel = e_sel + jnp.where(g_idx == grp, logits[grp * EXPERTS_PER_GROUP:(grp + 1) * EXPERTS_PER_GROUP], 0.0)
    row_e = lax.broadcasted_iota(I32, e_sel.shape, 0)
    m1 = jnp.max(e_sel, axis=0, keepdims=True)
    i1 = jnp.min(jnp.where(e_sel == m1, row_e, EXPERTS_PER_GROUP), axis=0, keepdims=True)
    rest = jnp.where(row_e == i1, -jnp.inf, e_sel)
    m2 = jnp.max(rest, axis=0, keepdims=True)
    i2 = jnp.min(jnp.where(rest == m2, row_e, EXPERTS_PER_GROUP), axis=0, keepdims=True)
    r = jnp.exp(m2 - m1)
    p1 = 1.0 / (1.0 + r)
    f1 = g_idx * EXPERTS_PER_GROUP + i1
    f2 = g_idx * EXPERTS_PER_GROUP + i2

    row_x = lax.broadcasted_iota(I32, (N_EXPERTS, t), 0)
    oh1 = row_x == f1
    oh2 = row_x == f2
    onehot = jnp.where(oh1 | oh2, 1.0, 0.0)
    prior = jnp.dot(onehot.astype(BF16), tri_ref[...], preferred_element_type=F32) + carry_ref[:, 0:1]
    rank1 = jnp.sum(jnp.where(oh1, prior, 0.0), axis=0, keepdims=True)
    rank2 = jnp.sum(jnp.where(oh2, prior, 0.0), axis=0, keepdims=True)
    total = carry_ref[...] + jnp.sum(onehot, axis=1, keepdims=True)
    carry_ref[...] = total
    cnt_ref[...] = total.astype(I32)

    zi = jnp.zeros((4, t), I32)
    meta_ref[...] = jnp.concatenate([f1, f2, rank1.astype(I32), rank2.astype(I32), zi], axis=0)
    wts_ref[...] = jnp.concatenate([g_p * p1, g_p * (r * p1), jnp.zeros((6, t), F32)], axis=0)


def _route(x2, g, wr, br):
    n = x2.shape[0]
    t = ROUTER_TILE
    return pl.pallas_call(
        _router_kernel,
        grid=(n // t,),
        in_specs=[
            pl.BlockSpec((t, D_MODEL), lambda i: (i, 0)),
            pl.BlockSpec((1, D_MODEL), lambda i: (0, 0)),
            pl.BlockSpec((ROUTER_ROWS, D_MODEL), lambda i: (0, 0)),
            pl.BlockSpec((ROUTER_ROWS, 1), lambda i: (0, 0)),
        ],
        out_specs=[
            pl.BlockSpec((8, t), lambda i: (0, i)),
            pl.BlockSpec((8, t), lambda i: (0, i)),
            pl.BlockSpec((N_EXPERTS, 128), lambda i: (0, 0)),
            pl.BlockSpec((t * SUBLANES, 128), lambda i: (i, 0)),
        ],
        out_shape=[
            jax.ShapeDtypeStruct((8, n), I32),
            jax.ShapeDtypeStruct((8, n), F32),
            jax.ShapeDtypeStruct((N_EXPERTS, 128), I32),
            jax.ShapeDtypeStruct((n * SUBLANES, 128), F32),
        ],
        scratch_shapes=[pltpu.VMEM((N_EXPERTS, 128), F32), pltpu.VMEM((t, t), BF16)],
        compiler_params=pltpu.CompilerParams(dimension_semantics=("arbitrary",), vmem_limit_bytes=VMEM_LIMIT),
        name="moe_router",
    )(x2, g, wr, br)


def _positions_kernel(cnt_ref, meta_ref, pos_ref, tile_expert_ref, info_ref, tile_valid_ref, off_ref,
                      *, max_tiles):
    def per_expert(e, tiles_before):
        count = cnt_ref[e, 0]
        n_tiles = (count + (MOE_ROW_TILE - 1)) // MOE_ROW_TILE
        off_ref[e] = tiles_before * MOE_ROW_TILE

        def fill(k, c):
            tile_expert_ref[tiles_before + k] = e
            tile_valid_ref[tiles_before + k] = jnp.minimum(count - k * MOE_ROW_TILE, MOE_ROW_TILE)
            return c

        lax.fori_loop(0, n_tiles, fill, 0)
        return tiles_before + n_tiles

    used = lax.fori_loop(0, N_EXPERTS, per_expert, jnp.int32(0))
    info_ref[0] = used
    last_expert = tile_expert_ref[used - 1]

    def fill_tail(k, c):
        tile_expert_ref[k] = last_expert
        tile_valid_ref[k] = 0
        return c

    lax.fori_loop(used, max_tiles, fill_tail, 0)
    tile_expert_ref[max_tiles] = last_expert

    ids = meta_ref[0:2, :]
    pos = meta_ref[2:4, :]
    for e in range(N_EXPERTS):
        pos = pos + jnp.where(ids == e, off_ref[e], 0)
    pos_ref[...] = pos


def _positions(counts, meta, max_tiles):
    n = meta.shape[1]
    smem = pl.BlockSpec(memory_space=pltpu.SMEM)
    return pl.pallas_call(
        functools.partial(_positions_kernel, max_tiles=max_tiles),
        in_specs=[smem, pl.BlockSpec((8, n), lambda: (0, 0))],
        out_specs=[pl.BlockSpec((2, n), lambda: (0, 0)), smem, smem, smem],
        out_shape=[
            jax.ShapeDtypeStruct((2, n), I32),
            jax.ShapeDtypeStruct((max_tiles + 1,), I32),
            jax.ShapeDtypeStruct((1,), I32),
            jax.ShapeDtypeStruct((max_tiles,), I32),
        ],
        scratch_shapes=[pltpu.SMEM((N_EXPERTS,), I32)],
        name="moe_positions",
    )(counts, meta)


def _inverse_kernel(pos_ref, tile_expert_ref, tile_valid_ref, dst_ref, *, n_tokens):
    tm = MOE_ROW_TILE

    def pad_tile(t, c):
        pad_base = 2 * n_tokens + tile_expert_ref[t] * tm

        def pad_row(r, c2):
            dst_ref[t * tm + r] = pad_base + r
            return c2

        lax.fori_loop(tile_valid_ref[t], tm, pad_row, 0)
        return c

    lax.fori_loop(0, dst_ref.shape[0] // tm, pad_tile, 0)

    def token(n, c):
        dst_ref[pos_ref[n]] = n
        dst_ref[pos_ref[n_tokens + n]] = n_tokens + n
        return c

    lax.fori_loop(0, n_tokens, token, 0, unroll=8)


def _inverse_map(pos_flat, tile_expert, tile_valid, n_rows):
    smem = pl.BlockSpec(memory_space=pltpu.SMEM)
    return pl.pallas_call(
        functools.partial(_inverse_kernel, n_tokens=pos_flat.shape[0] // 2),
        in_specs=[smem, smem, smem],
        out_specs=smem,
        out_shape=jax.ShapeDtypeStruct((n_rows,), I32),
        name="moe_inverse_map",
    )(pos_flat, tile_expert, tile_valid)


def _expert_kernel(tile_expert_ref, info_ref, dst_ref, x_hbm, wg_ref, wu_ref, wd_ref,
                   out_hbm, xbuf, ybuf, wg_bf, wu_bf, wd_bf, gsem, ssem, *, n_tokens):
    j = pl.program_id(0)
    used = info_ref[0]
    tm = MOE_ROW_TILE

    def token_rows(ref, token):
        return ref.at[pl.ds(pl.multiple_of(token * SUBLANES, SUBLANES), SUBLANES)]


    def start_gather(tile, s):
        for r in range(tm):
            token = dst_ref[tile * tm + r] & (n_tokens - 1)
            pltpu.make_async_copy(token_rows(x_hbm, token), xbuf.at[s, pl.ds(r * SUBLANES, SUBLANES)],
                                  gsem.at[s]).start(priority=r % 2)

    def wait_gather(s):
        pltpu.make_async_copy(x_hbm.at[pl.ds(0, tm * SUBLANES)], xbuf.at[s], gsem.at[s]).wait()

    def start_scatter(tile, s):
        for r in range(tm):
            pltpu.make_async_copy(ybuf.at[s, pl.ds(r * SUBLANES, SUBLANES)],
                                  token_rows(out_hbm, dst_ref[tile * tm + r]),
                                  ssem.at[s]).start(priority=(r + 1) % 2)

    def wait_scatter(s):
        pltpu.make_async_copy(ybuf.at[s], out_hbm.at[pl.ds(0, tm * SUBLANES)], ssem.at[s]).wait()

    def cast_weights():
        wg_bf[...] = wg_ref[...].astype(BF16)
        wu_bf[...] = wu_ref[...].astype(BF16)
        wd_bf[...] = wd_ref[...].astype(BF16)

    def mlp(s, start_copies):
        x = _load_token_major(xbuf, tm, (s,)).astype(BF16)
        gate = jnp.dot(x, wg_bf[...], preferred_element_type=F32)
        up = jnp.dot(x, wu_bf[...], preferred_element_type=F32)
        hid = (gate * jax.nn.sigmoid(gate) * up).astype(BF16)
        y = jnp.dot(hid, wd_bf[...], preferred_element_type=F32)
        start_copies()
        _store_token_major(ybuf, y, (s,))

    @pl.when(j == 0)
    def _():
        def pad_rows_copy(e):
            return pltpu.make_async_copy(
                ybuf.at[1], out_hbm.at[pl.ds((2 * n_tokens + e * tm) * SUBLANES, tm * SUBLANES)], ssem.at[1])

        ybuf[1] = jnp.zeros(ybuf.shape[1:], F32)
        for e in range(N_EXPERTS):
            pad_rows_copy(e).start()
        start_gather(0, 0)
        for e in range(N_EXPERTS):
            pad_rows_copy(e).wait()
        wait_gather(0)
        cast_weights()
        mlp(0, lambda: start_gather(jnp.minimum(1, used - 1), 1))

    for s in range(2):
        @pl.when((j >= 1) & (j < used) & (lax.rem(j, 2) == s))
        def _():
            wait_gather(s)

            @pl.when(j >= 2)
            def _():
                wait_scatter(s)

            @pl.when(tile_expert_ref[j] != tile_expert_ref[j - 1])
            def _():
                cast_weights()

            def start_copies():
                start_gather(jnp.minimum(j + 1, used - 1), 1 - s)
                start_scatter(j - 1, 1 - s)

            mlp(s, start_copies)

        @pl.when((j == used) & (lax.rem(j, 2) == s))
        def _():
            wait_gather(s)

            @pl.when(j >= 2)
            def _():
                wait_scatter(s)

            start_scatter(j - 1, 1 - s)
            wait_scatter(1 - s)


def _experts(tile_expert, info, dst, h3, w_gate, w_up, w_down, layer):
    n = h3.shape[0] // SUBLANES
    assert n & (n - 1) == 0
    tm = MOE_ROW_TILE
    wsel = lambda j, te, info, dst: (layer, te[j], 0, 0)
    return pl.pallas_call(
        functools.partial(_expert_kernel, n_tokens=n),
        grid_spec=pltpu.PrefetchScalarGridSpec(
            num_scalar_prefetch=3,
            grid=(dst.shape[0] // tm + 1,),
            in_specs=[
                pl.BlockSpec(memory_space=pl.ANY),
                pl.BlockSpec((None, None, D_MODEL, D_EXPERT), wsel),
                pl.BlockSpec((None, None, D_MODEL, D_EXPERT), wsel),
                pl.BlockSpec((None, None, D_EXPERT, D_MODEL), wsel),
            ],
            out_specs=pl.BlockSpec(memory_space=pl.ANY),
            scratch_shapes=[
                pltpu.VMEM((2, tm * SUBLANES, 128), F32),
                pltpu.VMEM((2, tm * SUBLANES, 128), F32),
                pltpu.VMEM((D_MODEL, D_EXPERT), BF16),
                pltpu.VMEM((D_MODEL, D_EXPERT), BF16),
                pltpu.VMEM((D_EXPERT, D_MODEL), BF16),
                pltpu.SemaphoreType.DMA((2,)),
                pltpu.SemaphoreType.DMA((2,)),
            ],
        ),
        out_shape=jax.ShapeDtypeStruct(((2 * n + N_EXPERTS * tm) * SUBLANES, 128), F32),
        compiler_params=pltpu.CompilerParams(
            dimension_semantics=("arbitrary",), vmem_limit_bytes=VMEM_LIMIT, has_side_effects=True),
        name="moe_experts",
    )(tile_expert, info, dst, h3, w_gate, w_up, w_down)


def _combine_kernel(x_ref, wts_ref, y1_ref, y2_ref, o_ref):
    tc = x_ref.shape[0]
    eye = jnp.where(lax.broadcasted_iota(I32, (tc, tc), 0) == lax.broadcasted_iota(I32, (tc, tc), 1),
                    1.0, 0.0).astype(BF16)
    w_hi, w_lo = _split_bf16(wts_ref[...])
    w_col = (lax.dot_general(eye, w_hi, _NT, preferred_element_type=F32)
             + lax.dot_general(eye, w_lo, _NT, preferred_element_type=F32))
    o_ref[...] = (x_ref[...] + w_col[:, 0:1] * _load_token_major(y1_ref, tc)
                  + w_col[:, 1:2] * _load_token_major(y2_ref, tc))


def _combine(x2, wts, ys):
    n = x2.shape[0]
    tc = COMBINE_TILE
    return pl.pallas_call(
        _combine_kernel,
        grid=(n // tc,),
        in_specs=[
            pl.BlockSpec((tc, D_MODEL), lambda i: (i, 0)),
            pl.BlockSpec((8, tc), lambda i: (0, i)),
            pl.BlockSpec((tc * SUBLANES, 128), lambda i: (i, 0)),
            pl.BlockSpec((tc * SUBLANES, 128), lambda i: (i + n // tc, 0)),
        ],
        out_specs=pl.BlockSpec((tc, D_MODEL), lambda i: (i, 0)),
        out_shape=jax.ShapeDtypeStruct((n, D_MODEL), F32),
        compiler_params=pltpu.CompilerParams(dimension_semantics=("arbitrary",), vmem_limit_bytes=VMEM_LIMIT),
        name="moe_combine",
    )(x2, wts, ys, ys)


def _hier_moe(x2, g, w_rg, b_rg, w_re, b_re, w_gate, w_up, w_down, layer):
    n = x2.shape[0]
    max_tiles = (2 * n) // MOE_ROW_TILE + N_EXPERTS
    wr = jnp.concatenate([
        jnp.transpose(w_re, (0, 2, 1)).reshape(N_EXPERTS, D_MODEL),
        w_rg.T,
        jnp.zeros((ROUTER_ROWS - N_EXPERTS - N_GROUPS, D_MODEL), F32)], axis=0)
    br = jnp.concatenate([
        b_re.reshape(N_EXPERTS), b_rg, jnp.zeros((ROUTER_ROWS - N_EXPERTS - N_GROUPS,), F32)])[:, None]

    meta, wts, counts, h3 = _route(x2, g, wr, br)
    pos, tile_expert, info, tile_valid = _positions(counts, meta, max_tiles)
    dst = _inverse_map(pos.reshape(2 * n), tile_expert, tile_valid, max_tiles * MOE_ROW_TILE)
    ys = _experts(tile_expert, info, dst, h3, w_gate, w_up, w_down, layer)
    return _combine(x2, wts, ys)


def _rope_tables(seq):
    half = ROT_DIM // 2
    offset = jnp.arange(HEAD_W) % HEAD_DIM
    inv_freq = ROPE_THETA ** (-(2 * (offset % half)).astype(F32) / ROT_DIM)
    lane_freq = jnp.where(offset < ROT_DIM, inv_freq, 0.0)
    ang = jnp.arange(seq, dtype=F32)[:, None] * lane_freq[None, :]
    cos, sin = jnp.cos(ang), jnp.sin(ang)
    sinp = jnp.where((offset >= half) & (offset < ROT_DIM), sin, 0.0)
    sinm = jnp.where(offset < half, -sin, 0.0)
    return cos, sinp, sinm


def kernel(x, attn_norm, diff_w_in, diff_q_gain, diff_k_gain, diff_lambda_q1, diff_lambda_k1,
           diff_lambda_q2, diff_lambda_k2, diff_subln, diff_w_out,
           conv_norm, conv_w_in, conv_b_in, conv_dw_w, conv_dw_b, conv_ln_g, conv_ln_b,
           conv_w_out, conv_b_out,
           ffn_norm, router_group_w, router_group_b, router_expert_w, router_expert_b,
           moe_w_gate, moe_w_up, moe_w_down):
    batch, seq, d = x.shape
    assert d == D_MODEL and seq % ATTN_TILE == 0 and seq % CONV_TILE == 0
    n = batch * seq
    depth = ffn_norm.shape[0]
    cos_t, sinp_t, sinm_t = _rope_tables(seq)
    row = lambda v: v.reshape(1, -1)
    x2 = x.reshape(n, d)
    for i in range(depth):
        j = i // 2
        if i % 2 == 0:
            lambda_init = 0.8 - 0.6 * math.exp(-0.3 * i)
            dup = lambda v: jnp.concatenate([v, v]).reshape(1, HEAD_W)
            q2, k2, vt4 = _qkv_project(x2, row(attn_norm[j]), diff_w_in[j].astype(BF16),
                                       dup(diff_q_gain[j]), dup(diff_k_gain[j]), cos_t, sinp_t, sinm_t,
                                       batch, seq)
            lam_params = jnp.stack([diff_lambda_q1[j], diff_lambda_k1[j], diff_lambda_q2[j], diff_lambda_k2[j]])
            score_bound = (SCORE_BOUND_UNIT * jnp.max(jnp.abs(diff_q_gain[j]))
                           * jnp.max(jnp.abs(diff_k_gain[j]))).reshape(1)
            o3 = _diff_attention(score_bound, q2.reshape(batch, seq, d), k2.reshape(batch, seq, d), vt4,
                                 lam_params, row(diff_subln[j]), lambda_init)
            x2 = _proj_residual(x2, o3.reshape(n, d), diff_w_out[j].astype(BF16))
        else:
            dw_w = jnp.concatenate([conv_dw_w[j], jnp.zeros((CONV_HALO - CONV_WIDTH, d), F32)], axis=0)
            x2 = _conv_block(x2.reshape(batch, seq, d), row(conv_norm[j]), conv_w_in[j].astype(BF16),
                             row(conv_b_in[j]), dw_w, row(conv_dw_b[j]), row(conv_ln_g[j]), row(conv_ln_b[j]),
                             conv_w_out[j].astype(BF16), row(conv_b_out[j])).reshape(n, d)
        x2 = _hier_moe(x2, row(ffn_norm[i]), router_group_w[i], router_group_b[i],
                       router_expert_w[i], router_expert_b[i], moe_w_gate, moe_w_up, moe_w_down, i)
    return x2.reshape(batch, seq, d)
```

```python
import functools
import math

import jax
import jax.numpy as jnp
from jax import lax
from jax.experimental import pallas as pl
from jax.experimental.pallas import tpu as pltpu

F32 = jnp.float32
BF16 = jnp.bfloat16
I32 = jnp.int32

D_MODEL = 1024
HEAD_DIM = 64
HEAD_W = 2 * HEAD_DIM
N_HEADS = D_MODEL // HEAD_W
ROT_DIM = HEAD_DIM // 4
ROPE_THETA = 500000.0
CHUNK = 64
CONV_WIDTH = 31
SUBLANES = 8
CONV_HALO = 32
N_GROUPS = 4
EXPERTS_PER_GROUP = 8
N_EXPERTS = N_GROUPS * EXPERTS_PER_GROUP
D_EXPERT = D_MODEL // 2
EPS = 1e-6
ROUTER_ROWS = 64
LOG2_E = math.log2(math.e)
Q_SCALE = HEAD_DIM ** -0.5 * LOG2_E
SCORE_BOUND_UNIT = 1.02 * HEAD_DIM * Q_SCALE
UNSHIFTED_SOFTMAX_LIMIT = 50.0

ATTN_TILE = 1024
ATTN_GROUPS = (2, 1)
ROW_TILE = 1024
CONV_TILE = 1024
ROUTER_TILE = 1024
MOE_ROW_TILE = 256
COMBINE_TILE = 1024
VMEM_LIMIT = 48 * 1024 * 1024

_NT = (((1,), (1,)), ((), ()))


def _rmsnorm_rows(x, g):
    return x * lax.rsqrt(jnp.mean(x * x, axis=-1, keepdims=True) + EPS) * g


def _qkv_kernel(x_ref, g_ref, w_ref, qg_ref, kg_ref, cos_ref, sinp_ref, sinm_ref,
                q_ref, k_ref, vt_ref, *, col_tile):
    tm = x_ref.shape[0]
    h = _rmsnorm_rows(x_ref[...], g_ref[...]).astype(BF16)
    lane = lax.broadcasted_iota(I32, (tm, HEAD_W), 1)
    lo = lane < HEAD_DIM
    cosv, sinp, sinm = cos_ref[...], sinp_ref[...], sinm_ref[...]

    def norm_rope(y, gain, scale):
        y2 = y * y
        s_lo = jnp.sum(jnp.where(lo, y2, 0.0), axis=-1, keepdims=True)
        s_hi = jnp.sum(y2, axis=-1, keepdims=True) - s_lo
        r = jnp.where(lo, lax.rsqrt(s_lo * (1.0 / HEAD_DIM) + EPS), lax.rsqrt(s_hi * (1.0 / HEAD_DIM) + EPS))
        yn = y * r * gain
        half = ROT_DIM // 2
        out = yn * cosv + pltpu.roll(yn, half, 1) * sinp + pltpu.roll(yn, HEAD_W - half, 1) * sinm
        return out * scale

    for c in range(3 * D_MODEL // col_tile):
        y = jnp.dot(h, w_ref[:, c * col_tile:(c + 1) * col_tile], preferred_element_type=F32)
        for s in range(col_tile // HEAD_W):
            col = c * col_tile + s * HEAD_W
            ys = y[:, s * HEAD_W:(s + 1) * HEAD_W]
            if col < D_MODEL:
                q_ref[:, col:col + HEAD_W] = norm_rope(ys, qg_ref[...], Q_SCALE).astype(BF16)
            elif col < 2 * D_MODEL:
                k_ref[:, col - D_MODEL:col - D_MODEL + HEAD_W] = norm_rope(ys, kg_ref[...], 1.0).astype(BF16)
            else:
                vt_ref[col - 2 * D_MODEL:col - 2 * D_MODEL + HEAD_W, :] = ys.T.astype(BF16)


def _qkv_project(x2, g, w_bf, q_gain, k_gain, cos_t, sinp_t, sinm_t, batch, seq):
    n = x2.shape[0]
    tm = ATTN_TILE
    tiles_per_seq = seq // tm
    row = lambda i: (i, 0)
    fixed = lambda i: (0, 0)
    tab = lambda i: (i % tiles_per_seq, 0)
    return pl.pallas_call(
        functools.partial(_qkv_kernel, col_tile=512),
        grid=(n // tm,),
        in_specs=[
            pl.BlockSpec((tm, D_MODEL), row),
            pl.BlockSpec((1, D_MODEL), fixed),
            pl.BlockSpec((D_MODEL, 3 * D_MODEL), fixed),
            pl.BlockSpec((1, HEAD_W), fixed),
            pl.BlockSpec((1, HEAD_W), fixed),
            pl.BlockSpec((tm, HEAD_W), tab),
            pl.BlockSpec((tm, HEAD_W), tab),
            pl.BlockSpec((tm, HEAD_W), tab),
        ],
        out_specs=[
            pl.BlockSpec((tm, D_MODEL), row),
            pl.BlockSpec((tm, D_MODEL), row),
            pl.BlockSpec((None, None, D_MODEL, tm), lambda i: (i // tiles_per_seq, i % tiles_per_seq, 0, 0)),
        ],
        out_shape=[
            jax.ShapeDtypeStruct((n, D_MODEL), BF16),
            jax.ShapeDtypeStruct((n, D_MODEL), BF16),
            jax.ShapeDtypeStruct((batch, tiles_per_seq, D_MODEL, tm), BF16),
        ],
        compiler_params=pltpu.CompilerParams(dimension_semantics=("arbitrary",), vmem_limit_bytes=VMEM_LIMIT),
        name="qkv_project",
    )(x2, g, w_bf, q_gain, k_gain, cos_t, sinp_t, sinm_t)


def _attn_kernel(bound_ref, lam_ref, q_ref, k_ref, vt_ref, g_ref, o_ref, m_ref, l_ref, acc_ref, *, lambda_init):
    t = q_ref.shape[0]
    i = pl.program_id(2)
    q = q_ref[...]
    lane = lax.broadcasted_iota(I32, (t, HEAD_W), 1)
    zero = jnp.zeros_like(q)
    q_comp = (jnp.where(lane < HEAD_DIM, q, zero), jnp.where(lane >= HEAD_DIM, q, zero))

    l_ref[...] = jnp.zeros(l_ref.shape, F32)
    acc_ref[...] = jnp.zeros(acc_ref.shape, F32)

    def tiles(j):
        return k_ref[pl.ds(pl.multiple_of(j * t, t), t), :], vt_ref[j]

    def chunk_mask(n_keys=t, n_queries=t):
        key_chunk = lax.broadcasted_iota(I32, (n_keys, n_queries), 0) // CHUNK
        qry_chunk = lax.broadcasted_iota(I32, (n_keys, n_queries), 1) // CHUNK
        return key_chunk <= qry_chunk

    def unshifted_block(kt, vt, c, q0, masked):
        n_keys = kt.shape[0]
        s = lax.dot_general(kt, q_comp[c][q0:], _NT, preferred_element_type=F32)
        p = jnp.exp2(s)
        if masked:
            p = jnp.where(chunk_mask(n_keys, t - q0), p, 0.0)
        l_ref[c, :, q0:] += jnp.sum(p, axis=0, keepdims=True)
        acc_ref[c, :, q0:] += jnp.dot(vt, p.astype(BF16), preferred_element_type=F32)

    def unshifted_step(j, masked):
        kt, vt = tiles(j)
        half = t // 2
        for c in range(2):
            if masked:
                unshifted_block(kt[:half], vt[:, :half], c, 0, True)
                unshifted_block(kt[half:], vt[:, half:], c, half, True)
            else:
                unshifted_block(kt, vt, c, 0, False)

    def online_step(j, masked):
        kt, vt = tiles(j)
        for c in range(2):
            s = lax.dot_general(kt, q_comp[c], _NT, preferred_element_type=F32)
            if masked:
                s = jnp.where(chunk_mask(), s, -jnp.inf)
            m_old = m_ref[c]
            m_new = jnp.maximum(m_old, jnp.max(s, axis=0, keepdims=True))
            alpha = jnp.exp2(m_old - m_new)
            p = jnp.exp2(s - m_new)
            l_ref[c] = alpha * l_ref[c] + jnp.sum(p, axis=0, keepdims=True)
            acc_ref[c] = alpha * acc_ref[c] + jnp.dot(vt, p.astype(BF16), preferred_element_type=F32)
            m_ref[c] = m_new

    def sweep(step, groups):
        done = 0
        for group in groups:
            def trip(jj, carry, group=group, done=done):
                for u in range(group):
                    step(done + jj * group + u, False)
                return carry

            n_trips = (i - done) // group
            lax.fori_loop(0, n_trips, trip, 0)
            done = done + n_trips * group
        step(i, True)

    unshifted_ok = bound_ref[0] <= UNSHIFTED_SOFTMAX_LIMIT

    @pl.when(unshifted_ok)
    def _():
        sweep(unshifted_step, ATTN_GROUPS)

    @pl.when(jnp.logical_not(unshifted_ok))
    def _():
        m_ref[...] = jnp.full(m_ref.shape, -jnp.inf, F32)
        sweep(online_step, (1,))

    lam_p = lam_ref[...]
    lam = (jnp.exp(jnp.sum(lam_p[0:1] * lam_p[1:2], axis=-1, keepdims=True))
           - jnp.exp(jnp.sum(lam_p[2:3] * lam_p[3:4], axis=-1, keepdims=True)) + lambda_init)
    ot = acc_ref[0] * (1.0 / l_ref[0]) - lam * (acc_ref[1] * (1.0 / l_ref[1]))
    ot = ot * lax.rsqrt(jnp.mean(ot * ot, axis=0, keepdims=True) + EPS)
    o_ref[...] = (ot.T * g_ref[...] * (1.0 - lambda_init)).astype(o_ref.dtype)


def _diff_attention(score_bound, q3, k3, vt4, lam_params, subln_g, lambda_init):
    batch, seq, _ = q3.shape
    t = ATTN_TILE
    nq = seq // t
    return pl.pallas_call(
        functools.partial(_attn_kernel, lambda_init=lambda_init),
        grid_spec=pltpu.PrefetchScalarGridSpec(
            num_scalar_prefetch=1,
            grid=(batch, N_HEADS, nq),
            in_specs=[
                pl.BlockSpec((4, HEAD_DIM), lambda b, h, i, bound: (0, 0)),
                pl.BlockSpec((None, t, HEAD_W), lambda b, h, i, bound: (b, i, h)),
                pl.BlockSpec((None, seq, HEAD_W), lambda b, h, i, bound: (b, 0, h)),
                pl.BlockSpec((None, nq, HEAD_W, t), lambda b, h, i, bound: (b, 0, h, 0)),
                pl.BlockSpec((1, HEAD_W), lambda b, h, i, bound: (0, 0)),
            ],
            out_specs=pl.BlockSpec((None, t, HEAD_W), lambda b, h, i, bound: (b, i, h)),
            scratch_shapes=[
                pltpu.VMEM((2, 1, t), F32),
                pltpu.VMEM((2, 1, t), F32),
                pltpu.VMEM((2, HEAD_W, t), F32),
            ],
        ),
        out_shape=jax.ShapeDtypeStruct((batch, seq, D_MODEL), BF16),
        compiler_params=pltpu.CompilerParams(
            dimension_semantics=("arbitrary", "arbitrary", "arbitrary"), vmem_limit_bytes=VMEM_LIMIT),
        name="diff_attention",
    )(score_bound, lam_params, q3, k3, vt4, subln_g)


def _proj_residual_kernel(x_ref, y_ref, w_ref, o_ref):
    o_ref[...] = x_ref[...] + jnp.dot(y_ref[...], w_ref[...], preferred_element_type=F32)


def _proj_residual(x2, y2, w_bf):
    n = x2.shape[0]
    tm = ROW_TILE
    return pl.pallas_call(
        _proj_residual_kernel,
        grid=(n // tm,),
        in_specs=[
            pl.BlockSpec((tm, D_MODEL), lambda i: (i, 0)),
            pl.BlockSpec((tm, D_MODEL), lambda i: (i, 0)),
            pl.BlockSpec((D_MODEL, D_MODEL), lambda i: (0, 0)),
        ],
        out_specs=pl.BlockSpec((tm, D_MODEL), lambda i: (i, 0)),
        out_shape=jax.ShapeDtypeStruct((n, D_MODEL), F32),
        compiler_params=pltpu.CompilerParams(dimension_semantics=("arbitrary",), vmem_limit_bytes=VMEM_LIMIT),
        name="attn_out_proj",
    )(x2, y2, w_bf)


def _conv_kernel(x_ref, g_ref, win_ref, bin_ref, dww_ref, dwb_ref, lng_ref, lnb_ref, wout_ref, bout_ref,
                 o_ref, u_ref):
    tm = x_ref.shape[0]
    x = x_ref[...]
    h = _rmsnorm_rows(x, g_ref[...]).astype(BF16)
    a = jnp.dot(h, win_ref[:, :D_MODEL], preferred_element_type=F32) + bin_ref[:, :D_MODEL]
    gate = jnp.dot(h, win_ref[:, D_MODEL:], preferred_element_type=F32) + bin_ref[:, D_MODEL:]
    u = a * jax.nn.sigmoid(gate)

    @pl.when(pl.program_id(1) == 0)
    def _():
        u_ref[0:CONV_HALO, :] = jnp.zeros((CONV_HALO, D_MODEL), F32)

    u_ref[CONV_HALO:CONV_HALO + tm, :] = u
    base = CONV_HALO - (CONV_WIDTH - 1)
    window = u_ref[...]
    n_rows = tm + CONV_HALO
    acc = jnp.zeros((tm, D_MODEL), F32) + dwb_ref[...]
    for phase in range(SUBLANES):
        rolled = window if phase == 0 else pltpu.roll(window, n_rows - phase, 0)
        for row0 in range(0, CONV_HALO + 1, SUBLANES):
            w = row0 + phase - base
            if 0 <= w < CONV_WIDTH:
                acc = acc + dww_ref[w:w + 1, :] * rolled[row0:row0 + tm, :]
    u_ref[0:CONV_HALO, :] = u_ref[tm:tm + CONV_HALO, :]

    mu = jnp.mean(acc, axis=-1, keepdims=True)
    xc = acc - mu
    var = jnp.mean(xc * xc, axis=-1, keepdims=True)
    v = xc * lax.rsqrt(var + EPS) * lng_ref[...] + lnb_ref[...]
    v = (v * jax.nn.sigmoid(v)).astype(BF16)
    o_ref[...] = x + jnp.dot(v, wout_ref[...], preferred_element_type=F32) + bout_ref[...]


def _conv_block(x3, g, win_bf, b_in, dw_w, dw_b, ln_g, ln_b, wout_bf, b_out):
    batch, seq, _ = x3.shape
    tm = CONV_TILE
    row = lambda b, s: (b, s, 0)
    fixed = lambda b, s: (0, 0)
    vec = pl.BlockSpec((1, D_MODEL), fixed)
    return pl.pallas_call(
        _conv_kernel,
        grid=(batch, seq // tm),
        in_specs=[
            pl.BlockSpec((None, tm, D_MODEL), row),
            vec,
            pl.BlockSpec((D_MODEL, 2 * D_MODEL), fixed),
            pl.BlockSpec((1, 2 * D_MODEL), fixed),
            pl.BlockSpec((CONV_HALO, D_MODEL), fixed),
            vec, vec, vec,
            pl.BlockSpec((D_MODEL, D_MODEL), fixed),
            vec,
        ],
        out_specs=pl.BlockSpec((None, tm, D_MODEL), row),
        out_shape=jax.ShapeDtypeStruct((batch, seq, D_MODEL), F32),
        scratch_shapes=[pltpu.VMEM((tm + CONV_HALO, D_MODEL), F32)],
        compiler_params=pltpu.CompilerParams(
            dimension_semantics=("arbitrary", "arbitrary"), vmem_limit_bytes=VMEM_LIMIT),
        name="conformer_conv",
    )(x3, g, win_bf, b_in, dw_w, dw_b, ln_g, ln_b, wout_bf, b_out)


def _store_token_major(ref, v, lead=()):
    t = v.shape[0]
    for s in range(SUBLANES):
        ref[lead + (pl.ds(s, t, stride=SUBLANES), slice(None))] = v[:, s * 128:(s + 1) * 128]


def _load_token_major(ref, t, lead=()):
    return jnp.concatenate(
        [ref[lead + (pl.ds(s, t, stride=SUBLANES), slice(None))] for s in range(SUBLANES)], axis=1)


def _split_bf16(v):
    hi = v.astype(BF16)
    lo = (v - hi.astype(F32)).astype(BF16)
    return hi, lo


def _router_kernel(x_ref, g_ref, wr_ref, br_ref, meta_ref, wts_ref, cnt_ref, h3_ref, carry_ref, tri_ref):
    t = x_ref.shape[0]

    @pl.when(pl.program_id(0) == 0)
    def _():
        carry_ref[...] = jnp.zeros(carry_ref.shape, F32)
        before = lax.broadcasted_iota(I32, (t, t), 0) < lax.broadcasted_iota(I32, (t, t), 1)
        tri_ref[...] = jnp.where(before, 1.0, 0.0).astype(BF16)

    h = _rmsnorm_rows(x_ref[...], g_ref[...])
    _store_token_major(h3_ref, h)
    h_hi, h_lo = _split_bf16(h)
    w_hi, w_lo = _split_bf16(wr_ref[...])
    logits = (lax.dot_general(w_hi, h_hi, _NT, preferred_element_type=F32)
              + lax.dot_general(w_hi, h_lo, _NT, preferred_element_type=F32)
              + lax.dot_general(w_lo, h_hi, _NT, preferred_element_type=F32)) + br_ref[...]

    gl = logits[N_EXPERTS:N_EXPERTS + N_GROUPS]
    gmax = jnp.max(gl, axis=0, keepdims=True)
    g_p = 1.0 / jnp.sum(jnp.exp(gl - gmax), axis=0, keepdims=True)
    row_g = lax.broadcasted_iota(I32, gl.shape, 0)
    g_idx = jnp.min(jnp.where(gl == gmax, row_g, N_GROUPS), axis=0, keepdims=True)

    e_sel = jnp.zeros((EXPERTS_PER_GROUP, t), F32)
    for grp in range(N_GROUPS):
        e_sel = e_sel + jnp.where(g_idx == grp, logits[grp * EXPERTS_PER_GROUP:(grp + 1) * EXPERTS_PER_GROUP], 0.0)
    row_e = lax.broadcasted_iota(I32, e_sel.shape, 0)
    m1 = jnp.max(e_sel, axis=0, keepdims=True)
    i1 = jnp.min(jnp.where(e_sel == m1, row_e, EXPERTS_PER_GROUP), axis=0, keepdims=True)
    rest = jnp.where(row_e == i1, -jnp.inf, e_sel)
    m2 = jnp.max(rest, axis=0, keepdims=True)
    i2 = jnp.min(jnp.where(rest == m2, row_e, EXPERTS_PER_GROUP), axis=0, keepdims=True)
    r = jnp.exp(m2 - m1)
    p1 = 1.0 / (1.0 + r)
    f1 = g_idx * EXPERTS_PER_GROUP + i1
    f2 = g_idx * EXPERTS_PER_GROUP + i2

    row_x = lax.broadcasted_iota(I32, (N_EXPERTS, t), 0)
    oh1 = row_x == f1
    oh2 = row_x == f2
    onehot = jnp.where(oh1 | oh2, 1.0, 0.0)
    prior = jnp.dot(onehot.astype(BF16), tri_ref[...], preferred_element_type=F32) + carry_ref[:, 0:1]
    rank1 = jnp.sum(jnp.where(oh1, prior, 0.0), axis=0, keepdims=True)
    rank2 = jnp.sum(jnp.where(oh2, prior, 0.0), axis=0, keepdims=True)
    total = carry_ref[...] + jnp.sum(onehot, axis=1, keepdims=True)
    carry_ref[...] = total
    cnt_ref[...] = total.astype(I32)

    zi = jnp.zeros((4, t), I32)
    meta_ref[...] = jnp.concatenate([f1, f2, rank1.astype(I32), rank2.astype(I32), zi], axis=0)
    wts_ref[...] = jnp.concatenate([g_p * p1, g_p * (r * p1), jnp.zeros((6, t), F32)], axis=0)


def _route(x2, g, wr, br):
    n = x2.shape[0]
    t = ROUTER_TILE
    return pl.pallas_call(
        _router_kernel,
        grid=(n // t,),
        in_specs=[
            pl.BlockSpec((t, D_MODEL), lambda i: (i, 0)),
            pl.BlockSpec((1, D_MODEL), lambda i: (0, 0)),
            pl.BlockSpec((ROUTER_ROWS, D_MODEL), lambda i: (0, 0)),
            pl.BlockSpec((ROUTER_ROWS, 1), lambda i: (0, 0)),
        ],
        out_specs=[
            pl.BlockSpec((8, t), lambda i: (0, i)),
            pl.BlockSpec((8, t), lambda i: (0, i)),
            pl.BlockSpec((N_EXPERTS, 128), lambda i: (0, 0)),
            pl.BlockSpec((t * SUBLANES, 128), lambda i: (i, 0)),
        ],
        out_shape=[
            jax.ShapeDtypeStruct((8, n), I32),
            jax.ShapeDtypeStruct((8, n), F32),
            jax.ShapeDtypeStruct((N_EXPERTS, 128), I32),
            jax.ShapeDtypeStruct((n * SUBLANES, 128), F32),
        ],
        scratch_shapes=[pltpu.VMEM((N_EXPERTS, 128), F32), pltpu.VMEM((t, t), BF16)],
        compiler_params=pltpu.CompilerParams(dimension_semantics=("arbitrary",), vmem_limit_bytes=VMEM_LIMIT),
        name="moe_router",
    )(x2, g, wr, br)


def _positions_kernel(cnt_ref, meta_ref, pos_ref, tile_expert_ref, info_ref, tile_valid_ref, off_ref,
                      *, max_tiles):
    def per_expert(e, tiles_before):
        count = cnt_ref[e, 0]
        n_tiles = (count + (MOE_ROW_TILE - 1)) // MOE_ROW_TILE
        off_ref[e] = tiles_before * MOE_ROW_TILE

        def fill(k, c):
            tile_expert_ref[tiles_before + k] = e
            tile_valid_ref[tiles_before + k] = jnp.minimum(count - k * MOE_ROW_TILE, MOE_ROW_TILE)
            return c

        lax.fori_loop(0, n_tiles, fill, 0)
        return tiles_before + n_tiles

    used = lax.fori_loop(0, N_EXPERTS, per_expert, jnp.int32(0))
    info_ref[0] = used
    last_expert = tile_expert_ref[used - 1]

    def fill_tail(k, c):
        tile_expert_ref[k] = last_expert
        tile_valid_ref[k] = 0
        return c

    lax.fori_loop(used, max_tiles, fill_tail, 0)
    tile_expert_ref[max_tiles] = last_expert

    ids = meta_ref[0:2, :]
    pos = meta_ref[2:4, :]
    for e in range(N_EXPERTS):
        pos = pos + jnp.where(ids == e, off_ref[e], 0)
    pos_ref[...] = pos


def _positions(counts, meta, max_tiles):
    n = meta.shape[1]
    smem = pl.BlockSpec(memory_space=pltpu.SMEM)
    return pl.pallas_call(
        functools.partial(_positions_kernel, max_tiles=max_tiles),
        in_specs=[smem, pl.BlockSpec((8, n), lambda: (0, 0))],
        out_specs=[pl.BlockSpec((2, n), lambda: (0, 0)), smem, smem, smem],
        out_shape=[
            jax.ShapeDtypeStruct((2, n), I32),
            jax.ShapeDtypeStruct((max_tiles + 1,), I32),
            jax.ShapeDtypeStruct((1,), I32),
            jax.ShapeDtypeStruct((max_tiles,), I32),
        ],
        scratch_shapes=[pltpu.SMEM((N_EXPERTS,), I32)],
        name="moe_positions",
    )(counts, meta)


def _inverse_kernel(pos_ref, tile_expert_ref, tile_valid_ref, dst_ref, *, n_tokens):
    tm = MOE_ROW_TILE

    def pad_tile(t, c):
        pad_base = 2 * n_tokens + tile_expert_ref[t] * tm

        def pad_row(r, c2):
            dst_ref[t * tm + r] = pad_base + r
            return c2

        lax.fori_loop(tile_valid_ref[t], tm, pad_row, 0)
        return c

    lax.fori_loop(0, dst_ref.shape[0] // tm, pad_tile, 0)

    def token(n, c):
        dst_ref[pos_ref[n]] = n
        dst_ref[pos_ref[n_tokens + n]] = n_tokens + n
        return c

    lax.fori_loop(0, n_tokens, token, 0, unroll=128)


def _inverse_map(pos_flat, tile_expert, tile_valid, n_rows):
    smem = pl.BlockSpec(memory_space=pltpu.SMEM)
    return pl.pallas_call(
        functools.partial(_inverse_kernel, n_tokens=pos_flat.shape[0] // 2),
        in_specs=[smem, smem, smem],
        out_specs=smem,
        out_shape=jax.ShapeDtypeStruct((n_rows,), I32),
        name="moe_inverse_map",
    )(pos_flat, tile_expert, tile_valid)


def _expert_kernel(tile_expert_ref, info_ref, dst_ref, x_hbm, wg_ref, wu_ref, wd_ref,
                   out_hbm, xbuf, ybuf, wg_bf, wu_bf, wd_bf, gsem, ssem, *, n_tokens):
    j = pl.program_id(0)
    used = info_ref[0]
    tm = MOE_ROW_TILE

    def token_rows(ref, token):
        return ref.at[pl.ds(pl.multiple_of(token * SUBLANES, SUBLANES), SUBLANES)]


    def start_gather(tile, s):
        for r in range(tm):
            token = dst_ref[tile * tm + r] & (n_tokens - 1)
            pltpu.make_async_copy(token_rows(x_hbm, token), xbuf.at[s, pl.ds(r * SUBLANES, SUBLANES)],
                                  gsem.at[s]).start(priority=r % 2)

    def wait_gather(s):
        pltpu.make_async_copy(x_hbm.at[pl.ds(0, tm * SUBLANES)], xbuf.at[s], gsem.at[s]).wait()

    def start_scatter(tile, s):
        for r in range(tm):
            pltpu.make_async_copy(ybuf.at[s, pl.ds(r * SUBLANES, SUBLANES)],
                                  token_rows(out_hbm, dst_ref[tile * tm + r]),
                                  ssem.at[s]).start(priority=(r + 1) % 2)

    def wait_scatter(s):
        pltpu.make_async_copy(ybuf.at[s], out_hbm.at[pl.ds(0, tm * SUBLANES)], ssem.at[s]).wait()

    def cast_weights():
        wg_bf[...] = wg_ref[...].astype(BF16)
        wu_bf[...] = wu_ref[...].astype(BF16)
        wd_bf[...] = wd_ref[...].astype(BF16)

    def mlp(s):
        x = _load_token_major(xbuf, tm, (s,)).astype(BF16)
        gate = jnp.dot(x, wg_bf[...], preferred_element_type=F32)
        up = jnp.dot(x, wu_bf[...], preferred_element_type=F32)
        hid = (gate * jax.nn.sigmoid(gate) * up).astype(BF16)
        _store_token_major(ybuf, jnp.dot(hid, wd_bf[...], preferred_element_type=F32), (s,))

    @pl.when(j == 0)
    def _():
        def pad_rows_copy(e):
            return pltpu.make_async_copy(
                ybuf.at[1], out_hbm.at[pl.ds((2 * n_tokens + e * tm) * SUBLANES, tm * SUBLANES)], ssem.at[1])

        ybuf[1] = jnp.zeros(ybuf.shape[1:], F32)
        for e in range(N_EXPERTS):
            pad_rows_copy(e).start()
        start_gather(0, 0)
        for e in range(N_EXPERTS):
            pad_rows_copy(e).wait()
        wait_gather(0)
        cast_weights()
        start_gather(jnp.minimum(1, used - 1), 1)
        mlp(0)

    for s in range(2):
        @pl.when((j >= 1) & (j < used) & (lax.rem(j, 2) == s))
        def _():
            wait_gather(s)

            @pl.when(j >= 2)
            def _():
                wait_scatter(s)

            @pl.when(tile_expert_ref[j] != tile_expert_ref[j - 1])
            def _():
                cast_weights()

            start_gather(jnp.minimum(j + 1, used - 1), 1 - s)
            start_scatter(j - 1, 1 - s)
            mlp(s)

        @pl.when((j == used) & (lax.rem(j, 2) == s))
        def _():
            wait_gather(s)

            @pl.when(j >= 2)
            def _():
                wait_scatter(s)

            start_scatter(j - 1, 1 - s)
            wait_scatter(1 - s)


def _experts(tile_expert, info, dst, h3, w_gate, w_up, w_down, layer):
    n = h3.shape[0] // SUBLANES
    assert n & (n - 1) == 0
    tm = MOE_ROW_TILE
    wsel = lambda j, te, info, dst: (layer, te[j], 0, 0)
    return pl.pallas_call(
        functools.partial(_expert_kernel, n_tokens=n),
        grid_spec=pltpu.PrefetchScalarGridSpec(
            num_scalar_prefetch=3,
            grid=(dst.shape[0] // tm + 1,),
            in_specs=[
                pl.BlockSpec(memory_space=pl.ANY),
                pl.BlockSpec((None, None, D_MODEL, D_EXPERT), wsel),
                pl.BlockSpec((None, None, D_MODEL, D_EXPERT), wsel),
                pl.BlockSpec((None, None, D_EXPERT, D_MODEL), wsel),
            ],
            out_specs=pl.BlockSpec(memory_space=pl.ANY),
            scratch_shapes=[
                pltpu.VMEM((2, tm * SUBLANES, 128), F32),
                pltpu.VMEM((2, tm * SUBLANES, 128), F32),
                pltpu.VMEM((D_MODEL, D_EXPERT), BF16),
                pltpu.VMEM((D_MODEL, D_EXPERT), BF16),
                pltpu.VMEM((D_EXPERT, D_MODEL), BF16),
                pltpu.SemaphoreType.DMA((2,)),
                pltpu.SemaphoreType.DMA((2,)),
            ],
        ),
        out_shape=jax.ShapeDtypeStruct(((2 * n + N_EXPERTS * tm) * SUBLANES, 128), F32),
        compiler_params=pltpu.CompilerParams(
            dimension_semantics=("arbitrary",), vmem_limit_bytes=VMEM_LIMIT, has_side_effects=True),
        name="moe_experts",
    )(tile_expert, info, dst, h3, w_gate, w_up, w_down)


def _combine_kernel(x_ref, wts_ref, y1_ref, y2_ref, o_ref):
    tc = x_ref.shape[0]
    eye = jnp.where(lax.broadcasted_iota(I32, (tc, tc), 0) == lax.broadcasted_iota(I32, (tc, tc), 1),
                    1.0, 0.0).astype(BF16)
    w_hi, w_lo = _split_bf16(wts_ref[...])
    w_col = (lax.dot_general(eye, w_hi, _NT, preferred_element_type=F32)
             + lax.dot_general(eye, w_lo, _NT, preferred_element_type=F32))
    o_ref[...] = (x_ref[...] + w_col[:, 0:1] * _load_token_major(y1_ref, tc)
                  + w_col[:, 1:2] * _load_token_major(y2_ref, tc))


def _combine(x2, wts, ys):
    n = x2.shape[0]
    tc = COMBINE_TILE
    return pl.pallas_call(
        _combine_kernel,
        grid=(n // tc,),
        in_specs=[
            pl.BlockSpec((tc, D_MODEL), lambda i: (i, 0)),
            pl.BlockSpec((8, tc), lambda i: (0, i)),
            pl.BlockSpec((tc * SUBLANES, 128), lambda i: (i, 0)),
            pl.BlockSpec((tc * SUBLANES, 128), lambda i: (i + n // tc, 0)),
        ],
        out_specs=pl.BlockSpec((tc, D_MODEL), lambda i: (i, 0)),
        out_shape=jax.ShapeDtypeStruct((n, D_MODEL), F32),
        compiler_params=pltpu.CompilerParams(dimension_semantics=("arbitrary",), vmem_limit_bytes=VMEM_LIMIT),
        name="moe_combine",
    )(x2, wts, ys, ys)


def _hier_moe(x2, g, w_rg, b_rg, w_re, b_re, w_gate, w_up, w_down, layer):
    n = x2.shape[0]
    max_tiles = (2 * n) // MOE_ROW_TILE + N_EXPERTS
    wr = jnp.concatenate([
        jnp.transpose(w_re, (0, 2, 1)).reshape(N_EXPERTS, D_MODEL),
        w_rg.T,
        jnp.zeros((ROUTER_ROWS - N_EXPERTS - N_GROUPS, D_MODEL), F32)], axis=0)
    br = jnp.concatenate([
        b_re.reshape(N_EXPERTS), b_rg, jnp.zeros((ROUTER_ROWS - N_EXPERTS - N_GROUPS,), F32)])[:, None]

    meta, wts, counts, h3 = _route(x2, g, wr, br)
    pos, tile_expert, info, tile_valid = _positions(counts, meta, max_tiles)
    dst = _inverse_map(pos.reshape(2 * n), tile_expert, tile_valid, max_tiles * MOE_ROW_TILE)
    ys = _experts(tile_expert, info, dst, h3, w_gate, w_up, w_down, layer)
    return _combine(x2, wts, ys)


def _rope_tables(seq):
    half = ROT_DIM // 2
    offset = jnp.arange(HEAD_W) % HEAD_DIM
    inv_freq = ROPE_THETA ** (-(2 * (offset % half)).astype(F32) / ROT_DIM)
    lane_freq = jnp.where(offset < ROT_DIM, inv_freq, 0.0)
    ang = jnp.arange(seq, dtype=F32)[:, None] * lane_freq[None, :]
    cos, sin = jnp.cos(ang), jnp.sin(ang)
    sinp = jnp.where((offset >= half) & (offset < ROT_DIM), sin, 0.0)
    sinm = jnp.where(offset < half, -sin, 0.0)
    return cos, sinp, sinm


def kernel(x, attn_norm, diff_w_in, diff_q_gain, diff_k_gain, diff_lambda_q1, diff_lambda_k1,
           diff_lambda_q2, diff_lambda_k2, diff_subln, diff_w_out,
           conv_norm, conv_w_in, conv_b_in, conv_dw_w, conv_dw_b, conv_ln_g, conv_ln_b,
           conv_w_out, conv_b_out,
           ffn_norm, router_group_w, router_group_b, router_expert_w, router_expert_b,
           moe_w_gate, moe_w_up, moe_w_down):
    batch, seq, d = x.shape
    assert d == D_MODEL and seq % ATTN_TILE == 0 and seq % CONV_TILE == 0
    n = batch * seq
    depth = ffn_norm.shape[0]
    cos_t, sinp_t, sinm_t = _rope_tables(seq)
    row = lambda v: v.reshape(1, -1)
    x2 = x.reshape(n, d)
    for i in range(depth):
        j = i // 2
        if i % 2 == 0:
            lambda_init = 0.8 - 0.6 * math.exp(-0.3 * i)
            dup = lambda v: jnp.concatenate([v, v]).reshape(1, HEAD_W)
            q2, k2, vt4 = _qkv_project(x2, row(attn_norm[j]), diff_w_in[j].astype(BF16),
                                       dup(diff_q_gain[j]), dup(diff_k_gain[j]), cos_t, sinp_t, sinm_t,
                                       batch, seq)
            lam_params = jnp.stack([diff_lambda_q1[j], diff_lambda_k1[j], diff_lambda_q2[j], diff_lambda_k2[j]])
            score_bound = (SCORE_BOUND_UNIT * jnp.max(jnp.abs(diff_q_gain[j]))
                           * jnp.max(jnp.abs(diff_k_gain[j]))).reshape(1)
            o3 = _diff_attention(score_bound, q2.reshape(batch, seq, d), k2.reshape(batch, seq, d), vt4,
                                 lam_params, row(diff_subln[j]), lambda_init)
            x2 = _proj_residual(x2, o3.reshape(n, d), diff_w_out[j].astype(BF16))
        else:
            dw_w = jnp.concatenate([conv_dw_w[j], jnp.zeros((CONV_HALO - CONV_WIDTH, d), F32)], axis=0)
            x2 = _conv_block(x2.reshape(batch, seq, d), row(conv_norm[j]), conv_w_in[j].astype(BF16),
                             row(conv_b_in[j]), dw_w, row(conv_dw_b[j]), row(conv_ln_g[j]), row(conv_ln_b[j]),
                             conv_w_out[j].astype(BF16), row(conv_b_out[j])).reshape(n, d)
        x2 = _hier_moe(x2, row(ffn_norm[i]), router_group_w[i], router_group_b[i],
                       router_expert_w[i], router_expert_b[i], moe_w_gate, moe_w_up, moe_w_down, i)
    return x2.reshape(batch, seq, d)
```

```python
import functools
import math

import jax
import jax.numpy as jnp
from jax import lax
from jax.experimental import pallas as pl
from jax.experimental.pallas import tpu as pltpu

F32 = jnp.float32
BF16 = jnp.bfloat16
I32 = jnp.int32

D_MODEL = 1024
HEAD_DIM = 64
HEAD_W = 2 * HEAD_DIM
N_HEADS = D_MODEL // HEAD_W
ROT_DIM = HEAD_DIM // 4
ROPE_THETA = 500000.0
CHUNK = 64
CONV_WIDTH = 31
SUBLANES = 8
CONV_HALO = 32
N_GROUPS = 4
EXPERTS_PER_GROUP = 8
N_EXPERTS = N_GROUPS * EXPERTS_PER_GROUP
D_EXPERT = D_MODEL // 2
EPS = 1e-6
ROUTER_ROWS = 64
LOG2_E = math.log2(math.e)
Q_SCALE = HEAD_DIM ** -0.5 * LOG2_E
SCORE_BOUND_UNIT = 1.02 * HEAD_DIM * Q_SCALE
UNSHIFTED_SOFTMAX_LIMIT = 50.0

ATTN_TILE = 1024
ATTN_GROUPS = (2, 1)
ROW_TILE = 1024
CONV_TILE = 1024
ROUTER_TILE = 1024
MOE_ROW_TILE = 256
COMBINE_TILE = 1024
VMEM_LIMIT = 48 * 1024 * 1024

_NT = (((1,), (1,)), ((), ()))


def _rmsnorm_rows(x, g):
    return x * lax.rsqrt(jnp.mean(x * x, axis=-1, keepdims=True) + EPS) * g


def _qkv_kernel(x_ref, g_ref, w_ref, qc_ref, qp_ref, qm_ref, kc_ref, kp_ref, km_ref,
                q_ref, k_ref, vt_ref, *, col_tile):
    tm = x_ref.shape[0]
    h = _rmsnorm_rows(x_ref[...], g_ref[...]).astype(BF16)
    lane = lax.broadcasted_iota(I32, (tm, HEAD_W), 1)
    lo = lane < HEAD_DIM
    q_tabs = (qc_ref[...], qp_ref[...], qm_ref[...])
    k_tabs = (kc_ref[...], kp_ref[...], km_ref[...])

    def norm_rope(y, tabs):
        y2 = y * y
        s_lo = jnp.sum(jnp.where(lo, y2, 0.0), axis=-1, keepdims=True)
        s_hi = jnp.sum(y2, axis=-1, keepdims=True) - s_lo
        r = jnp.where(lo, lax.rsqrt(s_lo * (1.0 / HEAD_DIM) + EPS), lax.rsqrt(s_hi * (1.0 / HEAD_DIM) + EPS))
        yn = y * r
        half = ROT_DIM // 2
        return yn * tabs[0] + pltpu.roll(yn, half, 1) * tabs[1] + pltpu.roll(yn, HEAD_W - half, 1) * tabs[2]

    for c in range(3 * D_MODEL // col_tile):
        y = jnp.dot(h, w_ref[:, c * col_tile:(c + 1) * col_tile], preferred_element_type=F32)
        for s in range(col_tile // HEAD_W):
            col = c * col_tile + s * HEAD_W
            ys = y[:, s * HEAD_W:(s + 1) * HEAD_W]
            if col < D_MODEL:
                q_ref[:, col:col + HEAD_W] = norm_rope(ys, q_tabs).astype(BF16)
            elif col < 2 * D_MODEL:
                k_ref[:, col - D_MODEL:col - D_MODEL + HEAD_W] = norm_rope(ys, k_tabs).astype(BF16)
            else:
                vt_ref[col - 2 * D_MODEL:col - 2 * D_MODEL + HEAD_W, :] = ys.T.astype(BF16)


def _qkv_project(x2, g, w_bf, q_tabs, k_tabs, batch, seq):
    n = x2.shape[0]
    tm = ATTN_TILE
    tiles_per_seq = seq // tm
    row = lambda i: (i, 0)
    fixed = lambda i: (0, 0)
    tab = lambda i: (i % tiles_per_seq, 0)
    return pl.pallas_call(
        functools.partial(_qkv_kernel, col_tile=512),
        grid=(n // tm,),
        in_specs=[
            pl.BlockSpec((tm, D_MODEL), row),
            pl.BlockSpec((1, D_MODEL), fixed),
            pl.BlockSpec((D_MODEL, 3 * D_MODEL), fixed),
        ] + [pl.BlockSpec((tm, HEAD_W), tab)] * 6,
        out_specs=[
            pl.BlockSpec((tm, D_MODEL), row),
            pl.BlockSpec((tm, D_MODEL), row),
            pl.BlockSpec((None, None, D_MODEL, tm), lambda i: (i // tiles_per_seq, i % tiles_per_seq, 0, 0)),
        ],
        out_shape=[
            jax.ShapeDtypeStruct((n, D_MODEL), BF16),
            jax.ShapeDtypeStruct((n, D_MODEL), BF16),
            jax.ShapeDtypeStruct((batch, tiles_per_seq, D_MODEL, tm), BF16),
        ],
        compiler_params=pltpu.CompilerParams(dimension_semantics=("arbitrary",), vmem_limit_bytes=VMEM_LIMIT),
        name="qkv_project",
    )(x2, g, w_bf, *q_tabs, *k_tabs)


def _attn_kernel(bound_ref, lam_ref, q_ref, k_ref, vt_ref, g_ref, o_ref, m_ref, l_ref, acc_ref, *, lambda_init):
    t = q_ref.shape[0]
    i = pl.program_id(2)
    q = q_ref[...]
    lane = lax.broadcasted_iota(I32, (t, HEAD_W), 1)
    zero = jnp.zeros_like(q)
    q_comp = (jnp.where(lane < HEAD_DIM, q, zero), jnp.where(lane >= HEAD_DIM, q, zero))

    l_ref[...] = jnp.zeros(l_ref.shape, F32)
    acc_ref[...] = jnp.zeros(acc_ref.shape, F32)

    def tiles(j):
        return k_ref[pl.ds(pl.multiple_of(j * t, t), t), :], vt_ref[j]

    def chunk_mask(n_keys=t, n_queries=t):
        key_chunk = lax.broadcasted_iota(I32, (n_keys, n_queries), 0) // CHUNK
        qry_chunk = lax.broadcasted_iota(I32, (n_keys, n_queries), 1) // CHUNK
        return key_chunk <= qry_chunk

    def unshifted_block(kt, vt, c, q0, masked):
        n_keys = kt.shape[0]
        s = lax.dot_general(kt, q_comp[c][q0:], _NT, preferred_element_type=F32)
        p = jnp.exp2(s)
        if masked:
            p = jnp.where(chunk_mask(n_keys, t - q0), p, 0.0)
        l_ref[c, :, q0:] += jnp.sum(p, axis=0, keepdims=True)
        acc_ref[c, :, q0:] += jnp.dot(vt, p.astype(BF16), preferred_element_type=F32)

    def unshifted_step(j, masked):
        kt, vt = tiles(j)
        half = t // 2
        for c in range(2):
            if masked:
                unshifted_block(kt[:half], vt[:, :half], c, 0, True)
                unshifted_block(kt[half:], vt[:, half:], c, half, True)
            else:
                unshifted_block(kt, vt, c, 0, False)

    def online_step(j, masked):
        kt, vt = tiles(j)
        for c in range(2):
            s = lax.dot_general(kt, q_comp[c], _NT, preferred_element_type=F32)
            if masked:
                s = jnp.where(chunk_mask(), s, -jnp.inf)
            m_old = m_ref[c]
            m_new = jnp.maximum(m_old, jnp.max(s, axis=0, keepdims=True))
            alpha = jnp.exp2(m_old - m_new)
            p = jnp.exp2(s - m_new)
            l_ref[c] = alpha * l_ref[c] + jnp.sum(p, axis=0, keepdims=True)
            acc_ref[c] = alpha * acc_ref[c] + jnp.dot(vt, p.astype(BF16), preferred_element_type=F32)
            m_ref[c] = m_new

    def sweep(step, groups):
        done = 0
        for group in groups:
            def trip(jj, carry, group=group, done=done):
                for u in range(group):
                    step(done + jj * group + u, False)
                return carry

            n_trips = (i - done) // group
            lax.fori_loop(0, n_trips, trip, 0)
            done = done + n_trips * group
        step(i, True)

    unshifted_ok = bound_ref[0] <= UNSHIFTED_SOFTMAX_LIMIT

    @pl.when(unshifted_ok)
    def _():
        sweep(unshifted_step, ATTN_GROUPS)

    @pl.when(jnp.logical_not(unshifted_ok))
    def _():
        m_ref[...] = jnp.full(m_ref.shape, -jnp.inf, F32)
        sweep(online_step, (1,))

    lam_p = lam_ref[...]
    lam = (jnp.exp(jnp.sum(lam_p[0:1] * lam_p[1:2], axis=-1, keepdims=True))
           - jnp.exp(jnp.sum(lam_p[2:3] * lam_p[3:4], axis=-1, keepdims=True)) + lambda_init)
    ot = acc_ref[0] * (1.0 / l_ref[0]) - lam * (acc_ref[1] * (1.0 / l_ref[1]))
    ot = ot * lax.rsqrt(jnp.mean(ot * ot, axis=0, keepdims=True) + EPS)
    o_ref[...] = (ot.T * g_ref[...] * (1.0 - lambda_init)).astype(o_ref.dtype)


def _diff_attention(score_bound, q3, k3, vt4, lam_params, subln_g, lambda_init):
    batch, seq, _ = q3.shape
    t = ATTN_TILE
    nq = seq // t
    return pl.pallas_call(
        functools.partial(_attn_kernel, lambda_init=lambda_init),
        grid_spec=pltpu.PrefetchScalarGridSpec(
            num_scalar_prefetch=1,
            grid=(batch, N_HEADS, nq),
            in_specs=[
                pl.BlockSpec((4, HEAD_DIM), lambda b, h, i, bound: (0, 0)),
                pl.BlockSpec((None, t, HEAD_W), lambda b, h, i, bound: (b, i, h)),
                pl.BlockSpec((None, seq, HEAD_W), lambda b, h, i, bound: (b, 0, h)),
                pl.BlockSpec((None, nq, HEAD_W, t), lambda b, h, i, bound: (b, 0, h, 0)),
                pl.BlockSpec((1, HEAD_W), lambda b, h, i, bound: (0, 0)),
            ],
            out_specs=pl.BlockSpec((None, t, HEAD_W), lambda b, h, i, bound: (b, i, h)),
            scratch_shapes=[
                pltpu.VMEM((2, 1, t), F32),
                pltpu.VMEM((2, 1, t), F32),
                pltpu.VMEM((2, HEAD_W, t), F32),
            ],
        ),
        out_shape=jax.ShapeDtypeStruct((batch, seq, D_MODEL), BF16),
        compiler_params=pltpu.CompilerParams(
            dimension_semantics=("arbitrary", "arbitrary", "arbitrary"), vmem_limit_bytes=VMEM_LIMIT),
        name="diff_attention",
    )(score_bound, lam_params, q3, k3, vt4, subln_g)


def _proj_residual_kernel(x_ref, y_ref, w_ref, o_ref):
    o_ref[...] = x_ref[...] + jnp.dot(y_ref[...], w_ref[...], preferred_element_type=F32)


def _proj_residual(x2, y2, w_bf):
    n = x2.shape[0]
    tm = ROW_TILE
    return pl.pallas_call(
        _proj_residual_kernel,
        grid=(n // tm,),
        in_specs=[
            pl.BlockSpec((tm, D_MODEL), lambda i: (i, 0)),
            pl.BlockSpec((tm, D_MODEL), lambda i: (i, 0)),
            pl.BlockSpec((D_MODEL, D_MODEL), lambda i: (0, 0)),
        ],
        out_specs=pl.BlockSpec((tm, D_MODEL), lambda i: (i, 0)),
        out_shape=jax.ShapeDtypeStruct((n, D_MODEL), F32),
        compiler_params=pltpu.CompilerParams(dimension_semantics=("arbitrary",), vmem_limit_bytes=VMEM_LIMIT),
        name="attn_out_proj",
    )(x2, y2, w_bf)


def _conv_kernel(x_ref, g_ref, win_ref, bin_ref, dww_ref, dwb_ref, lng_ref, lnb_ref, wout_ref, bout_ref,
                 o_ref, u_ref):
    tm = x_ref.shape[0]
    x = x_ref[...]
    h = _rmsnorm_rows(x, g_ref[...]).astype(BF16)
    a = jnp.dot(h, win_ref[:, :D_MODEL], preferred_element_type=F32) + bin_ref[:, :D_MODEL]
    gate = jnp.dot(h, win_ref[:, D_MODEL:], preferred_element_type=F32) + bin_ref[:, D_MODEL:]
    u = a * jax.nn.sigmoid(gate)

    @pl.when(pl.program_id(1) == 0)
    def _():
        u_ref[0:CONV_HALO, :] = jnp.zeros((CONV_HALO, D_MODEL), F32)

    u_ref[CONV_HALO:CONV_HALO + tm, :] = u
    base = CONV_HALO - (CONV_WIDTH - 1)
    window = u_ref[...]
    n_rows = tm + CONV_HALO
    acc = jnp.zeros((tm, D_MODEL), F32) + dwb_ref[...]
    for phase in range(SUBLANES):
        rolled = window if phase == 0 else pltpu.roll(window, n_rows - phase, 0)
        for row0 in range(0, CONV_HALO + 1, SUBLANES):
            w = row0 + phase - base
            if 0 <= w < CONV_WIDTH:
                acc = acc + dww_ref[w:w + 1, :] * rolled[row0:row0 + tm, :]
    u_ref[0:CONV_HALO, :] = u_ref[tm:tm + CONV_HALO, :]

    mu = jnp.mean(acc, axis=-1, keepdims=True)
    xc = acc - mu
    var = jnp.mean(xc * xc, axis=-1, keepdims=True)
    v = xc * lax.rsqrt(var + EPS) * lng_ref[...] + lnb_ref[...]
    v = (v * jax.nn.sigmoid(v)).astype(BF16)
    o_ref[...] = x + jnp.dot(v, wout_ref[...], preferred_element_type=F32) + bout_ref[...]


def _conv_block(x3, g, win_bf, b_in, dw_w, dw_b, ln_g, ln_b, wout_bf, b_out):
    batch, seq, _ = x3.shape
    tm = CONV_TILE
    row = lambda b, s: (b, s, 0)
    fixed = lambda b, s: (0, 0)
    vec = pl.BlockSpec((1, D_MODEL), fixed)
    return pl.pallas_call(
        _conv_kernel,
        grid=(batch, seq // tm),
        in_specs=[
            pl.BlockSpec((None, tm, D_MODEL), row),
            vec,
            pl.BlockSpec((D_MODEL, 2 * D_MODEL), fixed),
            pl.BlockSpec((1, 2 * D_MODEL), fixed),
            pl.BlockSpec((CONV_HALO, D_MODEL), fixed),
            vec, vec, vec,
            pl.BlockSpec((D_MODEL, D_MODEL), fixed),
            vec,
        ],
        out_specs=pl.BlockSpec((None, tm, D_MODEL), row),
        out_shape=jax.ShapeDtypeStruct((batch, seq, D_MODEL), F32),
        scratch_shapes=[pltpu.VMEM((tm + CONV_HALO, D_MODEL), F32)],
        compiler_params=pltpu.CompilerParams(
            dimension_semantics=("arbitrary", "arbitrary"), vmem_limit_bytes=VMEM_LIMIT),
        name="conformer_conv",
    )(x3, g, win_bf, b_in, dw_w, dw_b, ln_g, ln_b, wout_bf, b_out)


def _store_token_major(ref, v, lead=()):
    t = v.shape[0]
    for s in range(SUBLANES):
        ref[lead + (pl.ds(s, t, stride=SUBLANES), slice(None))] = v[:, s * 128:(s + 1) * 128]


def _load_token_major(ref, t, lead=()):
    return jnp.concatenate(
        [ref[lead + (pl.ds(s, t, stride=SUBLANES), slice(None))] for s in range(SUBLANES)], axis=1)


def _split_bf16(v):
    hi = v.astype(BF16)
    lo = (v - hi.astype(F32)).astype(BF16)
    return hi, lo


def _router_kernel(x_ref, g_ref, wr_ref, br_ref, meta_ref, wts_ref, cnt_ref, h3_ref, carry_ref, tri_ref):
    t = x_ref.shape[0]

    @pl.when(pl.program_id(0) == 0)
    def _():
        carry_ref[...] = jnp.zeros(carry_ref.shape, F32)
        before = lax.broadcasted_iota(I32, (t, t), 0) < lax.broadcasted_iota(I32, (t, t), 1)
        tri_ref[...] = jnp.where(before, 1.0, 0.0).astype(BF16)

    h = _rmsnorm_rows(x_ref[...], g_ref[...])
    _store_token_major(h3_ref, h)
    h_hi, h_lo = _split_bf16(h)
    w_hi, w_lo = _split_bf16(wr_ref[...])
    logits = (lax.dot_general(w_hi, h_hi, _NT, preferred_element_type=F32)
              + lax.dot_general(w_hi, h_lo, _NT, preferred_element_type=F32)
              + lax.dot_general(w_lo, h_hi, _NT, preferred_element_type=F32)) + br_ref[...]

    gl = logits[N_EXPERTS:N_EXPERTS + N_GROUPS]
    gmax = jnp.max(gl, axis=0, keepdims=True)
    g_p = 1.0 / jnp.sum(jnp.exp(gl - gmax), axis=0, keepdims=True)
    row_g = lax.broadcasted_iota(I32, gl.shape, 0)
    g_idx = jnp.min(jnp.where(gl == gmax, row_g, N_GROUPS), axis=0, keepdims=True)

    e_sel = jnp.zeros((EXPERTS_PER_GROUP, t), F32)
    for grp in range(N_GROUPS):
        e_sel = e_sel + jnp.where(g_idx == grp, logits[grp * EXPERTS_PER_GROUP:(grp + 1) * EXPERTS_PER_GROUP], 0.0)
    row_e = lax.broadcasted_iota(I32, e_sel.shape, 0)
    m1 = jnp.max(e_sel, axis=0, keepdims=True)
    i1 = jnp.min(jnp.where(e_sel == m1, row_e, EXPERTS_PER_GROUP), axis=0, keepdims=True)
    rest = jnp.where(row_e == i1, -jnp.inf, e_sel)
    m2 = jnp.max(rest, axis=0, keepdims=True)
    i2 = jnp.min(jnp.where(rest == m2, row_e, EXPERTS_PER_GROUP), axis=0, keepdims=True)
    r = jnp.exp(m2 - m1)
    p1 = 1.0 / (1.0 + r)
    f1 = g_idx * EXPERTS_PER_GROUP + i1
    f2 = g_idx * EXPERTS_PER_GROUP + i2

    row_x = lax.broadcasted_iota(I32, (N_EXPERTS, t), 0)
    oh1 = row_x == f1
    oh2 = row_x == f2
    onehot = jnp.where(oh1 | oh2, 1.0, 0.0)
    prior = jnp.dot(onehot.astype(BF16), tri_ref[...], preferred_element_type=F32) + carry_ref[:, 0:1]
    rank1 = jnp.sum(jnp.where(oh1, prior, 0.0), axis=0, keepdims=True)
    rank2 = jnp.sum(jnp.where(oh2, prior, 0.0), axis=0, keepdims=True)
    total = carry_ref[...] + jnp.sum(onehot, axis=1, keepdims=True)
    carry_ref[...] = total
    cnt_ref[...] = total.astype(I32)

    zi = jnp.zeros((4, t), I32)
    meta_ref[...] = jnp.concatenate([f1, f2, rank1.astype(I32), rank2.astype(I32), zi], axis=0)
    wts_ref[...] = jnp.concatenate([g_p * p1, g_p * (r * p1), jnp.zeros((6, t), F32)], axis=0)


def _route(x2, g, wr, br):
    n = x2.shape[0]
    t = ROUTER_TILE
    return pl.pallas_call(
        _router_kernel,
        grid=(n // t,),
        in_specs=[
            pl.BlockSpec((t, D_MODEL), lambda i: (i, 0)),
            pl.BlockSpec((1, D_MODEL), lambda i: (0, 0)),
            pl.BlockSpec((ROUTER_ROWS, D_MODEL), lambda i: (0, 0)),
            pl.BlockSpec((ROUTER_ROWS, 1), lambda i: (0, 0)),
        ],
        out_specs=[
            pl.BlockSpec((8, t), lambda i: (0, i)),
            pl.BlockSpec((8, t), lambda i: (0, i)),
            pl.BlockSpec((N_EXPERTS, 128), lambda i: (0, 0)),
            pl.BlockSpec((t * SUBLANES, 128), lambda i: (i, 0)),
        ],
        out_shape=[
            jax.ShapeDtypeStruct((8, n), I32),
            jax.ShapeDtypeStruct((8, n), F32),
            jax.ShapeDtypeStruct((N_EXPERTS, 128), I32),
            jax.ShapeDtypeStruct((n * SUBLANES, 128), F32),
        ],
        scratch_shapes=[pltpu.VMEM((N_EXPERTS, 128), F32), pltpu.VMEM((t, t), BF16)],
        compiler_params=pltpu.CompilerParams(dimension_semantics=("arbitrary",), vmem_limit_bytes=VMEM_LIMIT),
        name="moe_router",
    )(x2, g, wr, br)


def _positions_kernel(cnt_ref, meta_ref, pos_ref, tile_expert_ref, info_ref, tile_valid_ref, off_ref,
                      *, max_tiles):
    def per_expert(e, tiles_before):
        count = cnt_ref[e, 0]
        n_tiles = (count + (MOE_ROW_TILE - 1)) // MOE_ROW_TILE
        off_ref[e] = tiles_before * MOE_ROW_TILE

        def fill(k, c):
            tile_expert_ref[tiles_before + k] = e
            tile_valid_ref[tiles_before + k] = jnp.minimum(count - k * MOE_ROW_TILE, MOE_ROW_TILE)
            return c

        lax.fori_loop(0, n_tiles, fill, 0)
        return tiles_before + n_tiles

    used = lax.fori_loop(0, N_EXPERTS, per_expert, jnp.int32(0))
    info_ref[0] = used
    last_expert = tile_expert_ref[used - 1]

    def fill_tail(k, c):
        tile_expert_ref[k] = last_expert
        tile_valid_ref[k] = 0
        return c

    lax.fori_loop(used, max_tiles, fill_tail, 0)
    tile_expert_ref[max_tiles] = last_expert

    ids = meta_ref[0:2, :]
    pos = meta_ref[2:4, :]
    for e in range(N_EXPERTS):
        pos = pos + jnp.where(ids == e, off_ref[e], 0)
    pos_ref[...] = pos


def _positions(counts, meta, max_tiles):
    n = meta.shape[1]
    smem = pl.BlockSpec(memory_space=pltpu.SMEM)
    return pl.pallas_call(
        functools.partial(_positions_kernel, max_tiles=max_tiles),
        in_specs=[smem, pl.BlockSpec((8, n), lambda: (0, 0))],
        out_specs=[pl.BlockSpec((2, n), lambda: (0, 0)), smem, smem, smem],
        out_shape=[
            jax.ShapeDtypeStruct((2, n), I32),
            jax.ShapeDtypeStruct((max_tiles + 1,), I32),
            jax.ShapeDtypeStruct((1,), I32),
            jax.ShapeDtypeStruct((max_tiles,), I32),
        ],
        scratch_shapes=[pltpu.SMEM((N_EXPERTS,), I32)],
        name="moe_positions",
    )(counts, meta)


def _inverse_kernel(pos_ref, tile_expert_ref, tile_valid_ref, dst_ref, *, n_tokens):
    tm = MOE_ROW_TILE

    def pad_tile(t, c):
        pad_base = 2 * n_tokens + tile_expert_ref[t] * tm

        def pad_row(r, c2):
            dst_ref[t * tm + r] = pad_base + r
            return c2

        lax.fori_loop(tile_valid_ref[t], tm, pad_row, 0)
        return c

    lax.fori_loop(0, dst_ref.shape[0] // tm, pad_tile, 0)

    def token(n, c):
        dst_ref[pos_ref[n]] = n
        dst_ref[pos_ref[n_tokens + n]] = n_tokens + n
        return c

    lax.fori_loop(0, n_tokens, token, 0, unroll=128)


def _inverse_map(pos_flat, tile_expert, tile_valid, n_rows):
    smem = pl.BlockSpec(memory_space=pltpu.SMEM)
    return pl.pallas_call(
        functools.partial(_inverse_kernel, n_tokens=pos_flat.shape[0] // 2),
        in_specs=[smem, smem, smem],
        out_specs=smem,
        out_shape=jax.ShapeDtypeStruct((n_rows,), I32),
        name="moe_inverse_map",
    )(pos_flat, tile_expert, tile_valid)


def _expert_kernel(tile_expert_ref, info_ref, dst_ref, x_hbm, wg_ref, wu_ref, wd_ref,
                   out_hbm, xbuf, ybuf, wg_bf, wu_bf, wd_bf, gsem, ssem, *, n_tokens):
    j = pl.program_id(0)
    used = info_ref[0]
    tm = MOE_ROW_TILE

    def token_rows(ref, token):
        return ref.at[pl.ds(pl.multiple_of(token * SUBLANES, SUBLANES), SUBLANES)]


    def start_gather(tile, s):
        for r in range(tm):
            token = dst_ref[tile * tm + r] & (n_tokens - 1)
            pltpu.make_async_copy(token_rows(x_hbm, token), xbuf.at[s, pl.ds(r * SUBLANES, SUBLANES)],
                                  gsem.at[s]).start(priority=r % 2)

    def wait_gather(s):
        pltpu.make_async_copy(x_hbm.at[pl.ds(0, tm * SUBLANES)], xbuf.at[s], gsem.at[s]).wait()

    def start_scatter(tile, s):
        for r in range(tm):
            pltpu.make_async_copy(ybuf.at[s, pl.ds(r * SUBLANES, SUBLANES)],
                                  token_rows(out_hbm, dst_ref[tile * tm + r]),
                                  ssem.at[s]).start(priority=(r + 1) % 2)

    def wait_scatter(s):
        pltpu.make_async_copy(ybuf.at[s], out_hbm.at[pl.ds(0, tm * SUBLANES)], ssem.at[s]).wait()

    def cast_weights():
        wg_bf[...] = wg_ref[...].astype(BF16)
        wu_bf[...] = wu_ref[...].astype(BF16)
        wd_bf[...] = wd_ref[...].astype(BF16)

    def mlp(s):
        x = _load_token_major(xbuf, tm, (s,)).astype(BF16)
        gate = jnp.dot(x, wg_bf[...], preferred_element_type=F32)
        up = jnp.dot(x, wu_bf[...], preferred_element_type=F32)
        hid = (gate * jax.nn.sigmoid(gate) * up).astype(BF16)
        _store_token_major(ybuf, jnp.dot(hid, wd_bf[...], preferred_element_type=F32), (s,))

    @pl.when(j == 0)
    def _():
        def pad_rows_copy(e):
            return pltpu.make_async_copy(
                ybuf.at[1], out_hbm.at[pl.ds((2 * n_tokens + e * tm) * SUBLANES, tm * SUBLANES)], ssem.at[1])

        ybuf[1] = jnp.zeros(ybuf.shape[1:], F32)
        for e in range(N_EXPERTS):
            pad_rows_copy(e).start()
        start_gather(0, 0)
        for e in range(N_EXPERTS):
            pad_rows_copy(e).wait()
        wait_gather(0)
        cast_weights()
        start_gather(jnp.minimum(1, used - 1), 1)
        mlp(0)

    for s in range(2):
        @pl.when((j >= 1) & (j < used) & (lax.rem(j, 2) == s))
        def _():
            wait_gather(s)

            @pl.when(j >= 2)
            def _():
                wait_scatter(s)

            @pl.when(tile_expert_ref[j] != tile_expert_ref[j - 1])
            def _():
                cast_weights()

            start_gather(jnp.minimum(j + 1, used - 1), 1 - s)
            start_scatter(j - 1, 1 - s)
            mlp(s)

        @pl.when((j == used) & (lax.rem(j, 2) == s))
        def _():
            wait_gather(s)

            @pl.when(j >= 2)
            def _():
                wait_scatter(s)

            start_scatter(j - 1, 1 - s)
            wait_scatter(1 - s)


def _experts(tile_expert, info, dst, h3, w_gate, w_up, w_down, layer):
    n = h3.shape[0] // SUBLANES
    assert n & (n - 1) == 0
    tm = MOE_ROW_TILE
    wsel = lambda j, te, info, dst: (layer, te[j], 0, 0)
    return pl.pallas_call(
        functools.partial(_expert_kernel, n_tokens=n),
        grid_spec=pltpu.PrefetchScalarGridSpec(
            num_scalar_prefetch=3,
            grid=(dst.shape[0] // tm + 1,),
            in_specs=[
                pl.BlockSpec(memory_space=pl.ANY),
                pl.BlockSpec((None, None, D_MODEL, D_EXPERT), wsel),
                pl.BlockSpec((None, None, D_MODEL, D_EXPERT), wsel),
                pl.BlockSpec((None, None, D_EXPERT, D_MODEL), wsel),
            ],
            out_specs=pl.BlockSpec(memory_space=pl.ANY),
            scratch_shapes=[
                pltpu.VMEM((2, tm * SUBLANES, 128), F32),
                pltpu.VMEM((2, tm * SUBLANES, 128), F32),
                pltpu.VMEM((D_MODEL, D_EXPERT), BF16),
                pltpu.VMEM((D_MODEL, D_EXPERT), BF16),
                pltpu.VMEM((D_EXPERT, D_MODEL), BF16),
                pltpu.SemaphoreType.DMA((2,)),
                pltpu.SemaphoreType.DMA((2,)),
            ],
        ),
        out_shape=jax.ShapeDtypeStruct(((2 * n + N_EXPERTS * tm) * SUBLANES, 128), F32),
        compiler_params=pltpu.CompilerParams(
            dimension_semantics=("arbitrary",), vmem_limit_bytes=VMEM_LIMIT, has_side_effects=True),
        name="moe_experts",
    )(tile_expert, info, dst, h3, w_gate, w_up, w_down)


def _combine_kernel(x_ref, wts_ref, y1_ref, y2_ref, o_ref):
    tc = x_ref.shape[0]
    eye = jnp.where(lax.broadcasted_iota(I32, (tc, tc), 0) == lax.broadcasted_iota(I32, (tc, tc), 1),
                    1.0, 0.0).astype(BF16)
    w_hi, w_lo = _split_bf16(wts_ref[...])
    w_col = (lax.dot_general(eye, w_hi, _NT, preferred_element_type=F32)
             + lax.dot_general(eye, w_lo, _NT, preferred_element_type=F32))
    o_ref[...] = (x_ref[...] + w_col[:, 0:1] * _load_token_major(y1_ref, tc)
                  + w_col[:, 1:2] * _load_token_major(y2_ref, tc))


def _combine(x2, wts, ys):
    n = x2.shape[0]
    tc = COMBINE_TILE
    return pl.pallas_call(
        _combine_kernel,
        grid=(n // tc,),
        in_specs=[
            pl.BlockSpec((tc, D_MODEL), lambda i: (i, 0)),
            pl.BlockSpec((8, tc), lambda i: (0, i)),
            pl.BlockSpec((tc * SUBLANES, 128), lambda i: (i, 0)),
            pl.BlockSpec((tc * SUBLANES, 128), lambda i: (i + n // tc, 0)),
        ],
        out_specs=pl.BlockSpec((tc, D_MODEL), lambda i: (i, 0)),
        out_shape=jax.ShapeDtypeStruct((n, D_MODEL), F32),
        compiler_params=pltpu.CompilerParams(dimension_semantics=("arbitrary",), vmem_limit_bytes=VMEM_LIMIT),
        name="moe_combine",
    )(x2, wts, ys, ys)


def _hier_moe(x2, g, w_rg, b_rg, w_re, b_re, w_gate, w_up, w_down, layer):
    n = x2.shape[0]
    max_tiles = (2 * n) // MOE_ROW_TILE + N_EXPERTS
    wr = jnp.concatenate([
        jnp.transpose(w_re, (0, 2, 1)).reshape(N_EXPERTS, D_MODEL),
        w_rg.T,
        jnp.zeros((ROUTER_ROWS - N_EXPERTS - N_GROUPS, D_MODEL), F32)], axis=0)
    br = jnp.concatenate([
        b_re.reshape(N_EXPERTS), b_rg, jnp.zeros((ROUTER_ROWS - N_EXPERTS - N_GROUPS,), F32)])[:, None]

    meta, wts, counts, h3 = _route(x2, g, wr, br)
    pos, tile_expert, info, tile_valid = _positions(counts, meta, max_tiles)
    dst = _inverse_map(pos.reshape(2 * n), tile_expert, tile_valid, max_tiles * MOE_ROW_TILE)
    ys = _experts(tile_expert, info, dst, h3, w_gate, w_up, w_down, layer)
    return _combine(x2, wts, ys)


def _rope_tables(seq):
    half = ROT_DIM // 2
    offset = jnp.arange(HEAD_W) % HEAD_DIM
    inv_freq = ROPE_THETA ** (-(2 * (offset % half)).astype(F32) / ROT_DIM)
    lane_freq = jnp.where(offset < ROT_DIM, inv_freq, 0.0)
    ang = jnp.arange(seq, dtype=F32)[:, None] * lane_freq[None, :]
    cos, sin = jnp.cos(ang), jnp.sin(ang)
    sinp = jnp.where((offset >= half) & (offset < ROT_DIM), sin, 0.0)
    sinm = jnp.where(offset < half, -sin, 0.0)
    return cos, sinp, sinm


def _gained_rope_tables(tables, gain, scale):
    cos, sinp, sinm = tables
    half = ROT_DIM // 2
    lanes = jnp.concatenate([gain, gain]) * scale
    return cos * lanes, sinp * jnp.roll(lanes, half), sinm * jnp.roll(lanes, -half)


def kernel(x, attn_norm, diff_w_in, diff_q_gain, diff_k_gain, diff_lambda_q1, diff_lambda_k1,
           diff_lambda_q2, diff_lambda_k2, diff_subln, diff_w_out,
           conv_norm, conv_w_in, conv_b_in, conv_dw_w, conv_dw_b, conv_ln_g, conv_ln_b,
           conv_w_out, conv_b_out,
           ffn_norm, router_group_w, router_group_b, router_expert_w, router_expert_b,
           moe_w_gate, moe_w_up, moe_w_down):
    batch, seq, d = x.shape
    assert d == D_MODEL and seq % ATTN_TILE == 0 and seq % CONV_TILE == 0
    n = batch * seq
    depth = ffn_norm.shape[0]
    cos_t, sinp_t, sinm_t = _rope_tables(seq)
    row = lambda v: v.reshape(1, -1)
    x2 = x.reshape(n, d)
    for i in range(depth):
        j = i // 2
        if i % 2 == 0:
            lambda_init = 0.8 - 0.6 * math.exp(-0.3 * i)
            tables = (cos_t, sinp_t, sinm_t)
            q2, k2, vt4 = _qkv_project(x2, row(attn_norm[j]), diff_w_in[j].astype(BF16),
                                       _gained_rope_tables(tables, diff_q_gain[j], Q_SCALE),
                                       _gained_rope_tables(tables, diff_k_gain[j], 1.0), batch, seq)
            lam_params = jnp.stack([diff_lambda_q1[j], diff_lambda_k1[j], diff_lambda_q2[j], diff_lambda_k2[j]])
            score_bound = (SCORE_BOUND_UNIT * jnp.max(jnp.abs(diff_q_gain[j]))
                           * jnp.max(jnp.abs(diff_k_gain[j]))).reshape(1)
            o3 = _diff_attention(score_bound, q2.reshape(batch, seq, d), k2.reshape(batch, seq, d), vt4,
                                 lam_params, row(diff_subln[j]), lambda_init)
            x2 = _proj_residual(x2, o3.reshape(n, d), diff_w_out[j].astype(BF16))
        else:
            dw_w = jnp.concatenate([conv_dw_w[j], jnp.zeros((CONV_HALO - CONV_WIDTH, d), F32)], axis=0)
            x2 = _conv_block(x2.reshape(batch, seq, d), row(conv_norm[j]), conv_w_in[j].astype(BF16),
                             row(conv_b_in[j]), dw_w, row(conv_dw_b[j]), row(conv_ln_g[j]), row(conv_ln_b[j]),
                             conv_w_out[j].astype(BF16), row(conv_b_out[j])).reshape(n, d)
        x2 = _hier_moe(x2, row(ffn_norm[i]), router_group_w[i], router_group_b[i],
                       router_expert_w[i], router_expert_b[i], moe_w_gate, moe_w_up, moe_w_down, i)
    return x2.reshape(batch, seq, d)
```

```python
import functools
import math

import jax
import jax.numpy as jnp
from jax import lax
from jax.experimental import pallas as pl
from jax.experimental.pallas import tpu as pltpu

F32 = jnp.float32
BF16 = jnp.bfloat16
I32 = jnp.int32

D_MODEL = 1024
HEAD_DIM = 64
HEAD_W = 2 * HEAD_DIM
N_HEADS = D_MODEL // HEAD_W
ROT_DIM = HEAD_DIM // 4
ROPE_THETA = 500000.0
CHUNK = 64
CONV_WIDTH = 31
SUBLANES = 8
CONV_HALO = 32
N_GROUPS = 4
EXPERTS_PER_GROUP = 8
N_EXPERTS = N_GROUPS * EXPERTS_PER_GROUP
D_EXPERT = D_MODEL // 2
EPS = 1e-6
ROUTER_ROWS = 64
LOG2_E = math.log2(math.e)
Q_SCALE = HEAD_DIM ** -0.5 * LOG2_E
SCORE_BOUND_UNIT = 1.02 * HEAD_DIM * Q_SCALE
UNSHIFTED_SOFTMAX_LIMIT = 50.0

ATTN_TILE = 1024
ATTN_GROUPS = (2, 1)
ROW_TILE = 1024
CONV_TILE = 1024
ROUTER_TILE = 1024
MOE_ROW_TILE = 256
COMBINE_TILE = 1024
VMEM_LIMIT = 48 * 1024 * 1024

_NT = (((1,), (1,)), ((), ()))


def _rmsnorm_rows(x, g):
    return x * lax.rsqrt(jnp.mean(x * x, axis=-1, keepdims=True) + EPS) * g


def _qkv_kernel(x_ref, g_ref, w_ref, qc_ref, qp_ref, qm_ref, kc_ref, kp_ref, km_ref,
                q_ref, k_ref, vt_ref, *, col_tile):
    tm = x_ref.shape[0]
    h = _rmsnorm_rows(x_ref[...], g_ref[...]).astype(BF16)
    lane = lax.broadcasted_iota(I32, (tm, HEAD_W), 1)
    lo = lane < HEAD_DIM
    q_tabs = (qc_ref[...], qp_ref[...], qm_ref[...])
    k_tabs = (kc_ref[...], kp_ref[...], km_ref[...])

    def norm_rope(y, tabs):
        y2 = y * y
        s_lo = jnp.sum(jnp.where(lo, y2, 0.0), axis=-1, keepdims=True)
        s_hi = jnp.sum(y2, axis=-1, keepdims=True) - s_lo
        r = jnp.where(lo, lax.rsqrt(s_lo * (1.0 / HEAD_DIM) + EPS), lax.rsqrt(s_hi * (1.0 / HEAD_DIM) + EPS))
        yn = y * r
        half = ROT_DIM // 2
        return yn * tabs[0] + pltpu.roll(yn, half, 1) * tabs[1] + pltpu.roll(yn, HEAD_W - half, 1) * tabs[2]

    for c in range(3 * D_MODEL // col_tile):
        y = jnp.dot(h, w_ref[:, c * col_tile:(c + 1) * col_tile], preferred_element_type=F32)
        for s in range(col_tile // HEAD_W):
            col = c * col_tile + s * HEAD_W
            ys = y[:, s * HEAD_W:(s + 1) * HEAD_W]
            if col < D_MODEL:
                q_ref[:, col:col + HEAD_W] = norm_rope(ys, q_tabs).astype(BF16)
            elif col < 2 * D_MODEL:
                k_ref[:, col - D_MODEL:col - D_MODEL + HEAD_W] = norm_rope(ys, k_tabs).astype(BF16)
            else:
                vt_ref[col - 2 * D_MODEL:col - 2 * D_MODEL + HEAD_W, :] = ys.T.astype(BF16)


def _qkv_project(x2, g, w_bf, q_tabs, k_tabs, batch, seq):
    n = x2.shape[0]
    tm = ATTN_TILE
    tiles_per_seq = seq // tm
    row = lambda i: (i, 0)
    fixed = lambda i: (0, 0)
    tab = lambda i: (i % tiles_per_seq, 0)
    return pl.pallas_call(
        functools.partial(_qkv_kernel, col_tile=512),
        grid=(n // tm,),
        in_specs=[
            pl.BlockSpec((tm, D_MODEL), row),
            pl.BlockSpec((1, D_MODEL), fixed),
            pl.BlockSpec((D_MODEL, 3 * D_MODEL), fixed),
        ] + [pl.BlockSpec((tm, HEAD_W), tab)] * 6,
        out_specs=[
            pl.BlockSpec((tm, D_MODEL), row),
            pl.BlockSpec((tm, D_MODEL), row),
            pl.BlockSpec((None, None, D_MODEL, tm), lambda i: (i // tiles_per_seq, i % tiles_per_seq, 0, 0)),
        ],
        out_shape=[
            jax.ShapeDtypeStruct((n, D_MODEL), BF16),
            jax.ShapeDtypeStruct((n, D_MODEL), BF16),
            jax.ShapeDtypeStruct((batch, tiles_per_seq, D_MODEL, tm), BF16),
        ],
        compiler_params=pltpu.CompilerParams(dimension_semantics=("arbitrary",), vmem_limit_bytes=VMEM_LIMIT),
        name="qkv_project",
    )(x2, g, w_bf, *q_tabs, *k_tabs)


def _attn_kernel(bound_ref, lam_ref, q_ref, k_ref, vt_ref, g_ref, o_ref, m_ref, l_ref, acc_ref, *, lambda_init):
    t = q_ref.shape[0]
    i = pl.program_id(2)
    q = q_ref[...]
    lane = lax.broadcasted_iota(I32, (t, HEAD_W), 1)
    zero = jnp.zeros_like(q)
    q_comp = (jnp.where(lane < HEAD_DIM, q, zero), jnp.where(lane >= HEAD_DIM, q, zero))


    def tiles(j):
        return k_ref[pl.ds(pl.multiple_of(j * t, t), t), :], vt_ref[j]

    def chunk_mask(n_keys=t, n_queries=t):
        key_chunk = lax.broadcasted_iota(I32, (n_keys, n_queries), 0) // CHUNK
        qry_chunk = lax.broadcasted_iota(I32, (n_keys, n_queries), 1) // CHUNK
        return key_chunk <= qry_chunk

    def unshifted_block(kt, vt, c, q0, masked, first=False):
        n_keys = kt.shape[0]
        s = lax.dot_general(kt, q_comp[c][q0:], _NT, preferred_element_type=F32)
        p = jnp.exp2(s)
        if masked:
            p = jnp.where(chunk_mask(n_keys, t - q0), p, 0.0)
        l_new = jnp.sum(p, axis=0, keepdims=True)
        acc_new = jnp.dot(vt, p.astype(BF16), preferred_element_type=F32)
        if first:
            assert q0 == 0
            l_ref[c] = l_new
            acc_ref[c] = acc_new
        else:
            l_ref[c, :, q0:] += l_new
            acc_ref[c, :, q0:] += acc_new

    def unshifted_step(j, masked):
        kt, vt = tiles(j)
        half = t // 2
        for c in range(2):
            if masked:
                unshifted_block(kt[:half], vt[:, :half], c, 0, True, first=True)
                unshifted_block(kt[half:], vt[:, half:], c, half, True)
            else:
                unshifted_block(kt, vt, c, 0, False)

    def online_step(j, masked):
        kt, vt = tiles(j)
        for c in range(2):
            s = lax.dot_general(kt, q_comp[c], _NT, preferred_element_type=F32)
            if masked:
                s = jnp.where(chunk_mask(), s, -jnp.inf)
            m_old = m_ref[c]
            m_new = jnp.maximum(m_old, jnp.max(s, axis=0, keepdims=True))
            alpha = jnp.exp2(m_old - m_new)
            p = jnp.exp2(s - m_new)
            l_ref[c] = alpha * l_ref[c] + jnp.sum(p, axis=0, keepdims=True)
            acc_ref[c] = alpha * acc_ref[c] + jnp.dot(vt, p.astype(BF16), preferred_element_type=F32)
            m_ref[c] = m_new

    def sweep(step, groups, diagonal_first):
        if diagonal_first:
            step(i, True)
        done = 0
        for group in groups:
            def trip(jj, carry, group=group, done=done):
                for u in range(group):
                    step(done + jj * group + u, False)
                return carry

            n_trips = (i - done) // group
            lax.fori_loop(0, n_trips, trip, 0)
            done = done + n_trips * group
        if not diagonal_first:
            step(i, True)

    unshifted_ok = bound_ref[0] <= UNSHIFTED_SOFTMAX_LIMIT

    @pl.when(unshifted_ok)
    def _():
        sweep(unshifted_step, ATTN_GROUPS, True)

    @pl.when(jnp.logical_not(unshifted_ok))
    def _():
        m_ref[...] = jnp.full(m_ref.shape, -jnp.inf, F32)
        l_ref[...] = jnp.zeros(l_ref.shape, F32)
        acc_ref[...] = jnp.zeros(acc_ref.shape, F32)
        sweep(online_step, (1,), False)

    lam_p = lam_ref[...]
    lam = (jnp.exp(jnp.sum(lam_p[0:1] * lam_p[1:2], axis=-1, keepdims=True))
           - jnp.exp(jnp.sum(lam_p[2:3] * lam_p[3:4], axis=-1, keepdims=True)) + lambda_init)
    ot = acc_ref[0] * (1.0 / l_ref[0]) - lam * (acc_ref[1] * (1.0 / l_ref[1]))
    ot = ot * lax.rsqrt(jnp.mean(ot * ot, axis=0, keepdims=True) + EPS)
    o_ref[...] = (ot.T * g_ref[...] * (1.0 - lambda_init)).astype(o_ref.dtype)


def _diff_attention(score_bound, q3, k3, vt4, lam_params, subln_g, lambda_init):
    batch, seq, _ = q3.shape
    t = ATTN_TILE
    nq = seq // t
    return pl.pallas_call(
        functools.partial(_attn_kernel, lambda_init=lambda_init),
        grid_spec=pltpu.PrefetchScalarGridSpec(
            num_scalar_prefetch=1,
            grid=(batch, N_HEADS, nq),
            in_specs=[
                pl.BlockSpec((4, HEAD_DIM), lambda b, h, i, bound: (0, 0)),
                pl.BlockSpec((None, t, HEAD_W), lambda b, h, i, bound: (b, i, h)),
                pl.BlockSpec((None, seq, HEAD_W), lambda b, h, i, bound: (b, 0, h)),
                pl.BlockSpec((None, nq, HEAD_W, t), lambda b, h, i, bound: (b, 0, h, 0)),
                pl.BlockSpec((1, HEAD_W), lambda b, h, i, bound: (0, 0)),
            ],
            out_specs=pl.BlockSpec((None, t, HEAD_W), lambda b, h, i, bound: (b, i, h)),
            scratch_shapes=[
                pltpu.VMEM((2, 1, t), F32),
                pltpu.VMEM((2, 1, t), F32),
                pltpu.VMEM((2, HEAD_W, t), F32),
            ],
        ),
        out_shape=jax.ShapeDtypeStruct((batch, seq, D_MODEL), BF16),
        compiler_params=pltpu.CompilerParams(
            dimension_semantics=("arbitrary", "arbitrary", "arbitrary"), vmem_limit_bytes=VMEM_LIMIT),
        name="diff_attention",
    )(score_bound, lam_params, q3, k3, vt4, subln_g)


def _proj_residual_kernel(x_ref, y_ref, w_ref, o_ref):
    o_ref[...] = x_ref[...] + jnp.dot(y_ref[...], w_ref[...], preferred_element_type=F32)


def _proj_residual(x2, y2, w_bf):
    n = x2.shape[0]
    tm = ROW_TILE
    return pl.pallas_call(
        _proj_residual_kernel,
        grid=(n // tm,),
        in_specs=[
            pl.BlockSpec((tm, D_MODEL), lambda i: (i, 0)),
            pl.BlockSpec((tm, D_MODEL), lambda i: (i, 0)),
            pl.BlockSpec((D_MODEL, D_MODEL), lambda i: (0, 0)),
        ],
        out_specs=pl.BlockSpec((tm, D_MODEL), lambda i: (i, 0)),
        out_shape=jax.ShapeDtypeStruct((n, D_MODEL), F32),
        compiler_params=pltpu.CompilerParams(dimension_semantics=("arbitrary",), vmem_limit_bytes=VMEM_LIMIT),
        name="attn_out_proj",
    )(x2, y2, w_bf)


def _conv_kernel(x_ref, g_ref, win_ref, bin_ref, dww_ref, dwb_ref, lng_ref, lnb_ref, wout_ref, bout_ref,
                 o_ref, u_ref):
    tm = x_ref.shape[0]
    x = x_ref[...]
    h = _rmsnorm_rows(x, g_ref[...]).astype(BF16)
    a = jnp.dot(h, win_ref[:, :D_MODEL], preferred_element_type=F32) + bin_ref[:, :D_MODEL]
    gate = jnp.dot(h, win_ref[:, D_MODEL:], preferred_element_type=F32) + bin_ref[:, D_MODEL:]
    u = a * jax.nn.sigmoid(gate)

    @pl.when(pl.program_id(1) == 0)
    def _():
        u_ref[0:CONV_HALO, :] = jnp.zeros((CONV_HALO, D_MODEL), F32)

    u_ref[CONV_HALO:CONV_HALO + tm, :] = u
    base = CONV_HALO - (CONV_WIDTH - 1)
    window = u_ref[...]
    n_rows = tm + CONV_HALO
    acc = jnp.zeros((tm, D_MODEL), F32) + dwb_ref[...]
    for phase in range(SUBLANES):
        rolled = window if phase == 0 else pltpu.roll(window, n_rows - phase, 0)
        for row0 in range(0, CONV_HALO + 1, SUBLANES):
            w = row0 + phase - base
            if 0 <= w < CONV_WIDTH:
                acc = acc + dww_ref[w:w + 1, :] * rolled[row0:row0 + tm, :]
    u_ref[0:CONV_HALO, :] = u_ref[tm:tm + CONV_HALO, :]

    mu = jnp.mean(acc, axis=-1, keepdims=True)
    xc = acc - mu
    var = jnp.mean(xc * xc, axis=-1, keepdims=True)
    v = xc * lax.rsqrt(var + EPS) * lng_ref[...] + lnb_ref[...]
    v = (v * jax.nn.sigmoid(v)).astype(BF16)
    o_ref[...] = x + jnp.dot(v, wout_ref[...], preferred_element_type=F32) + bout_ref[...]


def _conv_block(x3, g, win_bf, b_in, dw_w, dw_b, ln_g, ln_b, wout_bf, b_out):
    batch, seq, _ = x3.shape
    tm = CONV_TILE
    row = lambda b, s: (b, s, 0)
    fixed = lambda b, s: (0, 0)
    vec = pl.BlockSpec((1, D_MODEL), fixed)
    return pl.pallas_call(
        _conv_kernel,
        grid=(batch, seq // tm),
        in_specs=[
            pl.BlockSpec((None, tm, D_MODEL), row),
            vec,
            pl.BlockSpec((D_MODEL, 2 * D_MODEL), fixed),
            pl.BlockSpec((1, 2 * D_MODEL), fixed),
            pl.BlockSpec((CONV_HALO, D_MODEL), fixed),
            vec, vec, vec,
            pl.BlockSpec((D_MODEL, D_MODEL), fixed),
            vec,
        ],
        out_specs=pl.BlockSpec((None, tm, D_MODEL), row),
        out_shape=jax.ShapeDtypeStruct((batch, seq, D_MODEL), F32),
        scratch_shapes=[pltpu.VMEM((tm + CONV_HALO, D_MODEL), F32)],
        compiler_params=pltpu.CompilerParams(
            dimension_semantics=("arbitrary", "arbitrary"), vmem_limit_bytes=VMEM_LIMIT),
        name="conformer_conv",
    )(x3, g, win_bf, b_in, dw_w, dw_b, ln_g, ln_b, wout_bf, b_out)


def _store_token_major(ref, v, lead=()):
    t = v.shape[0]
    for s in range(SUBLANES):
        ref[lead + (pl.ds(s, t, stride=SUBLANES), slice(None))] = v[:, s * 128:(s + 1) * 128]


def _load_token_major(ref, t, lead=()):
    return jnp.concatenate(
        [ref[lead + (pl.ds(s, t, stride=SUBLANES), slice(None))] for s in range(SUBLANES)], axis=1)


def _split_bf16(v):
    hi = v.astype(BF16)
    lo = (v - hi.astype(F32)).astype(BF16)
    return hi, lo


def _router_kernel(x_ref, g_ref, wr_ref, br_ref, meta_ref, wts_ref, cnt_ref, h3_ref, carry_ref, tri_ref):
    t = x_ref.shape[0]

    @pl.when(pl.program_id(0) == 0)
    def _():
        carry_ref[...] = jnp.zeros(carry_ref.shape, F32)
        before = lax.broadcasted_iota(I32, (t, t), 0) < lax.broadcasted_iota(I32, (t, t), 1)
        tri_ref[...] = jnp.where(before, 1.0, 0.0).astype(BF16)

    h = _rmsnorm_rows(x_ref[...], g_ref[...])
    _store_token_major(h3_ref, h)
    h_hi, h_lo = _split_bf16(h)
    w_hi, w_lo = _split_bf16(wr_ref[...])
    logits = (lax.dot_general(w_hi, h_hi, _NT, preferred_element_type=F32)
              + lax.dot_general(w_hi, h_lo, _NT, preferred_element_type=F32)
              + lax.dot_general(w_lo, h_hi, _NT, preferred_element_type=F32)) + br_ref[...]

    gl = logits[N_EXPERTS:N_EXPERTS + N_GROUPS]
    gmax = jnp.max(gl, axis=0, keepdims=True)
    g_p = 1.0 / jnp.sum(jnp.exp(gl - gmax), axis=0, keepdims=True)
    row_g = lax.broadcasted_iota(I32, gl.shape, 0)
    g_idx = jnp.min(jnp.where(gl == gmax, row_g, N_GROUPS), axis=0, keepdims=True)

    e_sel = jnp.zeros((EXPERTS_PER_GROUP, t), F32)
    for grp in range(N_GROUPS):
        e_sel = e_sel + jnp.where(g_idx == grp, logits[grp * EXPERTS_PER_GROUP:(grp + 1) * EXPERTS_PER_GROUP], 0.0)
    row_e = lax.broadcasted_iota(I32, e_sel.shape, 0)
    m1 = jnp.max(e_sel, axis=0, keepdims=True)
    i1 = jnp.min(jnp.where(e_sel == m1, row_e, EXPERTS_PER_GROUP), axis=0, keepdims=True)
    rest = jnp.where(row_e == i1, -jnp.inf, e_sel)
    m2 = jnp.max(rest, axis=0, keepdims=True)
    i2 = jnp.min(jnp.where(rest == m2, row_e, EXPERTS_PER_GROUP), axis=0, keepdims=True)
    r = jnp.exp(m2 - m1)
    p1 = 1.0 / (1.0 + r)
    f1 = g_idx * EXPERTS_PER_GROUP + i1
    f2 = g_idx * EXPERTS_PER_GROUP + i2

    row_x = lax.broadcasted_iota(I32, (N_EXPERTS, t), 0)
    oh1 = row_x == f1
    oh2 = row_x == f2
    onehot = jnp.where(oh1 | oh2, 1.0, 0.0)
    prior = jnp.dot(onehot.astype(BF16), tri_ref[...], preferred_element_type=F32) + carry_ref[:, 0:1]
    rank1 = jnp.sum(jnp.where(oh1, prior, 0.0), axis=0, keepdims=True)
    rank2 = jnp.sum(jnp.where(oh2, prior, 0.0), axis=0, keepdims=True)
    total = carry_ref[...] + jnp.sum(onehot, axis=1, keepdims=True)
    carry_ref[...] = total
    cnt_ref[...] = total.astype(I32)

    zi = jnp.zeros((4, t), I32)
    meta_ref[...] = jnp.concatenate([f1, f2, rank1.astype(I32), rank2.astype(I32), zi], axis=0)
    wts_ref[...] = jnp.concatenate([g_p * p1, g_p * (r * p1), jnp.zeros((6, t), F32)], axis=0)


def _route(x2, g, wr, br):
    n = x2.shape[0]
    t = ROUTER_TILE
    return pl.pallas_call(
        _router_kernel,
        grid=(n // t,),
        in_specs=[
            pl.BlockSpec((t, D_MODEL), lambda i: (i, 0)),
            pl.BlockSpec((1, D_MODEL), lambda i: (0, 0)),
            pl.BlockSpec((ROUTER_ROWS, D_MODEL), lambda i: (0, 0)),
            pl.BlockSpec((ROUTER_ROWS, 1), lambda i: (0, 0)),
        ],
        out_specs=[
            pl.BlockSpec((8, t), lambda i: (0, i)),
            pl.BlockSpec((8, t), lambda i: (0, i)),
            pl.BlockSpec((N_EXPERTS, 128), lambda i: (0, 0)),
            pl.BlockSpec((t * SUBLANES, 128), lambda i: (i, 0)),
        ],
        out_shape=[
            jax.ShapeDtypeStruct((8, n), I32),
            jax.ShapeDtypeStruct((8, n), F32),
            jax.ShapeDtypeStruct((N_EXPERTS, 128), I32),
            jax.ShapeDtypeStruct((n * SUBLANES, 128), F32),
        ],
        scratch_shapes=[pltpu.VMEM((N_EXPERTS, 128), F32), pltpu.VMEM((t, t), BF16)],
        compiler_params=pltpu.CompilerParams(dimension_semantics=("arbitrary",), vmem_limit_bytes=VMEM_LIMIT),
        name="moe_router",
    )(x2, g, wr, br)


def _positions_kernel(cnt_ref, meta_ref, pos_ref, tile_expert_ref, info_ref, tile_valid_ref, off_ref,
                      *, max_tiles):
    def per_expert(e, tiles_before):
        count = cnt_ref[e, 0]
        n_tiles = (count + (MOE_ROW_TILE - 1)) // MOE_ROW_TILE
        off_ref[e] = tiles_before * MOE_ROW_TILE

        def fill(k, c):
            tile_expert_ref[tiles_before + k] = e
            tile_valid_ref[tiles_before + k] = jnp.minimum(count - k * MOE_ROW_TILE, MOE_ROW_TILE)
            return c

        lax.fori_loop(0, n_tiles, fill, 0)
        return tiles_before + n_tiles

    used = lax.fori_loop(0, N_EXPERTS, per_expert, jnp.int32(0))
    info_ref[0] = used
    last_expert = tile_expert_ref[used - 1]

    def fill_tail(k, c):
        tile_expert_ref[k] = last_expert
        tile_valid_ref[k] = 0
        return c

    lax.fori_loop(used, max_tiles, fill_tail, 0)
    tile_expert_ref[max_tiles] = last_expert

    ids = meta_ref[0:2, :]
    pos = meta_ref[2:4, :]
    for e in range(N_EXPERTS):
        pos = pos + jnp.where(ids == e, off_ref[e], 0)
    pos_ref[...] = pos


def _positions(counts, meta, max_tiles):
    n = meta.shape[1]
    smem = pl.BlockSpec(memory_space=pltpu.SMEM)
    return pl.pallas_call(
        functools.partial(_positions_kernel, max_tiles=max_tiles),
        in_specs=[smem, pl.BlockSpec((8, n), lambda: (0, 0))],
        out_specs=[pl.BlockSpec((2, n), lambda: (0, 0)), smem, smem, smem],
        out_shape=[
            jax.ShapeDtypeStruct((2, n), I32),
            jax.ShapeDtypeStruct((max_tiles + 1,), I32),
            jax.ShapeDtypeStruct((1,), I32),
            jax.ShapeDtypeStruct((max_tiles,), I32),
        ],
        scratch_shapes=[pltpu.SMEM((N_EXPERTS,), I32)],
        name="moe_positions",
    )(counts, meta)


def _inverse_kernel(pos_ref, tile_expert_ref, tile_valid_ref, dst_ref, *, n_tokens):
    tm = MOE_ROW_TILE

    def pad_tile(t, c):
        pad_base = 2 * n_tokens + tile_expert_ref[t] * tm

        def pad_row(r, c2):
            dst_ref[t * tm + r] = pad_base + r
            return c2

        lax.fori_loop(tile_valid_ref[t], tm, pad_row, 0)
        return c

    lax.fori_loop(0, dst_ref.shape[0] // tm, pad_tile, 0)

    def token(n, c):
        dst_ref[pos_ref[n]] = n
        dst_ref[pos_ref[n_tokens + n]] = n_tokens + n
        return c

    lax.fori_loop(0, n_tokens, token, 0, unroll=128)


def _inverse_map(pos_flat, tile_expert, tile_valid, n_rows):
    smem = pl.BlockSpec(memory_space=pltpu.SMEM)
    return pl.pallas_call(
        functools.partial(_inverse_kernel, n_tokens=pos_flat.shape[0] // 2),
        in_specs=[smem, smem, smem],
        out_specs=smem,
        out_shape=jax.ShapeDtypeStruct((n_rows,), I32),
        name="moe_inverse_map",
    )(pos_flat, tile_expert, tile_valid)


def _expert_kernel(tile_expert_ref, info_ref, dst_ref, x_hbm, wg_ref, wu_ref, wd_ref,
                   out_hbm, xbuf, ybuf, wg_bf, wu_bf, wd_bf, gsem, ssem, *, n_tokens):
    j = pl.program_id(0)
    used = info_ref[0]
    tm = MOE_ROW_TILE

    def token_rows(ref, token):
        return ref.at[pl.ds(pl.multiple_of(token * SUBLANES, SUBLANES), SUBLANES)]


    def start_gather(tile, s):
        for r in range(tm):
            token = dst_ref[tile * tm + r] & (n_tokens - 1)
            pltpu.make_async_copy(token_rows(x_hbm, token), xbuf.at[s, pl.ds(r * SUBLANES, SUBLANES)],
                                  gsem.at[s]).start(priority=r % 2)

    def wait_gather(s):
        pltpu.make_async_copy(x_hbm.at[pl.ds(0, tm * SUBLANES)], xbuf.at[s], gsem.at[s]).wait()

    def start_scatter(tile, s):
        for r in range(tm):
            pltpu.make_async_copy(ybuf.at[s, pl.ds(r * SUBLANES, SUBLANES)],
                                  token_rows(out_hbm, dst_ref[tile * tm + r]),
                                  ssem.at[s]).start(priority=(r + 1) % 2)

    def wait_scatter(s):
        pltpu.make_async_copy(ybuf.at[s], out_hbm.at[pl.ds(0, tm * SUBLANES)], ssem.at[s]).wait()

    def cast_weights():
        wg_bf[...] = wg_ref[...].astype(BF16)
        wu_bf[...] = wu_ref[...].astype(BF16)
        wd_bf[...] = wd_ref[...].astype(BF16)

    def mlp(s):
        x = _load_token_major(xbuf, tm, (s,)).astype(BF16)
        gate = jnp.dot(x, wg_bf[...], preferred_element_type=F32)
        up = jnp.dot(x, wu_bf[...], preferred_element_type=F32)
        hid = (gate * jax.nn.sigmoid(gate) * up).astype(BF16)
        _store_token_major(ybuf, jnp.dot(hid, wd_bf[...], preferred_element_type=F32), (s,))

    @pl.when(j == 0)
    def _():
        def pad_rows_copy(e):
            return pltpu.make_async_copy(
                ybuf.at[1], out_hbm.at[pl.ds((2 * n_tokens + e * tm) * SUBLANES, tm * SUBLANES)], ssem.at[1])

        ybuf[1] = jnp.zeros(ybuf.shape[1:], F32)
        for e in range(N_EXPERTS):
            pad_rows_copy(e).start()
        start_gather(0, 0)
        for e in range(N_EXPERTS):
            pad_rows_copy(e).wait()
        wait_gather(0)
        cast_weights()
        start_gather(jnp.minimum(1, used - 1), 1)
        mlp(0)

    for s in range(2):
        @pl.when((j >= 1) & (j < used) & (lax.rem(j, 2) == s))
        def _():
            wait_gather(s)

            @pl.when(j >= 2)
            def _():
                wait_scatter(s)

            @pl.when(tile_expert_ref[j] != tile_expert_ref[j - 1])
            def _():
                cast_weights()

            start_gather(jnp.minimum(j + 1, used - 1), 1 - s)
            start_scatter(j - 1, 1 - s)
            mlp(s)

        @pl.when((j == used) & (lax.rem(j, 2) == s))
        def _():
            wait_gather(s)

            @pl.when(j >= 2)
            def _():
                wait_scatter(s)

            start_scatter(j - 1, 1 - s)
            wait_scatter(1 - s)


def _experts(tile_expert, info, dst, h3, w_gate, w_up, w_down, layer):
    n = h3.shape[0] // SUBLANES
    assert n & (n - 1) == 0
    tm = MOE_ROW_TILE
    wsel = lambda j, te, info, dst: (layer, te[j], 0, 0)
    return pl.pallas_call(
        functools.partial(_expert_kernel, n_tokens=n),
        grid_spec=pltpu.PrefetchScalarGridSpec(
            num_scalar_prefetch=3,
            grid=(dst.shape[0] // tm + 1,),
            in_specs=[
                pl.BlockSpec(memory_space=pl.ANY),
                pl.BlockSpec((None, None, D_MODEL, D_EXPERT), wsel),
                pl.BlockSpec((None, None, D_MODEL, D_EXPERT), wsel),
                pl.BlockSpec((None, None, D_EXPERT, D_MODEL), wsel),
            ],
            out_specs=pl.BlockSpec(memory_space=pl.ANY),
            scratch_shapes=[
                pltpu.VMEM((2, tm * SUBLANES, 128), F32),
                pltpu.VMEM((2, tm * SUBLANES, 128), F32),
                pltpu.VMEM((D_MODEL, D_EXPERT), BF16),
                pltpu.VMEM((D_MODEL, D_EXPERT), BF16),
                pltpu.VMEM((D_EXPERT, D_MODEL), BF16),
                pltpu.SemaphoreType.DMA((2,)),
                pltpu.SemaphoreType.DMA((2,)),
            ],
        ),
        out_shape=jax.ShapeDtypeStruct(((2 * n + N_EXPERTS * tm) * SUBLANES, 128), F32),
        compiler_params=pltpu.CompilerParams(
            dimension_semantics=("arbitrary",), vmem_limit_bytes=VMEM_LIMIT, has_side_effects=True),
        name="moe_experts",
    )(tile_expert, info, dst, h3, w_gate, w_up, w_down)


def _combine_kernel(x_ref, wts_ref, y1_ref, y2_ref, o_ref):
    tc = x_ref.shape[0]
    eye = jnp.where(lax.broadcasted_iota(I32, (tc, tc), 0) == lax.broadcasted_iota(I32, (tc, tc), 1),
                    1.0, 0.0).astype(BF16)
    w_hi, w_lo = _split_bf16(wts_ref[...])
    w_col = (lax.dot_general(eye, w_hi, _NT, preferred_element_type=F32)
             + lax.dot_general(eye, w_lo, _NT, preferred_element_type=F32))
    o_ref[...] = (x_ref[...] + w_col[:, 0:1] * _load_token_major(y1_ref, tc)
                  + w_col[:, 1:2] * _load_token_major(y2_ref, tc))


def _combine(x2, wts, ys):
    n = x2.shape[0]
    tc = COMBINE_TILE
    return pl.pallas_call(
        _combine_kernel,
        grid=(n // tc,),
        in_specs=[
            pl.BlockSpec((tc, D_MODEL), lambda i: (i, 0)),
            pl.BlockSpec((8, tc), lambda i: (0, i)),
            pl.BlockSpec((tc * SUBLANES, 128), lambda i: (i, 0)),
            pl.BlockSpec((tc * SUBLANES, 128), lambda i: (i + n // tc, 0)),
        ],
        out_specs=pl.BlockSpec((tc, D_MODEL), lambda i: (i, 0)),
        out_shape=jax.ShapeDtypeStruct((n, D_MODEL), F32),
        compiler_params=pltpu.CompilerParams(dimension_semantics=("arbitrary",), vmem_limit_bytes=VMEM_LIMIT),
        name="moe_combine",
    )(x2, wts, ys, ys)


def _hier_moe(x2, g, w_rg, b_rg, w_re, b_re, w_gate, w_up, w_down, layer):
    n = x2.shape[0]
    max_tiles = (2 * n) // MOE_ROW_TILE + N_EXPERTS
    wr = jnp.concatenate([
        jnp.transpose(w_re, (0, 2, 1)).reshape(N_EXPERTS, D_MODEL),
        w_rg.T,
        jnp.zeros((ROUTER_ROWS - N_EXPERTS - N_GROUPS, D_MODEL), F32)], axis=0)
    br = jnp.concatenate([
        b_re.reshape(N_EXPERTS), b_rg, jnp.zeros((ROUTER_ROWS - N_EXPERTS - N_GROUPS,), F32)])[:, None]

    meta, wts, counts, h3 = _route(x2, g, wr, br)
    pos, tile_expert, info, tile_valid = _positions(counts, meta, max_tiles)
    dst = _inverse_map(pos.reshape(2 * n), tile_expert, tile_valid, max_tiles * MOE_ROW_TILE)
    ys = _experts(tile_expert, info, dst, h3, w_gate, w_up, w_down, layer)
    return _combine(x2, wts, ys)


def _rope_tables(seq):
    half = ROT_DIM // 2
    offset = jnp.arange(HEAD_W) % HEAD_DIM
    inv_freq = ROPE_THETA ** (-(2 * (offset % half)).astype(F32) / ROT_DIM)
    lane_freq = jnp.where(offset < ROT_DIM, inv_freq, 0.0)
    ang = jnp.arange(seq, dtype=F32)[:, None] * lane_freq[None, :]
    cos, sin = jnp.cos(ang), jnp.sin(ang)
    sinp = jnp.where((offset >= half) & (offset < ROT_DIM), sin, 0.0)
    sinm = jnp.where(offset < half, -sin, 0.0)
    return cos, sinp, sinm


def _gained_rope_tables(tables, gain, scale):
    cos, sinp, sinm = tables
    half = ROT_DIM // 2
    lanes = jnp.concatenate([gain, gain]) * scale
    return cos * lanes, sinp * jnp.roll(lanes, half), sinm * jnp.roll(lanes, -half)


def kernel(x, attn_norm, diff_w_in, diff_q_gain, diff_k_gain, diff_lambda_q1, diff_lambda_k1,
           diff_lambda_q2, diff_lambda_k2, diff_subln, diff_w_out,
           conv_norm, conv_w_in, conv_b_in, conv_dw_w, conv_dw_b, conv_ln_g, conv_ln_b,
           conv_w_out, conv_b_out,
           ffn_norm, router_group_w, router_group_b, router_expert_w, router_expert_b,
           moe_w_gate, moe_w_up, moe_w_down):
    batch, seq, d = x.shape
    assert d == D_MODEL and seq % ATTN_TILE == 0 and seq % CONV_TILE == 0
    n = batch * seq
    depth = ffn_norm.shape[0]
    cos_t, sinp_t, sinm_t = _rope_tables(seq)
    row = lambda v: v.reshape(1, -1)
    x2 = x.reshape(n, d)
    for i in range(depth):
        j = i // 2
        if i % 2 == 0:
            lambda_init = 0.8 - 0.6 * math.exp(-0.3 * i)
            tables = (cos_t, sinp_t, sinm_t)
            q2, k2, vt4 = _qkv_project(x2, row(attn_norm[j]), diff_w_in[j].astype(BF16),
                                       _gained_rope_tables(tables, diff_q_gain[j], Q_SCALE),
                                       _gained_rope_tables(tables, diff_k_gain[j], 1.0), batch, seq)
            lam_params = jnp.stack([diff_lambda_q1[j], diff_lambda_k1[j], diff_lambda_q2[j], diff_lambda_k2[j]])
            score_bound = (SCORE_BOUND_UNIT * jnp.max(jnp.abs(diff_q_gain[j]))
                           * jnp.max(jnp.abs(diff_k_gain[j]))).reshape(1)
            o3 = _diff_attention(score_bound, q2.reshape(batch, seq, d), k2.reshape(batch, seq, d), vt4,
                                 lam_params, row(diff_subln[j]), lambda_init)
            x2 = _proj_residual(x2, o3.reshape(n, d), diff_w_out[j].astype(BF16))
        else:
            dw_w = jnp.concatenate([conv_dw_w[j], jnp.zeros((CONV_HALO - CONV_WIDTH, d), F32)], axis=0)
            x2 = _conv_block(x2.reshape(batch, seq, d), row(conv_norm[j]), conv_w_in[j].astype(BF16),
                             row(conv_b_in[j]), dw_w, row(conv_dw_b[j]), row(conv_ln_g[j]), row(conv_ln_b[j]),
                             conv_w_out[j].astype(BF16), row(conv_b_out[j])).reshape(n, d)
        x2 = _hier_moe(x2, row(ffn_norm[i]), router_group_w[i], router_group_b[i],
                       router_expert_w[i], router_expert_b[i], moe_w_gate, moe_w_up, moe_w_down, i)
    return x2.reshape(batch, seq, d)
```
